```python
import math
import jax, jax.numpy as jnp
from jax import lax
import numpy as np

D_MODEL = 1024
BATCH = 2
SEQ = 8192
DEPTH = 2

NORM_EPS = 1e-6
MASK_VALUE = -1e30
D_FF = 2816
N_BRANCHES = 4
BRANCH_WIDTH = 512

HG_HEADS = 4
HG_KDIM = 128
HG_VDIM = BRANCH_WIDTH // HG_HEADS
HG_CHUNK = 64

S5_GROUP = 16
S5_GROUPS = BRANCH_WIDTH // S5_GROUP
S5_STATE = 64
S5_DT_MIN = 1e-3
S5_DT_MAX = 1e-1

CONV_CH = BRANCH_WIDTH
CONV_WIDTH = 31

ATT_HEAD_DIM = 128
ATT_CONFIGS = ((128, 1), (512, 4), (2048, 16))
ATT_GROUPS = len(ATT_CONFIGS)
ATT_HEADS_PER_GROUP = BRANCH_WIDTH // ATT_HEAD_DIM
ATT_HEADS = ATT_GROUPS * ATT_HEADS_PER_GROUP
ATT_QBLOCK = 128
ROPE_THETA = 500000.0
ROPE_DIM = ATT_HEAD_DIM // 4

IN_SPLITS = (HG_HEADS * HG_KDIM, HG_HEADS * HG_KDIM, HG_HEADS * HG_VDIM, HG_HEADS * HG_VDIM,
             BRANCH_WIDTH, 2 * CONV_CH, 3 * ATT_HEADS * ATT_HEAD_DIM, N_BRANCHES * D_MODEL)
IN_WIDTH = int(sum(IN_SPLITS))
IN_SPLIT_POINTS = tuple(int(v) for v in np.cumsum(IN_SPLITS)[:-1])

kernel_name = 'hybrid_gated_parallel_mixers'


def _rmsnorm(x, gain):
    xf = x.astype(jnp.float32)
    y = xf * lax.rsqrt(jnp.mean(xf * xf, axis=-1, keepdims=True) + NORM_EPS)
    return (y * gain.astype(jnp.float32)).astype(x.dtype)


def _layernorm(x, gain, bias):
    xf = x.astype(jnp.float32)
    xc = xf - jnp.mean(xf, axis=-1, keepdims=True)
    var = jnp.mean(xc * xc, axis=-1, keepdims=True)
    y = xc * lax.rsqrt(var + NORM_EPS) * gain.astype(jnp.float32) + bias.astype(jnp.float32)
    return y.astype(x.dtype)


def _swiglu(x, w_gate, w_up, w_down):
    return (jax.nn.silu(x @ w_gate) * (x @ w_up)) @ w_down


def _hgrn2(q, f, i, g, lb, gnorm):
    f32 = jnp.float32
    Bsz, L, _ = q.shape
    n_chunks = L // HG_CHUNK
    f = f.astype(f32)
    lb = lb.astype(f32)
    qf = jax.nn.silu(q.astype(f32)) * (HG_KDIM ** -0.5)
    kf = (1.0 - lb) * jax.nn.sigmoid(-f)
    logf = jnp.log(lb + (1.0 - lb) * jax.nn.sigmoid(f))

    def heads(t, d):
        return t.reshape(Bsz, n_chunks, HG_CHUNK, HG_HEADS, d).transpose(1, 0, 3, 2, 4)

    qs, ks, gs = heads(qf, HG_KDIM), heads(kf, HG_KDIM), heads(logf, HG_KDIM)
    vs = heads(i.astype(f32), HG_VDIM)
    causal = jnp.tril(jnp.ones((HG_CHUNK, HG_CHUNK), dtype=bool))[:, :, None]

    def chunk_step(state, inp):
        qc, kc, vc, gc = inp
        b = jnp.cumsum(gc, axis=2)
        o_inter = jnp.einsum('bhck,bhkv->bhcv', qc * jnp.exp(b), state)
        rel = b[:, :, :, None, :] - b[:, :, None, :, :]
        decay = jnp.where(causal, jnp.exp(jnp.minimum(rel, 0.0)), 0.0)
        att = jnp.einsum('bhtk,bhsk,bhtsk->bhts', qc, kc, decay)
        o_intra = jnp.einsum('bhts,bhsv->bhtv', att, vc)
        b_last = b[:, :, -1:, :]
        new_state = jnp.exp(b_last[:, :, 0, :])[..., None] * state + jnp.einsum(
            'bhsk,bhsv->bhkv', kc * jnp.exp(b_last - b), vc)
        return new_state, o_inter + o_intra

    s0 = jnp.zeros((Bsz, HG_HEADS, HG_KDIM, HG_VDIM), f32)
    _, o = lax.scan(chunk_step, s0, (qs, ks, vs, gs))
    o = o.transpose(1, 0, 3, 2, 4).reshape(Bsz, L, HG_HEADS, HG_VDIM)
    o = o * lax.rsqrt(jnp.mean(o * o, axis=-1, keepdims=True) + NORM_EPS)
    o = o * gnorm.astype(f32).reshape(HG_HEADS, HG_VDIM)
    o = o.reshape(Bsz, L, HG_HEADS * HG_VDIM) * jax.nn.silu(g.astype(f32))
    return o.astype(q.dtype)


def _s5(u, a_re, a_im, log_dt, b_re, b_im, c_re, c_im, d_skip, w_glu):
    f32 = jnp.float32
    Bsz, L, _ = u.shape
    uf = u.astype(f32)
    a_re = a_re.astype(f32)
    a_im = a_im.astype(f32)
    b_re, b_im, c_re, c_im = (t.astype(f32) for t in (b_re, b_im, c_re, c_im))
    dt = jnp.exp(log_dt.astype(f32))[:, None]
    mag = jnp.exp(a_re * dt)
    ab_re = mag * jnp.cos(a_im * dt)
    ab_im = mag * jnp.sin(a_im * dt)
    den = a_re * a_re + a_im * a_im
    zr = ((ab_re - 1.0) * a_re + ab_im * a_im) / den
    zi = (ab_im * a_re - (ab_re - 1.0) * a_im) / den
    bb_re = zr[..., None] * b_re - zi[..., None] * b_im
    bb_im = zr[..., None] * b_im + zi[..., None] * b_re
    ug = uf.reshape(Bsz, L, S5_GROUPS, S5_GROUP)
    bu_re = jnp.einsum('blgc,gpc->blgp', ug, bb_re)
    bu_im = jnp.einsum('blgc,gpc->blgp', ug, bb_im)
    abar_re = jnp.broadcast_to(ab_re, bu_re.shape)
    abar_im = jnp.broadcast_to(ab_im, bu_im.shape)

    def combine(e1, e2):
        a1r, a1i, b1r, b1i = e1
        a2r, a2i, b2r, b2i = e2
        return (a2r * a1r - a2i * a1i, a2r * a1i + a2i * a1r,
                a2r * b1r - a2i * b1i + b2r, a2r * b1i + a2i * b1r + b2i)

    _, _, xr, xi = lax.associative_scan(combine, (abar_re, abar_im, bu_re, bu_im), axis=1)
    y = jnp.einsum('blgp,gcp->blgc', xr, c_re) - jnp.einsum('blgp,gcp->blgc', xi, c_im)
    y = y.reshape(Bsz, L, BRANCH_WIDTH) + d_skip.astype(f32) * uf
    z = jax.nn.gelu(y)
    za, zg = jnp.split(z @ w_glu.astype(f32), 2, axis=-1)
    return (za * jax.nn.sigmoid(zg)).astype(u.dtype)


def _conformer_conv(u, conv_w, conv_b, ln_g, ln_b):
    a, b = jnp.split(u, 2, axis=-1)
    z = a * jax.nn.sigmoid(b)
    z = lax.conv_general_dilated(
        z, conv_w[:, None, :].astype(z.dtype), window_strides=(1,),
        padding=((CONV_WIDTH - 1, 0),), dimension_numbers=('NWC', 'WIO', 'NWC'),
        feature_group_count=CONV_CH) + conv_b
    z = _layernorm(z, ln_g, ln_b)
    return jax.nn.silu(z)


def _partial_rope(x, positions):
    half = ROPE_DIM // 2
    inv_freq = ROPE_THETA ** (-jnp.arange(half, dtype=jnp.float32) / half)
    ang = positions.astype(jnp.float32)[..., None] * inv_freq
    cos = jnp.cos(ang)[:, :, None, :]
    sin = jnp.sin(ang)[:, :, None, :]
    xr = x[..., :ROPE_DIM].astype(jnp.float32)
    x1, x2 = xr[..., :half], xr[..., half:]
    rot = jnp.concatenate([x1 * cos - x2 * sin, x2 * cos + x1 * sin], axis=-1)
    return jnp.concatenate([rot.astype(x.dtype), x[..., ROPE_DIM:]], axis=-1)


def _strided_window_attention(q, k, v, window, dilation):
    f32 = jnp.float32
    Bsz, L, H, Dh = q.shape
    span = window // dilation
    n_sub = L // dilation
    nb = -(-n_sub // ATT_QBLOCK)
    pad = nb * ATT_QBLOCK - n_sub

    def gather(t):
        t = t.reshape(Bsz, n_sub, dilation, H, Dh).transpose(0, 2, 3, 1, 4)
        t = jnp.pad(t, ((0, 0), (0, 0), (0, 0), (0, pad), (0, 0)))
        return t.reshape(Bsz, dilation, H, nb, ATT_QBLOCK, Dh)

    def with_prev(t):
        prev = jnp.pad(t, ((0, 0), (0, 0), (0, 0), (1, 0), (0, 0), (0, 0)))[:, :, :, :-1]
        return jnp.concatenate([prev, t], axis=4)

    qb = gather(q)
    kk = with_prev(gather(k))
    vv = with_prev(gather(v))
    s = jnp.einsum('brhnqe,brhnke->brhnqk', qb, kk,
                   preferred_element_type=f32) * (Dh ** -0.5)
    qi = jnp.arange(ATT_QBLOCK)[:, None]
    kj = jnp.arange(2 * ATT_QBLOCK)[None, :]
    rel = ATT_QBLOCK + qi - kj
    band = (rel >= 0) & (rel <= span)
    not_first = jnp.arange(nb)[:, None, None] > 0
    valid = band[None] & (not_first | (kj[None] >= ATT_QBLOCK))
    s = jnp.where(valid, s, MASK_VALUE)
    lse = jax.nn.logsumexp(s, axis=-1)
    p = jnp.where(valid, jnp.exp(s - lse[..., None]), 0.0)
    o = jnp.einsum('brhnqk,brhnke->brhnqe', p, vv.astype(f32))
    o = o.reshape(Bsz, dilation, H, nb * ATT_QBLOCK, Dh)[:, :, :, :n_sub]
    o = o.transpose(0, 3, 1, 2, 4).reshape(Bsz, L, H, Dh)
    lse = lse.reshape(Bsz, dilation, H, nb * ATT_QBLOCK)[:, :, :, :n_sub]
    lse = lse.transpose(0, 3, 1, 2).reshape(Bsz, L, H)
    return o, lse


def _dilated_attention(qkv, positions):
    Bsz, L, _ = qkv.shape
    qkv = qkv.reshape(Bsz, L, 3, ATT_HEADS, ATT_HEAD_DIM)
    q = _partial_rope(qkv[:, :, 0], positions)
    k = _partial_rope(qkv[:, :, 1], positions)
    v = qkv[:, :, 2]
    outs, lses = [], []
    for gi, (window, dilation) in enumerate(ATT_CONFIGS):
        sl = slice(gi * ATT_HEADS_PER_GROUP, (gi + 1) * ATT_HEADS_PER_GROUP)
        o, lse = _strided_window_attention(q[:, :, sl], k[:, :, sl], v[:, :, sl], window, dilation)
        outs.append(o)
        lses.append(lse)
    w = jax.nn.softmax(jnp.stack(lses, axis=0), axis=0)
    out = jnp.sum(w[..., None] * jnp.stack(outs, axis=0), axis=0)
    return out.reshape(Bsz, L, ATT_HEADS_PER_GROUP * ATT_HEAD_DIM).astype(qkv.dtype)


def setup_inputs(seed: int = 0) -> dict:
    key = jax.random.key(seed)
    ks = iter(jax.random.split(key, 48))
    f32 = jnp.float32

    def nrm(shape, scale):
        return scale * jax.random.normal(next(ks), shape, f32)

    def gain(shape):
        return 1.0 + nrm(shape, 0.01)

    x = jax.random.normal(next(ks), (BATCH, SEQ, D_MODEL), f32)
    offsets = jax.random.randint(next(ks), (BATCH, 1), 0, 4096)
    positions = (offsets + jnp.arange(SEQ)[None, :]).astype(jnp.int32)
    a_im_base = jnp.pi * jnp.arange(S5_STATE, dtype=f32)
    return {
        'x': x,
        'positions': positions,
        'ffn1_norm': gain((DEPTH, D_MODEL)),
        'ffn1_w_gate': nrm((DEPTH, D_MODEL, D_FF), D_MODEL ** -0.5),
        'ffn1_w_up': nrm((DEPTH, D_MODEL, D_FF), D_MODEL ** -0.5),
        'ffn1_w_down': nrm((DEPTH, D_FF, D_MODEL), D_FF ** -0.5),
        'mix_norm': gain((DEPTH, D_MODEL)),
        'w_in': nrm((DEPTH, D_MODEL, IN_WIDTH), D_MODEL ** -0.5),
        'hg_lb_logits': nrm((DEPTH, HG_HEADS * HG_KDIM), 0.1),
        'hg_gnorm': gain((DEPTH, HG_HEADS * HG_VDIM)),
        's5_a_re': -0.5 + nrm((DEPTH, S5_GROUPS, S5_STATE), 0.01),
        's5_a_im': a_im_base + nrm((DEPTH, S5_GROUPS, S5_STATE), 0.01),
        's5_log_dt': jax.random.uniform(next(ks), (DEPTH, S5_GROUPS), f32,
                                        math.log(S5_DT_MIN), math.log(S5_DT_MAX)),
        's5_b_re': nrm((DEPTH, S5_GROUPS, S5_STATE, S5_GROUP), (2 * S5_GROUP) ** -0.5),
        's5_b_im': nrm((DEPTH, S5_GROUPS, S5_STATE, S5_GROUP), (2 * S5_GROUP) ** -0.5),
        's5_c_re': nrm((DEPTH, S5_GROUPS, S5_GROUP, S5_STATE), (2 * S5_STATE) ** -0.5),
        's5_c_im': nrm((DEPTH, S5_GROUPS, S5_GROUP, S5_STATE), (2 * S5_STATE) ** -0.5),
        's5_d': nrm((DEPTH, BRANCH_WIDTH), 1.0),
        's5_w_glu': nrm((DEPTH, BRANCH_WIDTH, 2 * BRANCH_WIDTH), BRANCH_WIDTH ** -0.5),
        'conv_w': nrm((DEPTH, CONV_WIDTH, CONV_CH), CONV_WIDTH ** -0.5),
        'conv_b': nrm((DEPTH, CONV_CH), 0.01),
        'conv_ln_g': gain((DEPTH, CONV_CH)),
        'conv_ln_b': nrm((DEPTH, CONV_CH), 0.01),
        'w_branch': nrm((DEPTH, N_BRANCHES, BRANCH_WIDTH, D_MODEL), BRANCH_WIDTH ** -0.5),
        'w_out': nrm((DEPTH, D_MODEL, D_MODEL), D_MODEL ** -0.5),
        'ffn2_norm': gain((DEPTH, D_MODEL)),
        'ffn2_w_gate': nrm((DEPTH, D_MODEL, D_FF), D_MODEL ** -0.5),
        'ffn2_w_up': nrm((DEPTH, D_MODEL, D_FF), D_MODEL ** -0.5),
        'ffn2_w_down': nrm((DEPTH, D_FF, D_MODEL), D_FF ** -0.5),
        'final_norm': gain((D_MODEL,)),
    }


def reference(x, positions, ffn1_norm, ffn1_w_gate, ffn1_w_up, ffn1_w_down, mix_norm, w_in,
              hg_lb_logits, hg_gnorm, s5_a_re, s5_a_im, s5_log_dt, s5_b_re, s5_b_im,
              s5_c_re, s5_c_im, s5_d, s5_w_glu, conv_w, conv_b, conv_ln_g, conv_ln_b,
              w_branch, w_out, ffn2_norm, ffn2_w_gate, ffn2_w_up, ffn2_w_down, final_norm):
    Bsz, L, _ = x.shape
    lb_soft = jax.nn.softmax(hg_lb_logits.astype(jnp.float32), axis=0)
    lb_all = jnp.cumsum(lb_soft, axis=0) - lb_soft[0]
    for l in range(DEPTH):
        h = _rmsnorm(x, ffn1_norm[l])
        x = x + 0.5 * _swiglu(h, ffn1_w_gate[l], ffn1_w_up[l], ffn1_w_down[l])

        h = _rmsnorm(x, mix_norm[l])
        hq, hf, hi, hg, s5_in, conv_in, att_in, gate_in = jnp.split(
            h @ w_in[l], IN_SPLIT_POINTS, axis=-1)
        y_a = _hgrn2(hq, hf, hi, hg, lb_all[l], hg_gnorm[l])
        y_b = _s5(s5_in, s5_a_re[l], s5_a_im[l], s5_log_dt[l], s5_b_re[l], s5_b_im[l],
                  s5_c_re[l], s5_c_im[l], s5_d[l], s5_w_glu[l])
        y_c = _conformer_conv(conv_in, conv_w[l], conv_b[l], conv_ln_g[l], conv_ln_b[l])
        y_d = _dilated_attention(att_in, positions)
        gates = jax.nn.sigmoid(gate_in).reshape(Bsz, L, N_BRANCHES, D_MODEL)
        merged = gates[:, :, 0] * (y_a @ w_branch[l, 0])
        merged = merged + gates[:, :, 1] * (y_b @ w_branch[l, 1])
        merged = merged + gates[:, :, 2] * (y_c @ w_branch[l, 2])
        merged = merged + gates[:, :, 3] * (y_d @ w_branch[l, 3])
        x = x + merged @ w_out[l]

        h = _rmsnorm(x, ffn2_norm[l])
        x = x + 0.5 * _swiglu(h, ffn2_w_gate[l], ffn2_w_up[l], ffn2_w_down[l])
    return _rmsnorm(x, final_norm)
```

```python
import functools
import math

import numpy as np
import jax
import jax.numpy as jnp
from jax import lax
from jax.experimental import pallas as pl
from jax.experimental.pallas import tpu as pltpu

F32 = jnp.float32
BF16 = jnp.bfloat16

NORM_EPS = 1e-6
MASK_VALUE = -1e30
D_MODEL = 1024
D_FF = 2816
N_BRANCHES = 4
BRANCH_WIDTH = 512

HG_HEADS = 4
HG_KDIM = 128
HG_VDIM = 128
HG_CHUNK = 64
HG_SUB = 16

S5_GROUP = 16
S5_GROUPS = 32
S5_STATE = 64
S5_HALF_CH = 256
S5_HALF_STATES = (S5_GROUPS // 2) * S5_STATE
S5_ROWS = 8

CONV_WIDTH = 31
CONV_HALO = 32

ATT_HEAD_DIM = 128
ATT_CONFIGS = ((128, 1), (512, 4), (2048, 16))
ATT_HEADS_PER_GROUP = 4
ATT_HEADS = 12
ATT_QBLOCK = 128
ROPE_THETA = 500000.0
ROPE_DIM = 32
ROPE_HALF = 16

SLAB = 512
SLAB_HQ, SLAB_HF, SLAB_HI, SLAB_HG, SLAB_S5, SLAB_CONV_A, SLAB_CONV_B = 0, 1, 2, 3, 4, 5, 6
SLAB_Q, SLAB_K, SLAB_V = 7, 10, 13
SLAB_GATE = 16
N_SLABS = 24

VMEM_LIMIT = 56 * 1024 * 1024


def _params(sem):
    return pltpu.CompilerParams(dimension_semantics=sem, vmem_limit_bytes=VMEM_LIMIT)


def _rms(x):
    return x * lax.rsqrt(jnp.mean(x * x, axis=-1, keepdims=True) + NORM_EPS)


def _sigmoid(x):
    return jax.nn.sigmoid(x)


def _ffn_kernel(x_ref, gain_ref, wg_ref, wu_ref, wd_ref, *rest, n_ff, final):
    if final:
        fgain_ref, o_ref, h_ref, acc_ref = rest
    else:
        o_ref, h_ref, acc_ref = rest
    j = pl.program_id(1)

    @pl.when(j == 0)
    def _():
        h_ref[...] = (_rms(x_ref[...]) * gain_ref[...]).astype(BF16)
        acc_ref[...] = jnp.zeros_like(acc_ref)

    h = h_ref[...]
    g = jnp.dot(h, wg_ref[...], preferred_element_type=F32)
    u = jnp.dot(h, wu_ref[...], preferred_element_type=F32)
    a = (g * _sigmoid(g) * u).astype(BF16)
    acc_ref[...] += jnp.dot(a, wd_ref[...], preferred_element_type=F32)

    @pl.when(j == n_ff - 1)
    def _():
        y = x_ref[...] + 0.5 * acc_ref[...]
        if final:
            y = _rms(y) * fgain_ref[...]
        o_ref[...] = y


def _ffn(x, gain, wg, wu, wd, final_gain=None, *, tm=512, tf=256):
    T, D = x.shape
    FF = wg.shape[1]
    n_ff = FF // tf
    final = final_gain is not None
    in_specs = [
        pl.BlockSpec((tm, D), lambda i, j: (i, 0)),
        pl.BlockSpec((1, D), lambda i, j: (0, 0)),
        pl.BlockSpec((D, tf), lambda i, j: (0, j)),
        pl.BlockSpec((D, tf), lambda i, j: (0, j)),
        pl.BlockSpec((tf, D), lambda i, j: (j, 0)),
    ]
    args = [x, gain.reshape(1, D), wg, wu, wd]
    if final:
        in_specs.append(pl.BlockSpec((1, D), lambda i, j: (0, 0)))
        args.append(final_gain.reshape(1, D))
    return pl.pallas_call(
        functools.partial(_ffn_kernel, n_ff=n_ff, final=final),
        grid=(T // tm, n_ff),
        in_specs=in_specs,
        out_specs=pl.BlockSpec((tm, D), lambda i, j: (i, 0)),
        out_shape=jax.ShapeDtypeStruct((T, D), F32),
        scratch_shapes=[pltpu.VMEM((tm, D), BF16), pltpu.VMEM((tm, D), F32)],
        compiler_params=_params(("parallel", "arbitrary")),
        name="ffn_final" if final else "ffn",
    )(*args)


def _proj_kernel(x_ref, gain_ref, w_ref, o_ref, h_ref):
    @pl.when(pl.program_id(1) == 0)
    def _():
        h_ref[...] = (_rms(x_ref[...]) * gain_ref[...]).astype(BF16)

    o_ref[...] = jnp.dot(h_ref[...], w_ref[...], preferred_element_type=F32).astype(o_ref.dtype)


def _proj(x, gain, w_in, *, tm=1024):
    T, D = x.shape
    return pl.pallas_call(
        _proj_kernel,
        grid=(T // tm, N_SLABS),
        in_specs=[
            pl.BlockSpec((tm, D), lambda i, j: (i, 0)),
            pl.BlockSpec((1, D), lambda i, j: (0, 0)),
            pl.BlockSpec((D, SLAB), lambda i, j: (0, j)),
        ],
        out_specs=pl.BlockSpec((None, tm, SLAB), lambda i, j: (j, i, 0)),
        out_shape=jax.ShapeDtypeStruct((N_SLABS, T, SLAB), BF16),
        scratch_shapes=[pltpu.VMEM((tm, D), BF16)],
        compiler_params=_params(("parallel", "arbitrary")),
        name="in_proj",
    )(x, gain.reshape(1, D), w_in)


def _slab_spec(slab, rows):
    return pl.BlockSpec((None, None, rows, SLAB), lambda b, t: (slab, b, t, 0))


def _split3_bf16(x):
    hi = x.astype(BF16)
    r = x - hi.astype(F32)
    mid = r.astype(BF16)
    lo = (r - mid.astype(F32)).astype(BF16)
    return hi, mid, lo


def _hgrn_kernel(q_ref, f_ref, i_ref, g_ref, lb_ref, gn_ref, o_ref, st_ref, *, n_chunks):
    C, S = HG_CHUNK, HG_SUB

    @pl.when(pl.program_id(1) == 0)
    def _():
        st_ref[...] = jnp.zeros_like(st_ref)

    lb = lb_ref[...]
    gn = gn_ref[...]
    row = lax.broadcasted_iota(jnp.int32, (C, C), 0)
    col = lax.broadcasted_iota(jnp.int32, (C, C), 1)
    tri = (col <= row).astype(BF16)
    row_in_sub = lax.broadcasted_iota(jnp.int32, (C, 1), 0) % S
    nt = (((1,), (1,)), ((), ()))
    tn = (((0,), (0,)), ((), ()))

    def chunk(c, carry):
        r0 = pl.multiple_of(c * C, C)
        q = q_ref[pl.ds(r0, C), :].astype(F32)
        f = f_ref[pl.ds(r0, C), :].astype(F32)
        v = i_ref[pl.ds(r0, C), :].astype(F32)
        g = g_ref[pl.ds(r0, C), :].astype(F32)
        v16 = v.astype(BF16)

        qf = q * _sigmoid(q) * (HG_KDIM ** -0.5)
        kf = (1.0 - lb) * _sigmoid(-f)
        logf = jnp.log(lb + (1.0 - lb) * _sigmoid(f))
        b = sum(jnp.dot(tri, p, preferred_element_type=F32) for p in _split3_bf16(logf))
        b_last = b[C - 1:C, :]

        qd = (qf * jnp.exp(b)).astype(BF16)
        kd_last = (kf * jnp.exp(b_last - b)).astype(BF16)

        diag = [jnp.zeros((C, HG_VDIM), F32) for _ in range(HG_HEADS)]
        for lag in range(S):
            if lag == 0:
                p = qf * kf
                vs = v
            else:
                bs = pltpu.roll(b, lag, 0)
                ks = pltpu.roll(kf, lag, 0)
                vs = pltpu.roll(v, lag, 0)
                p = qf * ks * jnp.exp(jnp.minimum(b - bs, 0.0))
            for h in range(HG_HEADS):
                hs = slice(h * HG_KDIM, (h + 1) * HG_KDIM)
                a = jnp.sum(p[:, hs], axis=-1, keepdims=True)
                a = jnp.where(row_in_sub >= lag, a, 0.0)
                diag[h] = diag[h] + a * vs[:, hs]

        outs = []
        for h in range(HG_HEADS):
            hs = slice(h * HG_KDIM, (h + 1) * HG_KDIM)
            st = st_ref[h]
            o = lax.dot_general(qd[:, hs], st.astype(BF16), nt, preferred_element_type=F32) + diag[h]
            off = [jnp.zeros((S, HG_VDIM), F32)]
            for i in range(1, C // S):
                r = b[S * i - 1:S * i, hs]
                qi = (qf[S * i:S * (i + 1), hs] * jnp.exp(b[S * i:S * (i + 1), hs] - r)).astype(BF16)
                kj = (kf[:S * i, hs] * jnp.exp(r - b[:S * i, hs])).astype(BF16)
                att = lax.dot_general(qi, kj, nt, preferred_element_type=F32)
                off.append(jnp.dot(att.astype(BF16), v16[:S * i, hs], preferred_element_type=F32))
            o = o + jnp.concatenate(off, axis=0)
            st_ref[h] = jnp.exp(b_last[:, hs]) * st + lax.dot_general(
                v16[:, hs], kd_last[:, hs], tn, preferred_element_type=F32)
            o = o * lax.rsqrt(jnp.mean(o * o, axis=-1, keepdims=True) + NORM_EPS)
            outs.append(o)
        o = jnp.concatenate(outs, axis=-1) * gn * (g * _sigmoid(g))
        o_ref[pl.ds(r0, C), :] = o.astype(o_ref.dtype)
        return carry

    lax.fori_loop(0, n_chunks, chunk, 0)


def _hgrn(proj, lb, gnorm, *, tt=256):
    _, B, L, _ = proj.shape
    W = HG_HEADS * HG_KDIM
    return pl.pallas_call(
        functools.partial(_hgrn_kernel, n_chunks=tt // HG_CHUNK),
        grid=(B, L // tt),
        in_specs=[
            _slab_spec(SLAB_HQ, tt), _slab_spec(SLAB_HF, tt), _slab_spec(SLAB_HI, tt), _slab_spec(SLAB_HG, tt),
            pl.BlockSpec((1, W), lambda b, t: (0, 0)),
            pl.BlockSpec((1, W), lambda b, t: (0, 0)),
        ],
        out_specs=pl.BlockSpec((None, tt, W), lambda b, t: (b, t, 0)),
        out_shape=jax.ShapeDtypeStruct((B, L, W), BF16),
        scratch_shapes=[pltpu.VMEM((HG_HEADS, HG_VDIM, HG_KDIM), F32)],
        compiler_params=_params(("parallel", "arbitrary")),
        name="hgrn2",
    )(proj, proj, proj, proj, lb.reshape(1, W), gnorm.reshape(1, W))


def _s5_kernel(u_ref, wb_ref, wc_ref, pw_ref, d_ref, wglu_ref, o_ref, x_ref, carry_ref, *, tt):
    NS = S5_HALF_STATES
    R = S5_ROWS
    LC = 512

    @pl.when(pl.program_id(1) == 0)
    def _():
        carry_ref[...] = jnp.zeros_like(carry_ref)

    u16 = u_ref[...]
    for hf in range(2):
        x_ref[hf] = jnp.dot(u16[:, hf * S5_HALF_CH:(hf + 1) * S5_HALF_CH], wb_ref[hf],
                            preferred_element_type=F32)

    rows = lax.broadcasted_iota(jnp.int32, (R, LC), 0)
    for hf in range(2):
        for lc in range(NS // LC):
            re = slice(lc * LC, (lc + 1) * LC)
            im = slice(NS + lc * LC, NS + (lc + 1) * LC)
            pr = pw_ref[hf, :, re]
            pi = pw_ref[hf, :, im]
            steps = []
            for s in (1, 2, 4):
                keep = rows >= s
                steps.append((s, jnp.where(keep, pr[s - 1:s, :], 0.0), jnp.where(keep, pi[s - 1:s, :], 0.0)))

            def block(k, carry, hf=hf, re=re, im=im, pr=pr, pi=pi, steps=steps):
                cr, ci = carry
                r0 = pl.multiple_of(k * R, R)
                xr = x_ref[hf, pl.ds(r0, R), re]
                xi = x_ref[hf, pl.ds(r0, R), im]
                for s, ar, ai in steps:
                    sr = pltpu.roll(xr, s, 0)
                    si = pltpu.roll(xi, s, 0)
                    xr, xi = xr + (ar * sr - ai * si), xi + (ar * si + ai * sr)
                xr, xi = xr + (pr * cr - pi * ci), xi + (pr * ci + pi * cr)
                x_ref[hf, pl.ds(r0, R), re] = xr
                x_ref[hf, pl.ds(r0, R), im] = xi
                return xr[R - 1:R, :], xi[R - 1:R, :]

            cr, ci = lax.fori_loop(0, tt // R, block, (carry_ref[hf:hf + 1, re], carry_ref[hf:hf + 1, im]))
            carry_ref[hf:hf + 1, re] = cr
            carry_ref[hf:hf + 1, im] = ci

    y = jnp.concatenate(
        [jnp.dot(x_ref[hf].astype(BF16), wc_ref[hf], preferred_element_type=F32) for hf in range(2)], axis=-1)
    y = y + d_ref[...] * u16.astype(F32)
    z = jax.nn.gelu(y).astype(BF16)
    zz = jnp.dot(z, wglu_ref[...], preferred_element_type=F32)
    o_ref[...] = (zz[:, :BRANCH_WIDTH] * _sigmoid(zz[:, BRANCH_WIDTH:])).astype(o_ref.dtype)


def _s5(proj, wb, wc, pw, d_skip, w_glu, *, tt=256):
    _, B, L, _ = proj.shape
    W = BRANCH_WIDTH
    NS2 = 2 * S5_HALF_STATES
    return pl.pallas_call(
        functools.partial(_s5_kernel, tt=tt),
        grid=(B, L // tt),
        in_specs=[
            _slab_spec(SLAB_S5, tt),
            pl.BlockSpec((2, S5_HALF_CH, NS2), lambda b, t: (0, 0, 0)),
            pl.BlockSpec((2, NS2, S5_HALF_CH), lambda b, t: (0, 0, 0)),
            pl.BlockSpec((2, S5_ROWS, NS2), lambda b, t: (0, 0, 0)),
            pl.BlockSpec((1, W), lambda b, t: (0, 0)),
            pl.BlockSpec((W, 2 * W), lambda b, t: (0, 0)),
        ],
        out_specs=pl.BlockSpec((None, tt, W), lambda b, t: (b, t, 0)),
        out_shape=jax.ShapeDtypeStruct((B, L, W), BF16),
        scratch_shapes=[pltpu.VMEM((2, tt, NS2), F32), pltpu.VMEM((2, NS2), F32)],
        compiler_params=_params(("parallel", "arbitrary")),
        name="s5",
    )(proj, wb, wc, pw, d_skip.reshape(1, W), w_glu)


def _s5_tables(a_re, a_im, log_dt, b_re, b_im, c_re, c_im):
    dt = jnp.exp(log_dt)[:, None]
    mag = jnp.exp(a_re * dt)
    ab_re = mag * jnp.cos(a_im * dt)
    ab_im = mag * jnp.sin(a_im * dt)
    den = a_re * a_re + a_im * a_im
    zr = ((ab_re - 1.0) * a_re + ab_im * a_im) / den
    zi = (ab_im * a_re - (ab_re - 1.0) * a_im) / den
    bb_re = zr[..., None] * b_re - zi[..., None] * b_im
    bb_im = zr[..., None] * b_im + zi[..., None] * b_re
    GH = S5_GROUPS // 2
    eye = jnp.eye(GH, dtype=F32)

    def bdiag_in(m):
        m = m.reshape(2, GH, S5_STATE, S5_GROUP)
        return jnp.einsum('hgpc,gk->hgckp', m, eye).reshape(2, GH * S5_GROUP, GH * S5_STATE)

    def bdiag_out(m):
        m = m.reshape(2, GH, S5_GROUP, S5_STATE)
        return jnp.einsum('hgcp,gk->hgpkc', m, eye).reshape(2, GH * S5_STATE, GH * S5_GROUP)

    wb = jnp.concatenate([bdiag_in(bb_re), bdiag_in(bb_im)], axis=-1).astype(BF16)
    wc = jnp.concatenate([bdiag_out(c_re), -bdiag_out(c_im)], axis=1).astype(BF16)
    n = jnp.arange(1, S5_ROWS + 1, dtype=F32)[:, None, None]
    pmag = jnp.exp(n * (a_re * dt)[None])
    pang = n * (a_im * dt)[None]
    pr = (pmag * jnp.cos(pang)).reshape(S5_ROWS, 2, S5_HALF_STATES)
    pi = (pmag * jnp.sin(pang)).reshape(S5_ROWS, 2, S5_HALF_STATES)
    pw = jnp.concatenate([pr, pi], axis=-1).transpose(1, 0, 2)
    return wb, wc, pw


def _conv_kernel(a_ref, b_ref, w_ref, cb_ref, lg_ref, lbias_ref, o_ref, z_ref, *, tt, rb):
    H = CONV_HALO

    @pl.when(pl.program_id(1) == 0)
    def _():
        z_ref[0:H, :] = jnp.zeros((H, BRANCH_WIDTH), F32)

    z_ref[H:H + tt, :] = a_ref[...].astype(F32) * _sigmoid(b_ref[...].astype(F32))
    cb = cb_ref[...]
    lg = lg_ref[...]
    lbias = lbias_ref[...]

    def block(k, carry):
        r0 = pl.multiple_of(k * rb, rb)
        win = z_ref[pl.ds(r0, rb + H), :]
        acc = jnp.zeros((rb, BRANCH_WIDTH), F32)
        for c in range(8):
            shifted = win if c == 0 else pltpu.roll(win, rb + H - c, 0)
            for off in range(c, H + 1, 8):
                w = off - (H - CONV_WIDTH + 1)
                if 0 <= w < CONV_WIDTH:
                    acc = acc + shifted[off - c:off - c + rb, :] * w_ref[w:w + 1, :]
        acc = acc + cb
        xc = acc - jnp.mean(acc, axis=-1, keepdims=True)
        var = jnp.mean(xc * xc, axis=-1, keepdims=True)
        y = xc * lax.rsqrt(var + NORM_EPS) * lg + lbias
        o_ref[pl.ds(r0, rb), :] = (y * _sigmoid(y)).astype(o_ref.dtype)
        return carry

    lax.fori_loop(0, tt // rb, block, 0)
    z_ref[0:H, :] = z_ref[tt:tt + H, :]


def _conv(proj, conv_w, conv_b, ln_g, ln_b, *, tt=512, rb=32):
    _, B, L, _ = proj.shape
    W = BRANCH_WIDTH
    w_pad = jnp.zeros((CONV_HALO, W), F32).at[:CONV_WIDTH].set(conv_w)
    vec = lambda: pl.BlockSpec((1, W), lambda b, t: (0, 0))
    return pl.pallas_call(
        functools.partial(_conv_kernel, tt=tt, rb=rb),
        grid=(B, L // tt),
        in_specs=[_slab_spec(SLAB_CONV_A, tt), _slab_spec(SLAB_CONV_B, tt),
                  pl.BlockSpec((CONV_HALO, W), lambda b, t: (0, 0)), vec(), vec(), vec()],
        out_specs=pl.BlockSpec((None, tt, W), lambda b, t: (b, t, 0)),
        out_shape=jax.ShapeDtypeStruct((B, L, W), BF16),
        scratch_shapes=[pltpu.VMEM((tt + CONV_HALO, W), F32)],
        compiler_params=_params(("parallel", "arbitrary")),
        name="conformer_conv",
    )(proj, proj, w_pad, conv_b.reshape(1, W), ln_g.reshape(1, W), ln_b.reshape(1, W))


def _rope_table_kernel(pos_ref, inv_ref, cos_ref, sin_ref):
    ang = pos_ref[...] * inv_ref[...]
    lane = lax.broadcasted_iota(jnp.int32, ang.shape, 1)
    cos_ref[...] = jnp.cos(ang)
    sin_ref[...] = jnp.where(lane < ROPE_HALF, -jnp.sin(ang), jnp.sin(ang))


def _rope_tables(positions, *, tm=1024):
    T = positions.size
    inv = ROPE_THETA ** (-jnp.arange(ROPE_HALF, dtype=F32) / ROPE_HALF)
    inv = jnp.concatenate([inv, inv, jnp.zeros((ATT_HEAD_DIM - ROPE_DIM,), F32)]).reshape(1, ATT_HEAD_DIM)
    pos = positions.astype(F32).reshape(T, 1)
    return pl.pallas_call(
        _rope_table_kernel,
        grid=(T // tm,),
        in_specs=[pl.BlockSpec((tm, 1), lambda i: (i, 0)), pl.BlockSpec((1, ATT_HEAD_DIM), lambda i: (0, 0))],
        out_specs=[pl.BlockSpec((tm, ATT_HEAD_DIM), lambda i: (i, 0))] * 2,
        out_shape=[jax.ShapeDtypeStruct((T, ATT_HEAD_DIM), F32)] * 2,
        compiler_params=_params(("parallel",)),
        name="rope_tables",
    )(pos, inv)


def _rope(x, cos, sin):
    lane = lax.broadcasted_iota(jnp.int32, x.shape, 1)
    partner = jnp.where(lane < ROPE_HALF, pltpu.roll(x, ATT_HEAD_DIM - ROPE_HALF, 1), pltpu.roll(x, ROPE_HALF, 1))
    return x * cos + partner * sin


def _att_kernel(q_ref, kc_ref, kp_ref, vc_ref, vp_ref, cq_ref, sq_ref, cp_ref, sp_ref, o_ref, lse_ref):
    Q = ATT_QBLOCK
    n = pl.program_id(2)
    cos_q, sin_q, cos_p, sin_p = cq_ref[...], sq_ref[...], cp_ref[...], sp_ref[...]
    qi = lax.broadcasted_iota(jnp.int32, (Q, 2 * Q), 0)
    kj = lax.broadcasted_iota(jnp.int32, (Q, 2 * Q), 1)
    rel = Q + qi - kj
    valid = (rel >= 0) & (rel <= Q) & ((n > 0) | (kj >= Q))
    nt = (((1,), (1,)), ((), ()))
    outs, lses = [], []
    for h in range(ATT_HEADS_PER_GROUP):
        hs = slice(h * ATT_HEAD_DIM, (h + 1) * ATT_HEAD_DIM)
        q = _rope(q_ref[:, hs].astype(F32), cos_q, sin_q).astype(BF16)
        kc = _rope(kc_ref[:, hs].astype(F32), cos_q, sin_q).astype(BF16)
        kp = _rope(kp_ref[:, hs].astype(F32), cos_p, sin_p).astype(BF16)
        k = jnp.concatenate([kp, kc], axis=0)
        v = jnp.concatenate([vp_ref[:, hs], vc_ref[:, hs]], axis=0)
        s = lax.dot_general(q, k, nt, preferred_element_type=F32) * (ATT_HEAD_DIM ** -0.5)
        s = jnp.where(valid, s, MASK_VALUE)
        m = jnp.max(s, axis=-1, keepdims=True)
        p = jnp.exp(s - m)
        l = jnp.sum(p, axis=-1, keepdims=True)
        o = jnp.dot(p.astype(BF16), v, preferred_element_type=F32) / l
        outs.append(o)
        lses.append(jnp.broadcast_to(m + jnp.log(l), (Q, ATT_HEAD_DIM)))
    o_ref[...] = jnp.concatenate(outs, axis=-1).astype(o_ref.dtype)
    lse_ref[...] = jnp.concatenate(lses, axis=-1)


def _attention_group(proj, cos_t, sin_t, gi, dilation):
    _, B, L, _ = proj.shape
    d = dilation
    W = ATT_HEADS_PER_GROUP * ATT_HEAD_DIM
    n_sub = L // d
    nb = n_sub // ATT_QBLOCK
    pv = proj.reshape(N_SLABS, B, n_sub, d * SLAB)
    cv = cos_t.reshape(B, n_sub, d * ATT_HEAD_DIM)
    sv = sin_t.reshape(B, n_sub, d * ATT_HEAD_DIM)
    Q = ATT_QBLOCK

    def slab(s, prev):
        if prev:
            return pl.BlockSpec((None, None, Q, SLAB), lambda b, r, n: (s, b, jnp.maximum(n - 1, 0), r))
        return pl.BlockSpec((None, None, Q, SLAB), lambda b, r, n: (s, b, n, r))

    def tab(prev):
        if prev:
            return pl.BlockSpec((None, Q, ATT_HEAD_DIM), lambda b, r, n: (b, jnp.maximum(n - 1, 0), r))
        return pl.BlockSpec((None, Q, ATT_HEAD_DIM), lambda b, r, n: (b, n, r))

    out_spec = pl.BlockSpec((None, Q, W), lambda b, r, n: (b, n, r))
    o, lse = pl.pallas_call(
        _att_kernel,
        grid=(B, d, nb),
        in_specs=[slab(SLAB_Q + gi, False), slab(SLAB_K + gi, False), slab(SLAB_K + gi, True),
                  slab(SLAB_V + gi, False), slab(SLAB_V + gi, True),
                  tab(False), tab(False), tab(True), tab(True)],
        out_specs=[out_spec, out_spec],
        out_shape=[jax.ShapeDtypeStruct((B, n_sub, d * W), BF16), jax.ShapeDtypeStruct((B, n_sub, d * W), F32)],
        compiler_params=_params(("parallel", "parallel", "arbitrary")),
        name=f"dilated_attention_g{gi}",
    )(pv, pv, pv, pv, pv, cv, sv, cv, sv)
    return o.reshape(B * L, W), lse.reshape(B * L, W)


def _merge_kernel(x_ref, ya_ref, yb_ref, yc_ref, o1_ref, o2_ref, o3_ref, l1_ref, l2_ref, l3_ref,
                  g0_ref, g1_ref, g2_ref, g3_ref, g4_ref, g5_ref, g6_ref, g7_ref, wb_ref, wo_ref, out_ref):
    l1, l2, l3 = l1_ref[...], l2_ref[...], l3_ref[...]
    m = jnp.maximum(jnp.maximum(l1, l2), l3)
    e1, e2, e3 = jnp.exp(l1 - m), jnp.exp(l2 - m), jnp.exp(l3 - m)
    yd = (e1 * o1_ref[...].astype(F32) + e2 * o2_ref[...].astype(F32) + e3 * o3_ref[...].astype(F32)) / (e1 + e2 + e3)
    ys = (ya_ref[...], yb_ref[...], yc_ref[...], yd.astype(BF16))
    gates = ((g0_ref, g1_ref), (g2_ref, g3_ref), (g4_ref, g5_ref), (g6_ref, g7_ref))
    merged = None
    for k in range(N_BRANCHES):
        gate = _sigmoid(jnp.concatenate([gates[k][0][...], gates[k][1][...]], axis=-1).astype(F32))
        term = gate * jnp.dot(ys[k], wb_ref[k], preferred_element_type=F32)
        merged = term if merged is None else merged + term
    out_ref[...] = x_ref[...] + jnp.dot(merged.astype(BF16), wo_ref[...], preferred_element_type=F32)


def _merge(x, ya, yb, yc, att, proj2d, w_branch, w_out, *, tm=256):
    T, D = x.shape
    W = BRANCH_WIDTH
    (o1, l1), (o2, l2), (o3, l3) = att
    row = lambda: pl.BlockSpec((tm, W), lambda i: (i, 0))
    gate = lambda s: pl.BlockSpec((None, tm, SLAB), lambda i: (s, i, 0))
    return pl.pallas_call(
        _merge_kernel,
        grid=(T // tm,),
        in_specs=[pl.BlockSpec((tm, D), lambda i: (i, 0))] + [row() for _ in range(9)]
                 + [gate(SLAB_GATE + s) for s in range(8)]
                 + [pl.BlockSpec((N_BRANCHES, W, D), lambda i: (0, 0, 0)), pl.BlockSpec((D, D), lambda i: (0, 0))],
        out_specs=pl.BlockSpec((tm, D), lambda i: (i, 0)),
        out_shape=jax.ShapeDtypeStruct((T, D), F32),
        compiler_params=_params(("parallel",)),
        name="gated_merge",
    )(x, ya, yb, yc, o1, o2, o3, l1, l2, l3, *([proj2d] * 8), w_branch, w_out)


def kernel(x, positions, ffn1_norm, ffn1_w_gate, ffn1_w_up, ffn1_w_down, mix_norm, w_in, hg_lb_logits, hg_gnorm, s5_a_re, s5_a_im, s5_log_dt, s5_b_re, s5_b_im, s5_c_re, s5_c_im, s5_d, s5_w_glu, conv_w, conv_b, conv_ln_g, conv_ln_b, w_branch, w_out, ffn2_norm, ffn2_w_gate, ffn2_w_up, ffn2_w_down, final_norm):
    B, L, D = x.shape
    T = B * L
    depth = w_in.shape[0]
    lb_soft = jax.nn.softmax(hg_lb_logits.astype(F32), axis=0)
    lb_all = jnp.cumsum(lb_soft, axis=0) - lb_soft[0]
    cos_t, sin_t = _rope_tables(positions)
    bf = lambda w: w.astype(BF16)

    xt = x.reshape(T, D)
    for l in range(depth):
        xt = _ffn(xt, ffn1_norm[l], bf(ffn1_w_gate[l]), bf(ffn1_w_up[l]), bf(ffn1_w_down[l]))
        proj2d = _proj(xt, mix_norm[l], bf(w_in[l]))
        proj = proj2d.reshape(N_SLABS, B, L, SLAB)
        ya = _hgrn(proj, lb_all[l], hg_gnorm[l]).reshape(T, BRANCH_WIDTH)
        wb, wc, pw = _s5_tables(s5_a_re[l], s5_a_im[l], s5_log_dt[l], s5_b_re[l], s5_b_im[l], s5_c_re[l], s5_c_im[l])
        yb = _s5(proj, wb, wc, pw, s5_d[l], bf(s5_w_glu[l])).reshape(T, BRANCH_WIDTH)
        yc = _conv(proj, conv_w[l], conv_b[l], conv_ln_g[l], conv_ln_b[l]).reshape(T, BRANCH_WIDTH)
        att = [_attention_group(proj, cos_t, sin_t, gi, dil) for gi, (_, dil) in enumerate(ATT_CONFIGS)]
        xt = _merge(xt, ya, yb, yc, att, proj2d, bf(w_branch[l]), bf(w_out[l]))
        last = l == depth - 1
        xt = _ffn(xt, ffn2_norm[l], bf(ffn2_w_gate[l]), bf(ffn2_w_up[l]), bf(ffn2_w_down[l]),
                  final_norm if last else None)
    return xt.reshape(B, L, D)
```

```python
import functools
import math

import numpy as np
import jax
import jax.numpy as jnp
from jax import lax
from jax.experimental import pallas as pl
from jax.experimental.pallas import tpu as pltpu

F32 = jnp.float32
BF16 = jnp.bfloat16

NORM_EPS = 1e-6
MASK_VALUE = -1e30
D_MODEL = 1024
D_FF = 2816
N_BRANCHES = 4
BRANCH_WIDTH = 512

HG_HEADS = 4
HG_KDIM = 128
HG_VDIM = 128
HG_CHUNK = 64
HG_SUB = 16

S5_GROUP = 16
S5_GROUPS = 32
S5_STATE = 64
S5_HALF_CH = 256
S5_HALF_STATES = (S5_GROUPS // 2) * S5_STATE
S5_ROWS = 8

CONV_WIDTH = 31
CONV_HALO = 32

ATT_HEAD_DIM = 128
ATT_CONFIGS = ((128, 1), (512, 4), (2048, 16))
ATT_HEADS_PER_GROUP = 4
ATT_HEADS = 12
ATT_QBLOCK = 128
ROPE_THETA = 500000.0
ROPE_DIM = 32
ROPE_HALF = 16

SLAB = 512
SLAB_HQ, SLAB_HF, SLAB_HI, SLAB_HG, SLAB_S5, SLAB_CONV_A, SLAB_CONV_B = 0, 1, 2, 3, 4, 5, 6
W_SLAB_Q = 7
W_SLAB_GATE = 16
SLAB_GATE = 7
N_SLABS = 15

VMEM_LIMIT = 56 * 1024 * 1024


def _params(sem):
    return pltpu.CompilerParams(dimension_semantics=sem, vmem_limit_bytes=VMEM_LIMIT)


def _rms(x):
    return x * lax.rsqrt(jnp.mean(x * x, axis=-1, keepdims=True) + NORM_EPS)


def _sigmoid(x):
    return jax.nn.sigmoid(x)


def _column_tiles(w, width):
    K, N = w.shape
    return w.reshape(K, N // width, width).transpose(1, 0, 2)


def _ffn_kernel(x_ref, gain_ref, wg_ref, wu_ref, wd_ref, *rest, n_ff, final):
    if final:
        fgain_ref, o_ref, h_ref, acc_ref = rest
    else:
        o_ref, h_ref, acc_ref = rest
    j = pl.program_id(1)

    @pl.when(j == 0)
    def _():
        h_ref[...] = (_rms(x_ref[...]) * gain_ref[...]).astype(BF16)
        acc_ref[...] = jnp.zeros_like(acc_ref)

    h = h_ref[...]
    g = jnp.dot(h, wg_ref[...], preferred_element_type=F32)
    u = jnp.dot(h, wu_ref[...], preferred_element_type=F32)
    a = (g * _sigmoid(g) * u).astype(BF16)
    acc_ref[...] += jnp.dot(a, wd_ref[...], preferred_element_type=F32)

    @pl.when(j == n_ff - 1)
    def _():
        y = x_ref[...] + 0.5 * acc_ref[...]
        if final:
            y = _rms(y) * fgain_ref[...]
        o_ref[...] = y


def _ffn(x, gain, wg, wu, wd, final_gain=None, *, tm=512, tf=1408):
    T, D = x.shape
    FF = wg.shape[1]
    n_ff = FF // tf
    final = final_gain is not None
    in_specs = [
        pl.BlockSpec((tm, D), lambda i, j: (i, 0)),
        pl.BlockSpec((1, D), lambda i, j: (0, 0)),
        pl.BlockSpec((None, D, tf), lambda i, j: (j, 0, 0)),
        pl.BlockSpec((None, D, tf), lambda i, j: (j, 0, 0)),
        pl.BlockSpec((tf, D), lambda i, j: (j, 0)),
    ]
    args = [x, gain.reshape(1, D), _column_tiles(wg, tf), _column_tiles(wu, tf), wd]
    if final:
        in_specs.append(pl.BlockSpec((1, D), lambda i, j: (0, 0)))
        args.append(final_gain.reshape(1, D))
    return pl.pallas_call(
        functools.partial(_ffn_kernel, n_ff=n_ff, final=final),
        grid=(T // tm, n_ff),
        in_specs=in_specs,
        out_specs=pl.BlockSpec((tm, D), lambda i, j: (i, 0)),
        out_shape=jax.ShapeDtypeStruct((T, D), F32),
        scratch_shapes=[pltpu.VMEM((tm, D), BF16), pltpu.VMEM((tm, D), F32)],
        compiler_params=_params(("parallel", "arbitrary")),
        name="ffn_final" if final else "ffn",
    )(*args)


def _proj_kernel(x_ref, gain_ref, w_ref, o_ref, h_ref):
    @pl.when(pl.program_id(1) == 0)
    def _():
        h_ref[...] = (_rms(x_ref[...]) * gain_ref[...]).astype(BF16)

    o_ref[...] = jnp.dot(h_ref[...], w_ref[...], preferred_element_type=F32).astype(o_ref.dtype)


def _proj(x, gain, w_in, *, tm=1024):
    T, D = x.shape
    return pl.pallas_call(
        _proj_kernel,
        grid=(T // tm, N_SLABS),
        in_specs=[
            pl.BlockSpec((tm, D), lambda i, j: (i, 0)),
            pl.BlockSpec((1, D), lambda i, j: (0, 0)),
            pl.BlockSpec((None, D, SLAB),
                         lambda i, j: (jnp.where(j < SLAB_GATE, j, j + (W_SLAB_GATE - SLAB_GATE)), 0, 0)),
        ],
        out_specs=pl.BlockSpec((None, tm, SLAB), lambda i, j: (j, i, 0)),
        out_shape=jax.ShapeDtypeStruct((N_SLABS, T, SLAB), BF16),
        scratch_shapes=[pltpu.VMEM((tm, D), BF16)],
        compiler_params=_params(("parallel", "arbitrary")),
        name="in_proj",
    )(x, gain.reshape(1, D), w_in)


def _slab_spec(slab, rows):
    return pl.BlockSpec((None, None, rows, SLAB), lambda b, t: (slab, b, t, 0))


def _split3_bf16(x):
    hi = x.astype(BF16)
    r = x - hi.astype(F32)
    mid = r.astype(BF16)
    lo = (r - mid.astype(F32)).astype(BF16)
    return hi, mid, lo


def _hgrn_kernel(q_ref, f_ref, i_ref, g_ref, lb_ref, gn_ref, o_ref, st_ref, *, n_chunks):
    C, S = HG_CHUNK, HG_SUB

    @pl.when(pl.program_id(1) == 0)
    def _():
        st_ref[...] = jnp.zeros_like(st_ref)

    lb = lb_ref[...]
    gn = gn_ref[...]
    row = lax.broadcasted_iota(jnp.int32, (C, C), 0)
    col = lax.broadcasted_iota(jnp.int32, (C, C), 1)
    tri = (col <= row).astype(BF16)
    row_in_sub = lax.broadcasted_iota(jnp.int32, (C, 1), 0) % S
    nt = (((1,), (1,)), ((), ()))
    tn = (((0,), (0,)), ((), ()))

    def chunk(c, carry):
        r0 = pl.multiple_of(c * C, C)
        q = q_ref[pl.ds(r0, C), :].astype(F32)
        f = f_ref[pl.ds(r0, C), :].astype(F32)
        v = i_ref[pl.ds(r0, C), :].astype(F32)
        g = g_ref[pl.ds(r0, C), :].astype(F32)
        v16 = v.astype(BF16)

        qf = q * _sigmoid(q) * (HG_KDIM ** -0.5)
        kf = (1.0 - lb) * _sigmoid(-f)
        logf = jnp.log(lb + (1.0 - lb) * _sigmoid(f))
        b = sum(jnp.dot(tri, p, preferred_element_type=F32) for p in _split3_bf16(logf))
        b_last = b[C - 1:C, :]

        qd = (qf * jnp.exp(b)).astype(BF16)
        kd_last = (kf * jnp.exp(b_last - b)).astype(BF16)

        diag = [jnp.zeros((C, HG_VDIM), F32) for _ in range(HG_HEADS)]
        for lag in range(S):
            if lag == 0:
                p = qf * kf
                vs = v
            else:
                bs = pltpu.roll(b, lag, 0)
                ks = pltpu.roll(kf, lag, 0)
                vs = pltpu.roll(v, lag, 0)
                p = qf * ks * jnp.exp(jnp.minimum(b - bs, 0.0))
            for h in range(HG_HEADS):
                hs = slice(h * HG_KDIM, (h + 1) * HG_KDIM)
                a = jnp.sum(p[:, hs], axis=-1, keepdims=True)
                a = jnp.where(row_in_sub >= lag, a, 0.0)
                diag[h] = diag[h] + a * vs[:, hs]

        outs = []
        for h in range(HG_HEADS):
            hs = slice(h * HG_KDIM, (h + 1) * HG_KDIM)
            st = st_ref[h]
            o = lax.dot_general(qd[:, hs], st.astype(BF16), nt, preferred_element_type=F32) + diag[h]
            off = [jnp.zeros((S, HG_VDIM), F32)]
            for i in range(1, C // S):
                r = b[S * i - 1:S * i, hs]
                qi = (qf[S * i:S * (i + 1), hs] * jnp.exp(b[S * i:S * (i + 1), hs] - r)).astype(BF16)
                kj = (kf[:S * i, hs] * jnp.exp(r - b[:S * i, hs])).astype(BF16)
                att = lax.dot_general(qi, kj, nt, preferred_element_type=F32)
                off.append(jnp.dot(att.astype(BF16), v16[:S * i, hs], preferred_element_type=F32))
            o = o + jnp.concatenate(off, axis=0)
            st_ref[h] = jnp.exp(b_last[:, hs]) * st + lax.dot_general(
                v16[:, hs], kd_last[:, hs], tn, preferred_element_type=F32)
            o = o * lax.rsqrt(jnp.mean(o * o, axis=-1, keepdims=True) + NORM_EPS)
            outs.append(o)
        o = jnp.concatenate(outs, axis=-1) * gn * (g * _sigmoid(g))
        o_ref[pl.ds(r0, C), :] = o.astype(o_ref.dtype)
        return carry

    lax.fori_loop(0, n_chunks, chunk, 0)


def _hgrn(proj, lb, gnorm, *, tt=256):
    _, B, L, _ = proj.shape
    W = HG_HEADS * HG_KDIM
    return pl.pallas_call(
        functools.partial(_hgrn_kernel, n_chunks=tt // HG_CHUNK),
        grid=(B, L // tt),
        in_specs=[
            _slab_spec(SLAB_HQ, tt), _slab_spec(SLAB_HF, tt), _slab_spec(SLAB_HI, tt), _slab_spec(SLAB_HG, tt),
            pl.BlockSpec((1, W), lambda b, t: (0, 0)),
            pl.BlockSpec((1, W), lambda b, t: (0, 0)),
        ],
        out_specs=pl.BlockSpec((None, tt, W), lambda b, t: (b, t, 0)),
        out_shape=jax.ShapeDtypeStruct((B, L, W), BF16),
        scratch_shapes=[pltpu.VMEM((HG_HEADS, HG_VDIM, HG_KDIM), F32)],
        compiler_params=_params(("parallel", "arbitrary")),
        name="hgrn2",
    )(proj, proj, proj, proj, lb.reshape(1, W), gnorm.reshape(1, W))


def _s5_kernel(u_ref, wb_ref, wc_ref, pw_ref, d_ref, wglu_ref, o_ref, x_ref, carry_ref, *, tt):
    NS = S5_HALF_STATES
    R = S5_ROWS
    LC = 512

    @pl.when(pl.program_id(1) == 0)
    def _():
        carry_ref[...] = jnp.zeros_like(carry_ref)

    u16 = u_ref[...]
    for hf in range(2):
        x_ref[hf] = jnp.dot(u16[:, hf * S5_HALF_CH:(hf + 1) * S5_HALF_CH], wb_ref[hf],
                            preferred_element_type=F32)

    rows = lax.broadcasted_iota(jnp.int32, (R, LC), 0)
    for hf in range(2):
        for lc in range(NS // LC):
            re = slice(lc * LC, (lc + 1) * LC)
            im = slice(NS + lc * LC, NS + (lc + 1) * LC)
            pr = pw_ref[hf, :, re]
            pi = pw_ref[hf, :, im]
            steps = []
            for s in (1, 2, 4):
                keep = rows >= s
                steps.append((s, jnp.where(keep, pr[s - 1:s, :], 0.0), jnp.where(keep, pi[s - 1:s, :], 0.0)))

            def block(k, carry, hf=hf, re=re, im=im, pr=pr, pi=pi, steps=steps):
                cr, ci = carry
                r0 = pl.multiple_of(k * R, R)
                xr = x_ref[hf, pl.ds(r0, R), re]
                xi = x_ref[hf, pl.ds(r0, R), im]
                for s, ar, ai in steps:
                    sr = pltpu.roll(xr, s, 0)
                    si = pltpu.roll(xi, s, 0)
                    xr, xi = xr + (ar * sr - ai * si), xi + (ar * si + ai * sr)
                xr, xi = xr + (pr * cr - pi * ci), xi + (pr * ci + pi * cr)
                x_ref[hf, pl.ds(r0, R), re] = xr
                x_ref[hf, pl.ds(r0, R), im] = xi
                return xr[R - 1:R, :], xi[R - 1:R, :]

            cr, ci = lax.fori_loop(0, tt // R, block, (carry_ref[hf:hf + 1, re], carry_ref[hf:hf + 1, im]))
            carry_ref[hf:hf + 1, re] = cr
            carry_ref[hf:hf + 1, im] = ci

    y = jnp.concatenate(
        [jnp.dot(x_ref[hf].astype(BF16), wc_ref[hf], preferred_element_type=F32) for hf in range(2)], axis=-1)
    y = y + d_ref[...] * u16.astype(F32)
    z = jax.nn.gelu(y).astype(BF16)
    zz = jnp.dot(z, wglu_ref[...], preferred_element_type=F32)
    o_ref[...] = (zz[:, :BRANCH_WIDTH] * _sigmoid(zz[:, BRANCH_WIDTH:])).astype(o_ref.dtype)


def _s5(proj, wb, wc, pw, d_skip, w_glu, *, tt=256):
    _, B, L, _ = proj.shape
    W = BRANCH_WIDTH
    NS2 = 2 * S5_HALF_STATES
    return pl.pallas_call(
        functools.partial(_s5_kernel, tt=tt),
        grid=(B, L // tt),
        in_specs=[
            _slab_spec(SLAB_S5, tt),
            pl.BlockSpec((2, S5_HALF_CH, NS2), lambda b, t: (0, 0, 0)),
            pl.BlockSpec((2, NS2, S5_HALF_CH), lambda b, t: (0, 0, 0)),
            pl.BlockSpec((2, S5_ROWS, NS2), lambda b, t: (0, 0, 0)),
            pl.BlockSpec((1, W), lambda b, t: (0, 0)),
            pl.BlockSpec((W, 2 * W), lambda b, t: (0, 0)),
        ],
        out_specs=pl.BlockSpec((None, tt, W), lambda b, t: (b, t, 0)),
        out_shape=jax.ShapeDtypeStruct((B, L, W), BF16),
        scratch_shapes=[pltpu.VMEM((2, tt, NS2), F32), pltpu.VMEM((2, NS2), F32)],
        compiler_params=_params(("parallel", "arbitrary")),
        name="s5",
    )(proj, wb, wc, pw, d_skip.reshape(1, W), w_glu)


def _s5_tables(a_re, a_im, log_dt, b_re, b_im, c_re, c_im):
    dt = jnp.exp(log_dt)[:, None]
    mag = jnp.exp(a_re * dt)
    ab_re = mag * jnp.cos(a_im * dt)
    ab_im = mag * jnp.sin(a_im * dt)
    den = a_re * a_re + a_im * a_im
    zr = ((ab_re - 1.0) * a_re + ab_im * a_im) / den
    zi = (ab_im * a_re - (ab_re - 1.0) * a_im) / den
    bb_re = zr[..., None] * b_re - zi[..., None] * b_im
    bb_im = zr[..., None] * b_im + zi[..., None] * b_re
    GH = S5_GROUPS // 2
    eye = jnp.eye(GH, dtype=F32)

    def bdiag_in(m):
        m = m.reshape(2, GH, S5_STATE, S5_GROUP)
        return jnp.einsum('hgpc,gk->hgckp', m, eye).reshape(2, GH * S5_GROUP, GH * S5_STATE)

    def bdiag_out(m):
        m = m.reshape(2, GH, S5_GROUP, S5_STATE)
        return jnp.einsum('hgcp,gk->hgpkc', m, eye).reshape(2, GH * S5_STATE, GH * S5_GROUP)

    wb = jnp.concatenate([bdiag_in(bb_re), bdiag_in(bb_im)], axis=-1).astype(BF16)
    wc = jnp.concatenate([bdiag_out(c_re), -bdiag_out(c_im)], axis=1).astype(BF16)
    n = jnp.arange(1, S5_ROWS + 1, dtype=F32)[:, None, None]
    pmag = jnp.exp(n * (a_re * dt)[None])
    pang = n * (a_im * dt)[None]
    pr = (pmag * jnp.cos(pang)).reshape(S5_ROWS, 2, S5_HALF_STATES)
    pi = (pmag * jnp.sin(pang)).reshape(S5_ROWS, 2, S5_HALF_STATES)
    pw = jnp.concatenate([pr, pi], axis=-1).transpose(1, 0, 2)
    return wb, wc, pw


def _conv_kernel(a_ref, b_ref, w_ref, cb_ref, lg_ref, lbias_ref, o_ref, z_ref, *, tt, rb):
    H = CONV_HALO

    @pl.when(pl.program_id(1) == 0)
    def _():
        z_ref[0:H, :] = jnp.zeros((H, BRANCH_WIDTH), F32)

    z_ref[H:H + tt, :] = a_ref[...].astype(F32) * _sigmoid(b_ref[...].astype(F32))
    cb = cb_ref[...]
    lg = lg_ref[...]
    lbias = lbias_ref[...]

    def block(k, carry):
        r0 = pl.multiple_of(k * rb, rb)
        win = z_ref[pl.ds(r0, rb + H), :]
        acc = jnp.zeros((rb, BRANCH_WIDTH), F32)
        for c in range(8):
            shifted = win if c == 0 else pltpu.roll(win, rb + H - c, 0)
            for off in range(c, H + 1, 8):
                w = off - (H - CONV_WIDTH + 1)
                if 0 <= w < CONV_WIDTH:
                    acc = acc + shifted[off - c:off - c + rb, :] * w_ref[w:w + 1, :]
        acc = acc + cb
        xc = acc - jnp.mean(acc, axis=-1, keepdims=True)
        var = jnp.mean(xc * xc, axis=-1, keepdims=True)
        y = xc * lax.rsqrt(var + NORM_EPS) * lg + lbias
        o_ref[pl.ds(r0, rb), :] = (y * _sigmoid(y)).astype(o_ref.dtype)
        return carry

    lax.fori_loop(0, tt // rb, block, 0)
    z_ref[0:H, :] = z_ref[tt:tt + H, :]


def _conv(proj, conv_w, conv_b, ln_g, ln_b, *, tt=512, rb=32):
    _, B, L, _ = proj.shape
    W = BRANCH_WIDTH
    w_pad = jnp.zeros((CONV_HALO, W), F32).at[:CONV_WIDTH].set(conv_w)
    vec = lambda: pl.BlockSpec((1, W), lambda b, t: (0, 0))
    return pl.pallas_call(
        functools.partial(_conv_kernel, tt=tt, rb=rb),
        grid=(B, L // tt),
        in_specs=[_slab_spec(SLAB_CONV_A, tt), _slab_spec(SLAB_CONV_B, tt),
                  pl.BlockSpec((CONV_HALO, W), lambda b, t: (0, 0)), vec(), vec(), vec()],
        out_specs=pl.BlockSpec((None, tt, W), lambda b, t: (b, t, 0)),
        out_shape=jax.ShapeDtypeStruct((B, L, W), BF16),
        scratch_shapes=[pltpu.VMEM((tt + CONV_HALO, W), F32)],
        compiler_params=_params(("parallel", "arbitrary")),
        name="conformer_conv",
    )(proj, proj, w_pad, conv_b.reshape(1, W), ln_g.reshape(1, W), ln_b.reshape(1, W))


def _rope_table_kernel(pos_ref, inv_ref, cos_ref, sin_ref):
    ang = pos_ref[...] * inv_ref[...]
    lane = lax.broadcasted_iota(jnp.int32, ang.shape, 1)
    cos_ref[...] = jnp.cos(ang)
    sin_ref[...] = jnp.where(lane < ROPE_HALF, -jnp.sin(ang), jnp.sin(ang))


def _rope_tables(positions, *, tm=1024):
    T = positions.size
    inv = ROPE_THETA ** (-jnp.arange(ROPE_HALF, dtype=F32) / ROPE_HALF)
    inv = jnp.concatenate([inv, inv, jnp.zeros((ATT_HEAD_DIM - ROPE_DIM,), F32)]).reshape(1, ATT_HEAD_DIM)
    pos = positions.astype(F32).reshape(T, 1)
    return pl.pallas_call(
        _rope_table_kernel,
        grid=(T // tm,),
        in_specs=[pl.BlockSpec((tm, 1), lambda i: (i, 0)), pl.BlockSpec((1, ATT_HEAD_DIM), lambda i: (0, 0))],
        out_specs=[pl.BlockSpec((tm, ATT_HEAD_DIM), lambda i: (i, 0))] * 2,
        out_shape=[jax.ShapeDtypeStruct((T, ATT_HEAD_DIM), F32)] * 2,
        compiler_params=_params(("parallel",)),
        name="rope_tables",
    )(pos, inv)


def _rope(x, cos, sin):
    lane = lax.broadcasted_iota(jnp.int32, x.shape, 1)
    partner = jnp.where(lane < ROPE_HALF, pltpu.roll(x, ATT_HEAD_DIM - ROPE_HALF, 1), pltpu.roll(x, ROPE_HALF, 1))
    return x * cos + partner * sin


def _qkv_kernel(x_ref, gain_ref, w_ref, cos_ref, sin_ref, o_ref, h_ref, s_ref, *, d):
    j = pl.program_id(1)
    tm = x_ref.shape[0]

    @pl.when(j == 0)
    def _():
        h_ref[...] = (_rms(x_ref[...]) * gain_ref[...]).astype(BF16)

    res = jnp.dot(h_ref[...], w_ref[...], preferred_element_type=F32)

    heads = [slice(h * ATT_HEAD_DIM, (h + 1) * ATT_HEAD_DIM) for h in range(ATT_HEADS_PER_GROUP)]

    @pl.when(j < 2)
    def _():
        cos, sin = cos_ref[...], sin_ref[...]
        for h, hs in enumerate(heads):
            s_ref[h] = _rope(res[:, hs], cos, sin)

    @pl.when(j == 2)
    def _():
        for h, hs in enumerate(heads):
            s_ref[h] = res[:, hs]

    for r in range(d):
        for h, hs in enumerate(heads):
            o_ref[r, :, hs] = s_ref[h, pl.ds(r, tm // d, stride=d), :].astype(o_ref.dtype)


def _qkv(x, gain, w_in, cos_t, sin_t, gi, d, B, *, tm=1024):
    T, D = x.shape
    L = T // B
    tiles = L // tm
    W = ATT_HEADS_PER_GROUP * ATT_HEAD_DIM
    return pl.pallas_call(
        functools.partial(_qkv_kernel, d=d),
        grid=(T // tm, 3),
        in_specs=[
            pl.BlockSpec((tm, D), lambda i, j: (i, 0)),
            pl.BlockSpec((1, D), lambda i, j: (0, 0)),
            pl.BlockSpec((None, D, SLAB), lambda i, j: (W_SLAB_Q + gi + 3 * j, 0, 0)),
            pl.BlockSpec((tm, ATT_HEAD_DIM), lambda i, j: (i, 0)),
            pl.BlockSpec((tm, ATT_HEAD_DIM), lambda i, j: (i, 0)),
        ],
        out_specs=pl.BlockSpec((None, None, d, tm // d, W), lambda i, j: (j, i // tiles, 0, i % tiles, 0)),
        out_shape=jax.ShapeDtypeStruct((3, B, d, L // d, W), BF16),
        scratch_shapes=[pltpu.VMEM((tm, D), BF16), pltpu.VMEM((ATT_HEADS_PER_GROUP, tm, ATT_HEAD_DIM), F32)],
        compiler_params=_params(("parallel", "arbitrary")),
        name=f"qkv_proj_g{gi}",
    )(x, gain.reshape(1, D), w_in, cos_t, sin_t)


def _att_kernel(q_ref, kc_ref, kp_ref, vc_ref, vp_ref, o_ref, lse_ref, *, d):
    Q = ATT_QBLOCK
    n = pl.program_id(1)
    qi = lax.broadcasted_iota(jnp.int32, (Q, 2 * Q), 0)
    kj = lax.broadcasted_iota(jnp.int32, (Q, 2 * Q), 1)
    rel = Q + qi - kj
    valid = (rel >= 0) & (rel <= Q) & ((n > 0) | (kj >= Q))
    head_of_lane = lax.broadcasted_iota(jnp.int32, (Q, ATT_HEAD_DIM), 1) // (ATT_HEAD_DIM // ATT_HEADS_PER_GROUP)
    nt = (((1,), (1,)), ((), ()))

    def residue(r, carry):
        outs = []
        lse = jnp.zeros((Q, ATT_HEAD_DIM), F32)
        for h in range(ATT_HEADS_PER_GROUP):
            hs = slice(h * ATT_HEAD_DIM, (h + 1) * ATT_HEAD_DIM)
            k = jnp.concatenate([kp_ref[r, :, hs], kc_ref[r, :, hs]], axis=0)
            v = jnp.concatenate([vp_ref[r, :, hs], vc_ref[r, :, hs]], axis=0)
            s = lax.dot_general(q_ref[r, :, hs], k, nt, preferred_element_type=F32) * (ATT_HEAD_DIM ** -0.5)
            s = jnp.where(valid, s, MASK_VALUE)
            m = jnp.max(s, axis=-1, keepdims=True)
            p = jnp.exp(s - m)
            l = jnp.sum(p, axis=-1, keepdims=True)
            outs.append(jnp.dot(p.astype(BF16), v, preferred_element_type=F32) / l)
            lse = jnp.where(head_of_lane == h, m + jnp.log(l), lse)
        o_ref[r] = jnp.concatenate(outs, axis=-1).astype(o_ref.dtype)
        lse_ref[r] = lse
        return carry

    if d <= 4:
        for r in range(d):
            residue(r, 0)
    else:
        lax.fori_loop(0, d, residue, 0)


def _attention_group(qkv, gi):
    _, B, d, n_sub, W = qkv.shape
    Q = ATT_QBLOCK

    def blk(s, prev):
        if prev:
            return pl.BlockSpec((None, None, d, Q, W), lambda b, n: (s, b, 0, jnp.maximum(n - 1, 0), 0))
        return pl.BlockSpec((None, None, d, Q, W), lambda b, n: (s, b, 0, n, 0))

    return pl.pallas_call(
        functools.partial(_att_kernel, d=d),
        grid=(B, n_sub // Q),
        in_specs=[blk(0, False), blk(1, False), blk(1, True), blk(2, False), blk(2, True)],
        out_specs=[pl.BlockSpec((None, d, Q, W), lambda b, n: (b, 0, n, 0)),
                   pl.BlockSpec((None, d, Q, ATT_HEAD_DIM), lambda b, n: (b, 0, n, 0))],
        out_shape=[jax.ShapeDtypeStruct((B, d, n_sub, W), BF16),
                   jax.ShapeDtypeStruct((B, d, n_sub, ATT_HEAD_DIM), F32)],
        compiler_params=_params(("parallel", "arbitrary")),
        name=f"dilated_attention_g{gi}",
    )(qkv, qkv, qkv, qkv, qkv)


def _merge_kernel(x_ref, ya_ref, yb_ref, yc_ref, o1_ref, o2_ref, o3_ref, l1_ref, l2_ref, l3_ref,
                  g0_ref, g1_ref, g2_ref, g3_ref, g4_ref, g5_ref, g6_ref, g7_ref, wb_ref, wo_ref, out_ref,
                  os_ref, ls_ref):
    for g, (o_ref, l_ref) in enumerate(((o1_ref, l1_ref), (o2_ref, l2_ref), (o3_ref, l3_ref))):
        d, rows = o_ref.shape[0], o_ref.shape[1]
        for r in range(d):
            o = o_ref[r].astype(F32)
            for h in range(ATT_HEADS_PER_GROUP):
                os_ref[g, h, pl.ds(r, rows, stride=d), :] = o[:, h * ATT_HEAD_DIM:(h + 1) * ATT_HEAD_DIM]
            ls_ref[g, pl.ds(r, rows, stride=d), :] = l_ref[r]
    lanes = ATT_HEAD_DIM // ATT_HEADS_PER_GROUP
    parts = []
    for h in range(ATT_HEADS_PER_GROUP):
        l1, l2, l3 = (ls_ref[g, :, h * lanes:h * lanes + 1] for g in range(3))
        m = jnp.maximum(jnp.maximum(l1, l2), l3)
        e1, e2, e3 = jnp.exp(l1 - m), jnp.exp(l2 - m), jnp.exp(l3 - m)
        parts.append((e1 * os_ref[0, h] + e2 * os_ref[1, h] + e3 * os_ref[2, h]) / (e1 + e2 + e3))
    yd = jnp.concatenate(parts, axis=-1)
    ys = (ya_ref[...], yb_ref[...], yc_ref[...], yd.astype(BF16))
    gates = ((g0_ref, g1_ref), (g2_ref, g3_ref), (g4_ref, g5_ref), (g6_ref, g7_ref))
    merged = None
    for k in range(N_BRANCHES):
        gate = _sigmoid(jnp.concatenate([gates[k][0][...], gates[k][1][...]], axis=-1).astype(F32))
        term = gate * jnp.dot(ys[k], wb_ref[k], preferred_element_type=F32)
        merged = term if merged is None else merged + term
    out_ref[...] = x_ref[...] + jnp.dot(merged.astype(BF16), wo_ref[...], preferred_element_type=F32)


def _merge(x, ya, yb, yc, att, proj, w_branch, w_out, *, tm=256):
    B, L, D = x.shape
    W = BRANCH_WIDTH
    (o1, l1), (o2, l2), (o3, l3) = att
    row = lambda: pl.BlockSpec((None, tm, W), lambda b, t: (b, t, 0))

    def res(a):
        d, last = a.shape[1], a.shape[3]
        return pl.BlockSpec((None, d, tm // d, last), lambda b, t: (b, 0, t, 0))

    return pl.pallas_call(
        _merge_kernel,
        grid=(B, L // tm),
        in_specs=[pl.BlockSpec((None, tm, D), lambda b, t: (b, t, 0))] + [row() for _ in range(3)]
                 + [res(a) for a in (o1, o2, o3, l1, l2, l3)]
                 + [_slab_spec(SLAB_GATE + s, tm) for s in range(8)]
                 + [pl.BlockSpec((N_BRANCHES, W, D), lambda b, t: (0, 0, 0)), pl.BlockSpec((D, D), lambda b, t: (0, 0))],
        out_specs=pl.BlockSpec((None, tm, D), lambda b, t: (b, t, 0)),
        out_shape=jax.ShapeDtypeStruct((B, L, D), F32),
        scratch_shapes=[pltpu.VMEM((3, ATT_HEADS_PER_GROUP, tm, ATT_HEAD_DIM), F32),
                        pltpu.VMEM((3, tm, ATT_HEAD_DIM), F32)],
        compiler_params=_params(("parallel", "parallel")),
        name="gated_merge",
    )(x, ya, yb, yc, o1, o2, o3, l1, l2, l3, *([proj] * 8), w_branch, w_out)


def kernel(x, positions, ffn1_norm, ffn1_w_gate, ffn1_w_up, ffn1_w_down, mix_norm, w_in, hg_lb_logits, hg_gnorm, s5_a_re, s5_a_im, s5_log_dt, s5_b_re, s5_b_im, s5_c_re, s5_c_im, s5_d, s5_w_glu, conv_w, conv_b, conv_ln_g, conv_ln_b, w_branch, w_out, ffn2_norm, ffn2_w_gate, ffn2_w_up, ffn2_w_down, final_norm):
    B, L, D = x.shape
    T = B * L
    depth = w_in.shape[0]
    lb_soft = jax.nn.softmax(hg_lb_logits.astype(F32), axis=0)
    lb_all = jnp.cumsum(lb_soft, axis=0) - lb_soft[0]
    cos_t, sin_t = _rope_tables(positions)
    bf = lambda w: w.astype(BF16)

    xt = x.reshape(T, D)
    for l in range(depth):
        xt = _ffn(xt, ffn1_norm[l], bf(ffn1_w_gate[l]), bf(ffn1_w_up[l]), bf(ffn1_w_down[l]))
        w_in16 = _column_tiles(bf(w_in[l]), SLAB)
        proj = _proj(xt, mix_norm[l], w_in16).reshape(N_SLABS, B, L, SLAB)
        ya = _hgrn(proj, lb_all[l], hg_gnorm[l])
        wb, wc, pw = _s5_tables(s5_a_re[l], s5_a_im[l], s5_log_dt[l], s5_b_re[l], s5_b_im[l], s5_c_re[l], s5_c_im[l])
        yb = _s5(proj, wb, wc, pw, s5_d[l], bf(s5_w_glu[l]))
        yc = _conv(proj, conv_w[l], conv_b[l], conv_ln_g[l], conv_ln_b[l])
        att = [_attention_group(_qkv(xt, mix_norm[l], w_in16, cos_t, sin_t, gi, dil, B), gi)
               for gi, (_, dil) in enumerate(ATT_CONFIGS)]
        xt = _merge(xt.reshape(B, L, D), ya, yb, yc, att, proj, bf(w_branch[l]), bf(w_out[l])).reshape(T, D)
        last = l == depth - 1
        xt = _ffn(xt, ffn2_norm[l], bf(ffn2_w_gate[l]), bf(ffn2_w_up[l]), bf(ffn2_w_down[l]),
                  final_norm if last else None)
    return xt.reshape(B, L, D)
```

```python
import functools
import math

import numpy as np
import jax
import jax.numpy as jnp
from jax import lax
from jax.experimental import pallas as pl
from jax.experimental.pallas import tpu as pltpu

F32 = jnp.float32
BF16 = jnp.bfloat16

NORM_EPS = 1e-6
MASK_VALUE = -1e30
D_MODEL = 1024
D_FF = 2816
N_BRANCHES = 4
BRANCH_WIDTH = 512

HG_HEADS = 4
HG_KDIM = 128
HG_VDIM = 128
HG_CHUNK = 64
HG_SUB = 16

S5_GROUP = 16
S5_GROUPS = 32
S5_STATE = 64
S5_HALF_CH = 256
S5_HALF_STATES = (S5_GROUPS // 2) * S5_STATE
S5_ROWS = 8

CONV_WIDTH = 31
CONV_HALO = 32

ATT_HEAD_DIM = 128
ATT_CONFIGS = ((128, 1), (512, 4), (2048, 16))
ATT_HEADS_PER_GROUP = 4
ATT_HEADS = 12
ATT_QBLOCK = 128
ROPE_THETA = 500000.0
ROPE_DIM = 32
ROPE_HALF = 16

SLAB = 512
SLAB_HQ, SLAB_HF, SLAB_HI, SLAB_HG, SLAB_S5, SLAB_CONV_A, SLAB_CONV_B = 0, 1, 2, 3, 4, 5, 6
W_SLAB_Q = 7
W_SLAB_GATE = 16
SLAB_GATE = 7
N_SLABS = 15

VMEM_LIMIT = 56 * 1024 * 1024


def _params(sem):
    return pltpu.CompilerParams(dimension_semantics=sem, vmem_limit_bytes=VMEM_LIMIT)


def _rms(x):
    return x * lax.rsqrt(jnp.mean(x * x, axis=-1, keepdims=True) + NORM_EPS)


def _sigmoid(x):
    return jax.nn.sigmoid(x)


def _ffn_kernel(x_ref, gain_ref, wg_ref, wu_ref, wd_ref, *rest, n_ff, final):
    if final:
        fgain_ref, o_ref, h_ref, acc_ref = rest
    else:
        o_ref, h_ref, acc_ref = rest
    j = pl.program_id(1)

    @pl.when(j == 0)
    def _():
        h_ref[...] = (_rms(x_ref[...]) * gain_ref[...]).astype(BF16)
        acc_ref[...] = jnp.zeros_like(acc_ref)

    h = h_ref[...]
    g = jnp.dot(h, wg_ref[...], preferred_element_type=F32)
    u = jnp.dot(h, wu_ref[...], preferred_element_type=F32)
    a = (g * _sigmoid(g) * u).astype(BF16)
    acc_ref[...] += jnp.dot(a, wd_ref[...], preferred_element_type=F32)

    @pl.when(j == n_ff - 1)
    def _():
        y = x_ref[...] + 0.5 * acc_ref[...]
        if final:
            y = _rms(y) * fgain_ref[...]
        o_ref[...] = y


def _ffn(x, gain, wg, wu, wd, final_gain=None, *, tm=512, tf=1408):
    T, D = x.shape
    FF = wg.shape[1]
    n_ff = FF // tf
    final = final_gain is not None
    in_specs = [
        pl.BlockSpec((tm, D), lambda i, j: (i, 0)),
        pl.BlockSpec((1, D), lambda i, j: (0, 0)),
        pl.BlockSpec((D, tf), lambda i, j: (0, j)),
        pl.BlockSpec((D, tf), lambda i, j: (0, j)),
        pl.BlockSpec((tf, D), lambda i, j: (j, 0)),
    ]
    args = [x, gain.reshape(1, D), wg, wu, wd]
    if final:
        in_specs.append(pl.BlockSpec((1, D), lambda i, j: (0, 0)))
        args.append(final_gain.reshape(1, D))
    return pl.pallas_call(
        functools.partial(_ffn_kernel, n_ff=n_ff, final=final),
        grid=(T // tm, n_ff),
        in_specs=in_specs,
        out_specs=pl.BlockSpec((tm, D), lambda i, j: (i, 0)),
        out_shape=jax.ShapeDtypeStruct((T, D), F32),
        scratch_shapes=[pltpu.VMEM((tm, D), BF16), pltpu.VMEM((tm, D), F32)],
        compiler_params=_params(("parallel", "arbitrary")),
        name="ffn_final" if final else "ffn",
    )(*args)


def _proj_kernel(x_ref, gain_ref, w_ref, o_ref, h_ref, *, per_step):
    @pl.when(pl.program_id(1) == 0)
    def _():
        h_ref[...] = (_rms(x_ref[...]) * gain_ref[...]).astype(BF16)

    h = h_ref[...]
    for s in range(per_step):
        o_ref[s] = jnp.dot(h, w_ref[:, s * SLAB:(s + 1) * SLAB], preferred_element_type=F32).astype(o_ref.dtype)


def _proj(x, gain, w_main, *, tm=1024, per_step=3):
    T, D = x.shape
    slabs, h = pl.pallas_call(
        functools.partial(_proj_kernel, per_step=per_step),
        grid=(T // tm, N_SLABS // per_step),
        in_specs=[
            pl.BlockSpec((tm, D), lambda i, j: (i, 0)),
            pl.BlockSpec((1, D), lambda i, j: (0, 0)),
            pl.BlockSpec((D, per_step * SLAB), lambda i, j: (0, j)),
        ],
        out_specs=[pl.BlockSpec((per_step, tm, SLAB), lambda i, j: (j, i, 0)),
                   pl.BlockSpec((tm, D), lambda i, j: (i, 0))],
        out_shape=[jax.ShapeDtypeStruct((N_SLABS, T, SLAB), BF16), jax.ShapeDtypeStruct((T, D), BF16)],
        compiler_params=_params(("parallel", "arbitrary")),
        name="in_proj",
    )(x, gain.reshape(1, D), w_main)
    return slabs, h


def _slab_spec(slab, rows):
    return pl.BlockSpec((None, None, rows, SLAB), lambda b, t: (slab, b, t, 0))


def _split3_bf16(x):
    hi = x.astype(BF16)
    r = x - hi.astype(F32)
    mid = r.astype(BF16)
    lo = (r - mid.astype(F32)).astype(BF16)
    return hi, mid, lo


def _hgrn_kernel(q_ref, f_ref, i_ref, g_ref, lb_ref, gn_ref, o_ref, st_ref, b_ref, k_ref, v_ref, *, n_chunks):
    C, S = HG_CHUNK, HG_SUB

    @pl.when(pl.program_id(1) == 0)
    def _():
        st_ref[...] = jnp.zeros_like(st_ref)

    lb = lb_ref[...]
    gn = gn_ref[...]
    row = lax.broadcasted_iota(jnp.int32, (C, C), 0)
    col = lax.broadcasted_iota(jnp.int32, (C, C), 1)
    tri = (col <= row).astype(BF16)
    row8 = lax.broadcasted_iota(jnp.int32, (8, 1), 0)
    nt = (((1,), (1,)), ((), ()))
    tn = (((0,), (0,)), ((), ()))

    def chunk(c):
        r0 = c * C
        q = q_ref[pl.ds(r0, C), :].astype(F32)
        f = f_ref[pl.ds(r0, C), :].astype(F32)
        v = i_ref[pl.ds(r0, C), :].astype(F32)
        g = g_ref[pl.ds(r0, C), :].astype(F32)
        v16 = v.astype(BF16)

        qf = q * _sigmoid(q) * (HG_KDIM ** -0.5)
        kf = (1.0 - lb) * _sigmoid(-f)
        logf = jnp.log(lb + (1.0 - lb) * _sigmoid(f))
        b = sum(jnp.dot(tri, p, preferred_element_type=F32) for p in _split3_bf16(logf))
        b_last = b[C - 1:C, :]

        qd = (qf * jnp.exp(b)).astype(BF16)
        kd_last = (kf * jnp.exp(b_last - b)).astype(BF16)

        diag = []
        for h in range(HG_HEADS):
            hs = slice(h * HG_KDIM, (h + 1) * HG_KDIM)
            b_ref[c, h] = b[:, hs]
            k_ref[c, h] = kf[:, hs]
            v_ref[c, h] = v[:, hs]
            tiles = []
            for base in range(0, C, S):
                for lo in range(0, S, 8):
                    qq = qf[base + lo:base + lo + 8, hs]
                    bb = b[base + lo:base + lo + 8, hs]
                    acc = jnp.zeros((8, HG_VDIM), F32)
                    for j in range(lo + 8):
                        row = pl.ds(base + j, 8, stride=0)
                        p = qq * k_ref[c, h, row, :] * jnp.exp(jnp.minimum(bb - b_ref[c, h, row, :], 0.0))
                        a = jnp.sum(p, axis=-1, keepdims=True)
                        if j >= lo:
                            a = jnp.where(row8 >= j - lo, a, 0.0)
                        acc = acc + a * v_ref[c, h, row, :]
                    tiles.append(acc)
            diag.append(jnp.concatenate(tiles, axis=0))

        outs = []
        for h in range(HG_HEADS):
            hs = slice(h * HG_KDIM, (h + 1) * HG_KDIM)
            st = st_ref[h]
            o = lax.dot_general(qd[:, hs], st.astype(BF16), nt, preferred_element_type=F32) + diag[h]
            off = [jnp.zeros((S, HG_VDIM), F32)]
            for i in range(1, C // S):
                r = b[S * i - 1:S * i, hs]
                qi = (qf[S * i:S * (i + 1), hs] * jnp.exp(b[S * i:S * (i + 1), hs] - r)).astype(BF16)
                kj = (kf[:S * i, hs] * jnp.exp(r - b[:S * i, hs])).astype(BF16)
                att = lax.dot_general(qi, kj, nt, preferred_element_type=F32)
                off.append(jnp.dot(att.astype(BF16), v16[:S * i, hs], preferred_element_type=F32))
            o = o + jnp.concatenate(off, axis=0)
            st_ref[h] = jnp.exp(b_last[:, hs]) * st + lax.dot_general(
                v16[:, hs], kd_last[:, hs], tn, preferred_element_type=F32)
            o = o * lax.rsqrt(jnp.mean(o * o, axis=-1, keepdims=True) + NORM_EPS)
            outs.append(o)
        o = jnp.concatenate(outs, axis=-1) * gn * (g * _sigmoid(g))
        o_ref[pl.ds(r0, C), :] = o.astype(o_ref.dtype)

    for c in range(n_chunks):
        chunk(c)


def _hgrn(proj, lb, gnorm, *, tt=256):
    _, B, L, _ = proj.shape
    W = HG_HEADS * HG_KDIM
    return pl.pallas_call(
        functools.partial(_hgrn_kernel, n_chunks=tt // HG_CHUNK),
        grid=(B, L // tt),
        in_specs=[
            _slab_spec(SLAB_HQ, tt), _slab_spec(SLAB_HF, tt), _slab_spec(SLAB_HI, tt), _slab_spec(SLAB_HG, tt),
            pl.BlockSpec((1, W), lambda b, t: (0, 0)),
            pl.BlockSpec((1, W), lambda b, t: (0, 0)),
        ],
        out_specs=pl.BlockSpec((None, tt, W), lambda b, t: (b, t, 0)),
        out_shape=jax.ShapeDtypeStruct((B, L, W), BF16),
        scratch_shapes=[pltpu.VMEM((HG_HEADS, HG_VDIM, HG_KDIM), F32)] + [pltpu.VMEM((tt // HG_CHUNK, HG_HEADS, HG_CHUNK, HG_KDIM), F32)] * 3,
        compiler_params=_params(("parallel", "arbitrary")),
        name="hgrn2",
    )(proj, proj, proj, proj, lb.reshape(1, W), gnorm.reshape(1, W))


def _s5_kernel(u_ref, wb_ref, wc_ref, pw_ref, d_ref, wglu_ref, o_ref, x_ref, carry_ref, *, tt):
    NS = S5_HALF_STATES
    R = S5_ROWS
    LC = 512

    @pl.when(pl.program_id(1) == 0)
    def _():
        carry_ref[...] = jnp.zeros_like(carry_ref)

    u16 = u_ref[...]
    for hf in range(2):
        x_ref[hf] = jnp.dot(u16[:, hf * S5_HALF_CH:(hf + 1) * S5_HALF_CH], wb_ref[hf],
                            preferred_element_type=F32)

    rows = lax.broadcasted_iota(jnp.int32, (R, LC), 0)
    for hf in range(2):
        for lc in range(NS // LC):
            re = slice(lc * LC, (lc + 1) * LC)
            im = slice(NS + lc * LC, NS + (lc + 1) * LC)
            pr = pw_ref[hf, :, re]
            pi = pw_ref[hf, :, im]
            steps = []
            for s in (1, 2, 4):
                keep = rows >= s
                steps.append((s, jnp.where(keep, pr[s - 1:s, :], 0.0), jnp.where(keep, pi[s - 1:s, :], 0.0)))

            def block(k, carry, hf=hf, re=re, im=im, pr=pr, pi=pi, steps=steps):
                cr, ci = carry
                r0 = pl.multiple_of(k * R, R)
                xr = x_ref[hf, pl.ds(r0, R), re]
                xi = x_ref[hf, pl.ds(r0, R), im]
                for s, ar, ai in steps:
                    sr = pltpu.roll(xr, s, 0)
                    si = pltpu.roll(xi, s, 0)
                    xr, xi = xr + (ar * sr - ai * si), xi + (ar * si + ai * sr)
                xr, xi = xr + (pr * cr - pi * ci), xi + (pr * ci + pi * cr)
                x_ref[hf, pl.ds(r0, R), re] = xr
                x_ref[hf, pl.ds(r0, R), im] = xi
                return xr[R - 1:R, :], xi[R - 1:R, :]

            cr, ci = lax.fori_loop(0, tt // R, block, (carry_ref[hf:hf + 1, re], carry_ref[hf:hf + 1, im]))
            carry_ref[hf:hf + 1, re] = cr
            carry_ref[hf:hf + 1, im] = ci

    y = jnp.concatenate(
        [jnp.dot(x_ref[hf].astype(BF16), wc_ref[hf], preferred_element_type=F32) for hf in range(2)], axis=-1)
    y = y + d_ref[...] * u16.astype(F32)
    z = jax.nn.gelu(y).astype(BF16)
    zz = jnp.dot(z, wglu_ref[...], preferred_element_type=F32)
    o_ref[...] = (zz[:, :BRANCH_WIDTH] * _sigmoid(zz[:, BRANCH_WIDTH:])).astype(o_ref.dtype)


def _s5(proj, wb, wc, pw, d_skip, w_glu, *, tt=256):
    _, B, L, _ = proj.shape
    W = BRANCH_WIDTH
    NS2 = 2 * S5_HALF_STATES
    return pl.pallas_call(
        functools.partial(_s5_kernel, tt=tt),
        grid=(B, L // tt),
        in_specs=[
            _slab_spec(SLAB_S5, tt),
            pl.BlockSpec((2, S5_HALF_CH, NS2), lambda b, t: (0, 0, 0)),
            pl.BlockSpec((2, NS2, S5_HALF_CH), lambda b, t: (0, 0, 0)),
            pl.BlockSpec((2, S5_ROWS, NS2), lambda b, t: (0, 0, 0)),
            pl.BlockSpec((1, W), lambda b, t: (0, 0)),
            pl.BlockSpec((W, 2 * W), lambda b, t: (0, 0)),
        ],
        out_specs=pl.BlockSpec((None, tt, W), lambda b, t: (b, t, 0)),
        out_shape=jax.ShapeDtypeStruct((B, L, W), BF16),
        scratch_shapes=[pltpu.VMEM((2, tt, NS2), F32), pltpu.VMEM((2, NS2), F32)],
        compiler_params=_params(("parallel", "arbitrary")),
        name="s5",
    )(proj, wb, wc, pw, d_skip.reshape(1, W), w_glu)


def _s5_tables(a_re, a_im, log_dt, b_re, b_im, c_re, c_im):
    dt = jnp.exp(log_dt)[:, None]
    mag = jnp.exp(a_re * dt)
    ab_re = mag * jnp.cos(a_im * dt)
    ab_im = mag * jnp.sin(a_im * dt)
    den = a_re * a_re + a_im * a_im
    zr = ((ab_re - 1.0) * a_re + ab_im * a_im) / den
    zi = (ab_im * a_re - (ab_re - 1.0) * a_im) / den
    bb_re = zr[..., None] * b_re - zi[..., None] * b_im
    bb_im = zr[..., None] * b_im + zi[..., None] * b_re
    GH = S5_GROUPS // 2
    eye = jnp.eye(GH, dtype=F32)

    def bdiag_in(m):
        m = m.reshape(2, GH, S5_STATE, S5_GROUP)
        return jnp.einsum('hgpc,gk->hgckp', m, eye).reshape(2, GH * S5_GROUP, GH * S5_STATE)

    def bdiag_out(m):
        m = m.reshape(2, GH, S5_GROUP, S5_STATE)
        return jnp.einsum('hgcp,gk->hgpkc', m, eye).reshape(2, GH * S5_STATE, GH * S5_GROUP)

    wb = jnp.concatenate([bdiag_in(bb_re), bdiag_in(bb_im)], axis=-1).astype(BF16)
    wc = jnp.concatenate([bdiag_out(c_re), -bdiag_out(c_im)], axis=1).astype(BF16)
    n = jnp.arange(1, S5_ROWS + 1, dtype=F32)[:, None, None]
    pmag = jnp.exp(n * (a_re * dt)[None])
    pang = n * (a_im * dt)[None]
    pr = (pmag * jnp.cos(pang)).reshape(S5_ROWS, 2, S5_HALF_STATES)
    pi = (pmag * jnp.sin(pang)).reshape(S5_ROWS, 2, S5_HALF_STATES)
    pw = jnp.concatenate([pr, pi], axis=-1).transpose(1, 0, 2)
    return wb, wc, pw


def _conv_kernel(a_ref, b_ref, w_ref, cb_ref, lg_ref, lbias_ref, o_ref, z_ref, *, tt, rb):
    H = CONV_HALO

    @pl.when(pl.program_id(1) == 0)
    def _():
        z_ref[0:H, :] = jnp.zeros((H, BRANCH_WIDTH), F32)

    z_ref[H:H + tt, :] = a_ref[...].astype(F32) * _sigmoid(b_ref[...].astype(F32))
    cb = cb_ref[...]
    lg = lg_ref[...]
    lbias = lbias_ref[...]

    def block(k, carry):
        r0 = pl.multiple_of(k * rb, rb)
        parts = []
        for lc in range(0, BRANCH_WIDTH, 128):
            win = z_ref[pl.ds(r0, rb + H), lc:lc + 128]
            acc = jnp.zeros((rb, 128), F32)
            for c in range(8):
                shifted = win if c == 0 else pltpu.roll(win, rb + H - c, 0)
                for off in range(c, H + 1, 8):
                    w = off - (H - CONV_WIDTH + 1)
                    if 0 <= w < CONV_WIDTH:
                        acc = acc + shifted[off - c:off - c + rb, :] * w_ref[w:w + 1, lc:lc + 128]
            parts.append(acc)
        acc = jnp.concatenate(parts, axis=-1) + cb
        xc = acc - jnp.mean(acc, axis=-1, keepdims=True)
        var = jnp.mean(xc * xc, axis=-1, keepdims=True)
        y = xc * lax.rsqrt(var + NORM_EPS) * lg + lbias
        o_ref[pl.ds(r0, rb), :] = (y * _sigmoid(y)).astype(o_ref.dtype)
        return carry

    lax.fori_loop(0, tt // rb, block, 0)
    z_ref[0:H, :] = z_ref[tt:tt + H, :]


def _conv(proj, conv_w, conv_b, ln_g, ln_b, *, tt=512, rb=32):
    _, B, L, _ = proj.shape
    W = BRANCH_WIDTH
    w_pad = jnp.zeros((CONV_HALO, W), F32).at[:CONV_WIDTH].set(conv_w)
    vec = lambda: pl.BlockSpec((1, W), lambda b, t: (0, 0))
    return pl.pallas_call(
        functools.partial(_conv_kernel, tt=tt, rb=rb),
        grid=(B, L // tt),
        in_specs=[_slab_spec(SLAB_CONV_A, tt), _slab_spec(SLAB_CONV_B, tt),
                  pl.BlockSpec((CONV_HALO, W), lambda b, t: (0, 0)), vec(), vec(), vec()],
        out_specs=pl.BlockSpec((None, tt, W), lambda b, t: (b, t, 0)),
        out_shape=jax.ShapeDtypeStruct((B, L, W), BF16),
        scratch_shapes=[pltpu.VMEM((tt + CONV_HALO, W), F32)],
        compiler_params=_params(("parallel", "arbitrary")),
        name="conformer_conv",
    )(proj, proj, w_pad, conv_b.reshape(1, W), ln_g.reshape(1, W), ln_b.reshape(1, W))


def _rope_table_kernel(pos_ref, inv_ref, cos_ref, sin_ref):
    ang = pos_ref[...] * inv_ref[...]
    lane = lax.broadcasted_iota(jnp.int32, ang.shape, 1)
    cos_ref[...] = jnp.cos(ang)
    sin_ref[...] = jnp.where(lane < ROPE_HALF, -jnp.sin(ang), jnp.sin(ang))


def _rope_tables(positions, *, tm=1024):
    T = positions.size
    inv = ROPE_THETA ** (-jnp.arange(ROPE_HALF, dtype=F32) / ROPE_HALF)
    inv = jnp.concatenate([inv, inv, jnp.zeros((ATT_HEAD_DIM - ROPE_DIM,), F32)]).reshape(1, ATT_HEAD_DIM)
    pos = positions.astype(F32).reshape(T, 1)
    return pl.pallas_call(
        _rope_table_kernel,
        grid=(T // tm,),
        in_specs=[pl.BlockSpec((tm, 1), lambda i: (i, 0)), pl.BlockSpec((1, ATT_HEAD_DIM), lambda i: (0, 0))],
        out_specs=[pl.BlockSpec((tm, ATT_HEAD_DIM), lambda i: (i, 0))] * 2,
        out_shape=[jax.ShapeDtypeStruct((T, ATT_HEAD_DIM), F32)] * 2,
        compiler_params=_params(("parallel",)),
        name="rope_tables",
    )(pos, inv)


def _rope(x, cos, sin):
    lane = lax.broadcasted_iota(jnp.int32, x.shape, 1)
    partner = jnp.where(lane < ROPE_HALF, pltpu.roll(x, ATT_HEAD_DIM - ROPE_HALF, 1), pltpu.roll(x, ROPE_HALF, 1))
    return x * cos + partner * sin


DEINT_STRIDE = 4


def _qkv_kernel(h_ref, w_ref, cos_ref, sin_ref, o_ref, s_ref, t_ref, *, d):
    tm = h_ref.shape[0]
    h16 = h_ref[...]
    cos, sin = cos_ref[...], sin_ref[...]
    for j in range(3):
        res = jnp.dot(h16, w_ref[:, j * SLAB:(j + 1) * SLAB], preferred_element_type=F32)
        for h in range(ATT_HEADS_PER_GROUP):
            hs = slice(h * ATT_HEAD_DIM, (h + 1) * ATT_HEAD_DIM)
            s_ref[j, h] = _rope(res[:, hs], cos, sin) if j < 2 else res[:, hs]
            if d <= DEINT_STRIDE:
                for r in range(d):
                    o_ref[j, r, :, hs] = s_ref[j, h, pl.ds(r, tm // d, stride=d), :].astype(o_ref.dtype)
            else:
                q = tm // DEINT_STRIDE
                for r0 in range(DEINT_STRIDE):
                    t_ref[j, h, pl.ds(r0 * q, q), :] = s_ref[j, h, pl.ds(r0, q, stride=DEINT_STRIDE), :]
                for r0 in range(DEINT_STRIDE):
                    for r1 in range(d // DEINT_STRIDE):
                        o_ref[j, DEINT_STRIDE * r1 + r0, :, hs] = t_ref[
                            j, h, pl.ds(r0 * q + r1, tm // d, stride=d // DEINT_STRIDE), :].astype(o_ref.dtype)


def _qkv(h, w_qkv, cos_t, sin_t, gi, d, B, *, tm=1024):
    T, D = h.shape
    L = T // B
    tiles = L // tm
    W = ATT_HEADS_PER_GROUP * ATT_HEAD_DIM
    assert d <= DEINT_STRIDE or d == DEINT_STRIDE * DEINT_STRIDE
    stage = (3, ATT_HEADS_PER_GROUP, tm, ATT_HEAD_DIM)
    return pl.pallas_call(
        functools.partial(_qkv_kernel, d=d),
        grid=(T // tm,),
        in_specs=[
            pl.BlockSpec((tm, D), lambda i: (i, 0)),
            pl.BlockSpec((D, 3 * SLAB), lambda i: (0, 0)),
            pl.BlockSpec((tm, ATT_HEAD_DIM), lambda i: (i, 0)),
            pl.BlockSpec((tm, ATT_HEAD_DIM), lambda i: (i, 0)),
        ],
        out_specs=pl.BlockSpec((3, None, d, tm // d, W), lambda i: (0, i // tiles, 0, i % tiles, 0)),
        out_shape=jax.ShapeDtypeStruct((3, B, d, L // d, W), BF16),
        scratch_shapes=[pltpu.VMEM(stage, F32),
                        pltpu.VMEM(stage if d > DEINT_STRIDE else (1, 1, 8, ATT_HEAD_DIM), F32)],
        compiler_params=_params(("parallel",)),
        name=f"qkv_proj_g{gi}",
    )(h, w_qkv, cos_t, sin_t)


ATT_UNROLL = 4


def _att_kernel(q_ref, kc_ref, kp_ref, vc_ref, vp_ref, o_ref, lse_ref, *, d, nq):
    Q = ATT_QBLOCK
    n = pl.program_id(1)
    qi = lax.broadcasted_iota(jnp.int32, (Q, 2 * Q), 0)
    kj = lax.broadcasted_iota(jnp.int32, (Q, 2 * Q), 1)
    rel = Q + qi - kj
    band = (rel >= 0) & (rel <= Q)
    band_first = band & ((n > 0) | (kj >= Q))
    head_of_lane = lax.broadcasted_iota(jnp.int32, (Q, ATT_HEAD_DIM), 1) // (ATT_HEAD_DIM // ATT_HEADS_PER_GROUP)
    nt = (((1,), (1,)), ((), ()))

    def block(r, i):
        rows = slice(i * Q, (i + 1) * Q)
        outs = []
        lse = jnp.zeros((Q, ATT_HEAD_DIM), F32)
        for h in range(ATT_HEADS_PER_GROUP):
            hs = slice(h * ATT_HEAD_DIM, (h + 1) * ATT_HEAD_DIM)
            if i == 0:
                k = jnp.concatenate([kp_ref[r, :, hs], kc_ref[r, rows, hs]], axis=0)
                v = jnp.concatenate([vp_ref[r, :, hs], vc_ref[r, rows, hs]], axis=0)
            else:
                k = kc_ref[r, (i - 1) * Q:(i + 1) * Q, hs]
                v = vc_ref[r, (i - 1) * Q:(i + 1) * Q, hs]
            s = lax.dot_general(q_ref[r, rows, hs], k, nt, preferred_element_type=F32) * (ATT_HEAD_DIM ** -0.5)
            s = jnp.where(band_first if i == 0 else band, s, MASK_VALUE)
            m = jnp.max(s, axis=-1, keepdims=True)
            p = jnp.exp(s - m)
            l = jnp.sum(p, axis=-1, keepdims=True)
            outs.append(jnp.dot(p.astype(BF16), v, preferred_element_type=F32) / l)
            lse = jnp.where(head_of_lane == h, m + jnp.log(l), lse)
        o_ref[r, rows, :] = jnp.concatenate(outs, axis=-1).astype(o_ref.dtype)
        lse_ref[r, rows, :] = lse

    if d <= ATT_UNROLL:
        for r in range(d):
            for i in range(nq):
                block(r, i)
    else:
        def residues(it, carry):
            for u in range(ATT_UNROLL):
                for i in range(nq):
                    block(it * ATT_UNROLL + u, i)
            return carry

        lax.fori_loop(0, d // ATT_UNROLL, residues, 0)


def _attention_group(qkv, gi):
    _, B, d, n_sub, W = qkv.shape
    Q = ATT_QBLOCK
    nq = max(1, ATT_UNROLL // d)

    def cur(s):
        return pl.BlockSpec((None, None, d, nq * Q, W), lambda b, n: (s, b, 0, n, 0))

    def prev(s):
        return pl.BlockSpec((None, None, d, Q, W), lambda b, n: (s, b, 0, jnp.maximum(nq * n - 1, 0), 0))

    return pl.pallas_call(
        functools.partial(_att_kernel, d=d, nq=nq),
        grid=(B, n_sub // (nq * Q)),
        in_specs=[cur(0), cur(1), prev(1), cur(2), prev(2)],
        out_specs=[pl.BlockSpec((None, d, nq * Q, W), lambda b, n: (b, 0, n, 0)),
                   pl.BlockSpec((None, d, nq * Q, ATT_HEAD_DIM), lambda b, n: (b, 0, n, 0))],
        out_shape=[jax.ShapeDtypeStruct((B, d, n_sub, W), BF16),
                   jax.ShapeDtypeStruct((B, d, n_sub, ATT_HEAD_DIM), F32)],
        compiler_params=_params(("parallel", "arbitrary")),
        name=f"dilated_attention_g{gi}",
    )(qkv, qkv, qkv, qkv, qkv)


def _merge_kernel(x_ref, ya_ref, yb_ref, yc_ref, o1_ref, o2_ref, o3_ref, l1_ref, l2_ref, l3_ref,
                  g0_ref, g1_ref, g2_ref, g3_ref, g4_ref, g5_ref, g6_ref, g7_ref, wb_ref, wo_ref, out_ref,
                  os_ref, ls_ref):
    for g, (o_ref, l_ref) in enumerate(((o1_ref, l1_ref), (o2_ref, l2_ref), (o3_ref, l3_ref))):
        d, rows = o_ref.shape[0], o_ref.shape[1]
        for r in range(d):
            o = o_ref[r].astype(F32)
            for h in range(ATT_HEADS_PER_GROUP):
                os_ref[g, h, pl.ds(r, rows, stride=d), :] = o[:, h * ATT_HEAD_DIM:(h + 1) * ATT_HEAD_DIM]
            ls_ref[g, pl.ds(r, rows, stride=d), :] = l_ref[r]
    lanes = ATT_HEAD_DIM // ATT_HEADS_PER_GROUP
    parts = []
    for h in range(ATT_HEADS_PER_GROUP):
        l1, l2, l3 = (ls_ref[g, :, h * lanes:h * lanes + 1] for g in range(3))
        m = jnp.maximum(jnp.maximum(l1, l2), l3)
        e1, e2, e3 = jnp.exp(l1 - m), jnp.exp(l2 - m), jnp.exp(l3 - m)
        parts.append((e1 * os_ref[0, h] + e2 * os_ref[1, h] + e3 * os_ref[2, h]) / (e1 + e2 + e3))
    yd = jnp.concatenate(parts, axis=-1)
    ys = (ya_ref[...], yb_ref[...], yc_ref[...], yd.astype(BF16))
    gates = ((g0_ref, g1_ref), (g2_ref, g3_ref), (g4_ref, g5_ref), (g6_ref, g7_ref))
    merged = None
    for k in range(N_BRANCHES):
        gate = _sigmoid(jnp.concatenate([gates[k][0][...], gates[k][1][...]], axis=-1).astype(F32))
        term = gate * jnp.dot(ys[k], wb_ref[k], preferred_element_type=F32)
        merged = term if merged is None else merged + term
    out_ref[...] = x_ref[...] + jnp.dot(merged.astype(BF16), wo_ref[...], preferred_element_type=F32)


def _merge(x, ya, yb, yc, att, proj, w_branch, w_out, *, tm=256):
    B, L, D = x.shape
    W = BRANCH_WIDTH
    (o1, l1), (o2, l2), (o3, l3) = att
    row = lambda: pl.BlockSpec((None, tm, W), lambda b, t: (b, t, 0))

    def res(a):
        d, last = a.shape[1], a.shape[3]
        return pl.BlockSpec((None, d, tm // d, last), lambda b, t: (b, 0, t, 0))

    return pl.pallas_call(
        _merge_kernel,
        grid=(B, L // tm),
        in_specs=[pl.BlockSpec((None, tm, D), lambda b, t: (b, t, 0))] + [row() for _ in range(3)]
                 + [res(a) for a in (o1, o2, o3, l1, l2, l3)]
                 + [_slab_spec(SLAB_GATE + s, tm) for s in range(8)]
                 + [pl.BlockSpec((N_BRANCHES, W, D), lambda b, t: (0, 0, 0)), pl.BlockSpec((D, D), lambda b, t: (0, 0))],
        out_specs=pl.BlockSpec((None, tm, D), lambda b, t: (b, t, 0)),
        out_shape=jax.ShapeDtypeStruct((B, L, D), F32),
        scratch_shapes=[pltpu.VMEM((3, ATT_HEADS_PER_GROUP, tm, ATT_HEAD_DIM), F32),
                        pltpu.VMEM((3, tm, ATT_HEAD_DIM), F32)],
        compiler_params=_params(("parallel", "parallel")),
        name="gated_merge",
    )(x, ya, yb, yc, o1, o2, o3, l1, l2, l3, *([proj] * 8), w_branch, w_out)


def kernel(x, positions, ffn1_norm, ffn1_w_gate, ffn1_w_up, ffn1_w_down, mix_norm, w_in, hg_lb_logits, hg_gnorm, s5_a_re, s5_a_im, s5_log_dt, s5_b_re, s5_b_im, s5_c_re, s5_c_im, s5_d, s5_w_glu, conv_w, conv_b, conv_ln_g, conv_ln_b, w_branch, w_out, ffn2_norm, ffn2_w_gate, ffn2_w_up, ffn2_w_down, final_norm):
    B, L, D = x.shape
    T = B * L
    depth = w_in.shape[0]
    lb_soft = jax.nn.softmax(hg_lb_logits.astype(F32), axis=0)
    lb_all = jnp.cumsum(lb_soft, axis=0) - lb_soft[0]
    cos_t, sin_t = _rope_tables(positions)
    bf = lambda w: w.astype(BF16)

    xt = x.reshape(T, D)
    for l in range(depth):
        xt = _ffn(xt, ffn1_norm[l], bf(ffn1_w_gate[l]), bf(ffn1_w_up[l]), bf(ffn1_w_down[l]))
        w16 = bf(w_in[l])
        col = lambda s, n=1: w16[:, s * SLAB:(s + n) * SLAB]
        w_main = jnp.concatenate([col(0, SLAB_GATE), col(W_SLAB_GATE, N_SLABS - SLAB_GATE)], axis=1)
        proj, h_mix = _proj(xt, mix_norm[l], w_main)
        proj = proj.reshape(N_SLABS, B, L, SLAB)
        ya = _hgrn(proj, lb_all[l], hg_gnorm[l])
        wb, wc, pw = _s5_tables(s5_a_re[l], s5_a_im[l], s5_log_dt[l], s5_b_re[l], s5_b_im[l], s5_c_re[l], s5_c_im[l])
        yb = _s5(proj, wb, wc, pw, s5_d[l], bf(s5_w_glu[l]))
        yc = _conv(proj, conv_w[l], conv_b[l], conv_ln_g[l], conv_ln_b[l])
        att = []
        for gi, (_, dil) in enumerate(ATT_CONFIGS):
            w_qkv = jnp.concatenate([col(W_SLAB_Q + gi + 3 * j) for j in range(3)], axis=1)
            att.append(_attention_group(_qkv(h_mix, w_qkv, cos_t, sin_t, gi, dil, B), gi))
        xt = _merge(xt.reshape(B, L, D), ya, yb, yc, att, proj, bf(w_branch[l]), bf(w_out[l])).reshape(T, D)
        last = l == depth - 1
        xt = _ffn(xt, ffn2_norm[l], bf(ffn2_w_gate[l]), bf(ffn2_w_up[l]), bf(ffn2_w_down[l]),
                  final_norm if last else None)
    return xt.reshape(B, L, D)
```

```python
import functools
import math

import numpy as np
import jax
import jax.numpy as jnp
from jax import lax
from jax.experimental import pallas as pl
from jax.experimental.pallas import tpu as pltpu

F32 = jnp.float32
BF16 = jnp.bfloat16

NORM_EPS = 1e-6
MASK_VALUE = -1e30
D_MODEL = 1024
D_FF = 2816
N_BRANCHES = 4
BRANCH_WIDTH = 512

HG_HEADS = 4
HG_KDIM = 128
HG_VDIM = 128
HG_CHUNK = 64
HG_SUB = 16

S5_GROUP = 16
S5_GROUPS = 32
S5_STATE = 64
S5_HALF_CH = 256
S5_HALF_STATES = (S5_GROUPS // 2) * S5_STATE
S5_TILE = 256
S5_SEGS = 8
S5_PITCH = S5_TILE // S5_SEGS + 8
S5_SCAN_LANES = 512

CONV_WIDTH = 31
CONV_HALO = 32

ATT_HEAD_DIM = 128
ATT_CONFIGS = ((128, 1), (512, 4), (2048, 16))
ATT_HEADS_PER_GROUP = 4
ATT_HEADS = 12
ATT_QBLOCK = 128
ROPE_THETA = 500000.0
ROPE_DIM = 32
ROPE_HALF = 16

SLAB = 512
SLAB_HQ, SLAB_HF, SLAB_HI, SLAB_HG, SLAB_S5, SLAB_CONV_A, SLAB_CONV_B = 0, 1, 2, 3, 4, 5, 6
W_SLAB_Q = 7
W_SLAB_GATE = 16
SLABS_PER_STEP = 4
SLAB_PAD = 7
SLAB_GATE = 8
N_SLABS = 16

VMEM_LIMIT = 56 * 1024 * 1024


def _params(sem):
    return pltpu.CompilerParams(dimension_semantics=sem, vmem_limit_bytes=VMEM_LIMIT)


def _rms(x):
    return x * lax.rsqrt(jnp.mean(x * x, axis=-1, keepdims=True) + NORM_EPS)


def _sigmoid(x):
    return jax.nn.sigmoid(x)


def _ffn_kernel(x_ref, gain_ref, wg_ref, wu_ref, wd_ref, *rest, n_ff, final):
    if final:
        fgain_ref, o_ref, h_ref, acc_ref = rest
    else:
        o_ref, h_ref, acc_ref = rest
    j = pl.program_id(1)

    @pl.when(j == 0)
    def _():
        h_ref[...] = (_rms(x_ref[...]) * gain_ref[...]).astype(BF16)
        acc_ref[...] = jnp.zeros_like(acc_ref)

    h = h_ref[...]
    g = jnp.dot(h, wg_ref[...], preferred_element_type=F32)
    u = jnp.dot(h, wu_ref[...], preferred_element_type=F32)
    a = (g * _sigmoid(g) * u).astype(BF16)
    acc_ref[...] += jnp.dot(a, wd_ref[...], preferred_element_type=F32)

    @pl.when(j == n_ff - 1)
    def _():
        y = x_ref[...] + 0.5 * acc_ref[...]
        if final:
            y = _rms(y) * fgain_ref[...]
        o_ref[...] = y


def _ffn(x, gain, wg, wu, wd, final_gain=None, *, tm=512, tf=1408):
    T, D = x.shape
    FF = wg.shape[1]
    n_ff = FF // tf
    final = final_gain is not None
    in_specs = [
        pl.BlockSpec((tm, D), lambda i, j: (i, 0)),
        pl.BlockSpec((1, D), lambda i, j: (0, 0)),
        pl.BlockSpec((D, tf), lambda i, j: (0, j)),
        pl.BlockSpec((D, tf), lambda i, j: (0, j)),
        pl.BlockSpec((tf, D), lambda i, j: (j, 0)),
    ]
    args = [x, gain.reshape(1, D), wg, wu, wd]
    if final:
        in_specs.append(pl.BlockSpec((1, D), lambda i, j: (0, 0)))
        args.append(final_gain.reshape(1, D))
    return pl.pallas_call(
        functools.partial(_ffn_kernel, n_ff=n_ff, final=final),
        grid=(T // tm, n_ff),
        in_specs=in_specs,
        out_specs=pl.BlockSpec((tm, D), lambda i, j: (i, 0)),
        out_shape=jax.ShapeDtypeStruct((T, D), F32),
        scratch_shapes=[pltpu.VMEM((tm, D), BF16), pltpu.VMEM((tm, D), F32)],
        compiler_params=_params(("parallel", "arbitrary")),
        name="ffn_final" if final else "ffn",
    )(*args)


def _proj_kernel(x_ref, gain_ref, w_ref, o_ref, h_ref):
    j = pl.program_id(1)

    @pl.when(j == 0)
    def _():
        h_ref[...] = (_rms(x_ref[...]) * gain_ref[...]).astype(BF16)

    def slab(s):
        w = w_ref[:, s * SLAB:(s + 1) * SLAB].astype(BF16)
        o_ref[s] = jnp.dot(h_ref[...], w, preferred_element_type=F32).astype(o_ref.dtype)

    for s in range(SLABS_PER_STEP):
        if s == SLAB_PAD % SLABS_PER_STEP:
            is_pad = j == SLAB_PAD // SLABS_PER_STEP
            pl.when(jnp.logical_not(is_pad))(functools.partial(slab, s))

            @pl.when(is_pad)
            def _():
                o_ref[s] = jnp.zeros(o_ref.shape[1:], o_ref.dtype)
        else:
            slab(s)


def _proj(x, gain, w_in, layer, *, tm=1024):
    T, D = x.shape
    first_gate_step = SLAB_GATE // SLABS_PER_STEP
    gate_shift = (W_SLAB_GATE - SLAB_GATE) // SLABS_PER_STEP
    slabs, h = pl.pallas_call(
        _proj_kernel,
        grid=(T // tm, N_SLABS // SLABS_PER_STEP),
        in_specs=[
            pl.BlockSpec((tm, D), lambda i, j: (i, 0)),
            pl.BlockSpec((1, D), lambda i, j: (0, 0)),
            pl.BlockSpec((None, D, SLABS_PER_STEP * SLAB),
                         lambda i, j: (layer, 0, jnp.where(j < first_gate_step, j, j + gate_shift))),
        ],
        out_specs=[pl.BlockSpec((SLABS_PER_STEP, tm, SLAB), lambda i, j: (j, i, 0)),
                   pl.BlockSpec((tm, D), lambda i, j: (i, 0))],
        out_shape=[jax.ShapeDtypeStruct((N_SLABS, T, SLAB), BF16), jax.ShapeDtypeStruct((T, D), BF16)],
        compiler_params=_params(("parallel", "arbitrary")),
        name="in_proj",
    )(x, gain.reshape(1, D), w_in)
    return slabs, h


def _slab_spec(slab, rows):
    return pl.BlockSpec((None, None, rows, SLAB), lambda b, t: (slab, b, t, 0))


def _split3_bf16(x):
    hi = x.astype(BF16)
    r = x - hi.astype(F32)
    mid = r.astype(BF16)
    lo = (r - mid.astype(F32)).astype(BF16)
    return hi, mid, lo


def _hgrn_kernel(q_ref, f_ref, i_ref, g_ref, lb_ref, gn_ref, o_ref, st_ref, b_ref, k_ref, v_ref, *, n_chunks):
    C, S = HG_CHUNK, HG_SUB

    @pl.when(pl.program_id(1) == 0)
    def _():
        st_ref[...] = jnp.zeros_like(st_ref)

    lb = lb_ref[...]
    gn = gn_ref[...]
    row = lax.broadcasted_iota(jnp.int32, (C, C), 0)
    col = lax.broadcasted_iota(jnp.int32, (C, C), 1)
    tri = (col <= row).astype(BF16)
    row8 = lax.broadcasted_iota(jnp.int32, (8, 1), 0)
    nt = (((1,), (1,)), ((), ()))
    tn = (((0,), (0,)), ((), ()))

    def chunk(c):
        r0 = c * C
        q = q_ref[pl.ds(r0, C), :].astype(F32)
        f = f_ref[pl.ds(r0, C), :].astype(F32)
        v = i_ref[pl.ds(r0, C), :].astype(F32)
        g = g_ref[pl.ds(r0, C), :].astype(F32)
        v16 = v.astype(BF16)

        qf = q * _sigmoid(q) * (HG_KDIM ** -0.5)
        kf = (1.0 - lb) * _sigmoid(-f)
        logf = jnp.log(lb + (1.0 - lb) * _sigmoid(f))
        b = sum(jnp.dot(tri, p, preferred_element_type=F32) for p in _split3_bf16(logf))
        b_last = b[C - 1:C, :]

        qd = (qf * jnp.exp(b)).astype(BF16)
        kd_last = (kf * jnp.exp(b_last - b)).astype(BF16)

        diag = []
        for h in range(HG_HEADS):
            hs = slice(h * HG_KDIM, (h + 1) * HG_KDIM)
            b_ref[c, h] = b[:, hs]
            k_ref[c, h] = kf[:, hs]
            v_ref[c, h] = v[:, hs]
            tiles = []
            for base in range(0, C, S):
                for lo in range(0, S, 8):
                    qq = qf[base + lo:base + lo + 8, hs]
                    bb = b[base + lo:base + lo + 8, hs]
                    acc = jnp.zeros((8, HG_VDIM), F32)
                    for j in range(lo + 8):
                        row = pl.ds(base + j, 8, stride=0)
                        p = qq * k_ref[c, h, row, :] * jnp.exp(jnp.minimum(bb - b_ref[c, h, row, :], 0.0))
                        a = jnp.sum(p, axis=-1, keepdims=True)
                        if j >= lo:
                            a = jnp.where(row8 >= j - lo, a, 0.0)
                        acc = acc + a * v_ref[c, h, row, :]
                    tiles.append(acc)
            diag.append(jnp.concatenate(tiles, axis=0))

        outs = []
        for h in range(HG_HEADS):
            hs = slice(h * HG_KDIM, (h + 1) * HG_KDIM)
            st = st_ref[h]
            o = lax.dot_general(qd[:, hs], st.astype(BF16), nt, preferred_element_type=F32) + diag[h]
            off = [jnp.zeros((S, HG_VDIM), F32)]
            for i in range(1, C // S):
                r = b[S * i - 1:S * i, hs]
                qi = (qf[S * i:S * (i + 1), hs] * jnp.exp(b[S * i:S * (i + 1), hs] - r)).astype(BF16)
                kj = (kf[:S * i, hs] * jnp.exp(r - b[:S * i, hs])).astype(BF16)
                att = lax.dot_general(qi, kj, nt, preferred_element_type=F32)
                off.append(jnp.dot(att.astype(BF16), v16[:S * i, hs], preferred_element_type=F32))
            o = o + jnp.concatenate(off, axis=0)
            st_ref[h] = jnp.exp(b_last[:, hs]) * st + lax.dot_general(
                v16[:, hs], kd_last[:, hs], tn, preferred_element_type=F32)
            o = o * lax.rsqrt(jnp.mean(o * o, axis=-1, keepdims=True) + NORM_EPS)
            outs.append(o)
        o = jnp.concatenate(outs, axis=-1) * gn * (g * _sigmoid(g))
        o_ref[pl.ds(r0, C), :] = o.astype(o_ref.dtype)

    for c in range(n_chunks):
        chunk(c)


def _hgrn(proj, lb, gnorm, *, tt=256):
    _, B, L, _ = proj.shape
    W = HG_HEADS * HG_KDIM
    return pl.pallas_call(
        functools.partial(_hgrn_kernel, n_chunks=tt // HG_CHUNK),
        grid=(B, L // tt),
        in_specs=[
            _slab_spec(SLAB_HQ, tt), _slab_spec(SLAB_HF, tt), _slab_spec(SLAB_HI, tt), _slab_spec(SLAB_HG, tt),
            pl.BlockSpec((1, W), lambda b, t: (0, 0)),
            pl.BlockSpec((1, W), lambda b, t: (0, 0)),
        ],
        out_specs=pl.BlockSpec((None, tt, W), lambda b, t: (b, t, 0)),
        out_shape=jax.ShapeDtypeStruct((B, L, W), BF16),
        scratch_shapes=[pltpu.VMEM((HG_HEADS, HG_VDIM, HG_KDIM), F32)] + [pltpu.VMEM((tt // HG_CHUNK, HG_HEADS, HG_CHUNK, HG_KDIM), F32)] * 3,
        compiler_params=_params(("parallel", "arbitrary")),
        name="hgrn2",
    )(proj, proj, proj, proj, lb.reshape(1, W), gnorm.reshape(1, W))


def _cmul(ar, ai, br, bi):
    return ar * br - ai * bi, ar * bi + ai * br


def _s5_kernel(u_ref, wb_ref, wc_ref, pj_ref, pq_ref, d_ref, wglu_ref, o_ref, x_ref, rq_ref, carry_ref, *, tt):
    SEGS, SR, PITCH, NS, LC = S5_SEGS, tt // S5_SEGS, S5_PITCH, S5_HALF_STATES, S5_SCAN_LANES
    row8 = lax.broadcasted_iota(jnp.int32, (SEGS, LC), 0)
    lane_tiles = range(BRANCH_WIDTH // 128)

    @pl.when(pl.program_id(1) == 0)
    def _():
        carry_ref[...] = jnp.zeros_like(carry_ref)

    u = u_ref[...].astype(F32)
    for lt in lane_tiles:
        for s in range(SEGS):
            rq_ref[lt, pl.ds(PITCH * s, SR), :] = u[SR * s:SR * (s + 1), lt * 128:(lt + 1) * 128]
    u = jnp.concatenate(
        [jnp.concatenate([rq_ref[lt, pl.ds(j, SEGS, stride=PITCH), :] for lt in lane_tiles], axis=-1)
         for j in range(SR)], axis=0)
    u16 = u.astype(BF16)

    for hf in range(2):
        x_ref[hf] = jnp.dot(u16[:, hf * S5_HALF_CH:(hf + 1) * S5_HALF_CH], wb_ref[hf],
                            preferred_element_type=F32)

    for hf in range(2):
        for lc in range(NS // LC):
            re = slice(lc * LC, (lc + 1) * LC)
            im = slice(NS + lc * LC, NS + (lc + 1) * LC)
            a1r, a1i = pj_ref[hf, 0:SEGS, re], pj_ref[hf, 0:SEGS, im]

            def recur(j, state, hf=hf, re=re, im=im, a1r=a1r, a1i=a1i):
                r0 = pl.multiple_of(j * SEGS, SEGS)
                xr, xi = _cmul(a1r, a1i, state[0], state[1])
                xr = xr + x_ref[hf, pl.ds(r0, SEGS), re]
                xi = xi + x_ref[hf, pl.ds(r0, SEGS), im]
                x_ref[hf, pl.ds(r0, SEGS), re] = xr
                x_ref[hf, pl.ds(r0, SEGS), im] = xi
                return xr, xi

            zero = jnp.zeros((SEGS, LC), F32)
            er, ei = lax.fori_loop(0, SR, recur, (zero, zero))

            for shift in (1, 2, 4):
                cr = jnp.where(row8 >= shift, pq_ref[hf, shift:shift + 1, re], 0.0)
                ci = jnp.where(row8 >= shift, pq_ref[hf, shift:shift + 1, im], 0.0)
                dr, di = _cmul(cr, ci, pltpu.roll(er, shift, 0), pltpu.roll(ei, shift, 0))
                er, ei = er + dr, ei + di
            c0r = carry_ref[hf, SEGS - 1:SEGS, re]
            c0i = carry_ref[hf, SEGS - 1:SEGS, im]
            sr, si = _cmul(pq_ref[hf, 0:SEGS, re], pq_ref[hf, 0:SEGS, im], c0r, c0i)
            sr = sr + jnp.where(row8 >= 1, pltpu.roll(er, 1, 0), 0.0)
            si = si + jnp.where(row8 >= 1, pltpu.roll(ei, 1, 0), 0.0)
            nr, ni = _cmul(pq_ref[hf, SEGS:2 * SEGS, re], pq_ref[hf, SEGS:2 * SEGS, im], c0r, c0i)
            carry_ref[hf, :, re] = er + nr
            carry_ref[hf, :, im] = ei + ni

            def fix(j, carry, hf=hf, re=re, im=im, sr=sr, si=si):
                r0 = pl.multiple_of(j * SEGS, SEGS)
                dr, di = _cmul(pj_ref[hf, pl.ds(r0, SEGS), re], pj_ref[hf, pl.ds(r0, SEGS), im], sr, si)
                x_ref[hf, pl.ds(r0, SEGS), re] = x_ref[hf, pl.ds(r0, SEGS), re] + dr
                x_ref[hf, pl.ds(r0, SEGS), im] = x_ref[hf, pl.ds(r0, SEGS), im] + di
                return carry

            lax.fori_loop(0, SR, fix, 0)

    y = jnp.concatenate(
        [jnp.dot(x_ref[hf].astype(BF16), wc_ref[hf], preferred_element_type=F32) for hf in range(2)], axis=-1)
    y = y + d_ref[...] * u
    z = jax.nn.gelu(y).astype(BF16)
    zz = jnp.dot(z, wglu_ref[...], preferred_element_type=F32)
    out = zz[:, :BRANCH_WIDTH] * _sigmoid(zz[:, BRANCH_WIDTH:])
    for lt in lane_tiles:
        for j in range(SR):
            rq_ref[lt, pl.ds(j, SEGS, stride=PITCH), :] = out[SEGS * j:SEGS * (j + 1), lt * 128:(lt + 1) * 128]
    for lt in lane_tiles:
        for s in range(SEGS):
            o_ref[SR * s:SR * (s + 1), lt * 128:(lt + 1) * 128] = rq_ref[lt, pl.ds(PITCH * s, SR), :].astype(o_ref.dtype)


def _s5(proj, wb, wc, pj, pq, d_skip, w_glu, *, tt=S5_TILE):
    _, B, L, _ = proj.shape
    W = BRANCH_WIDTH
    NS2 = 2 * S5_HALF_STATES
    return pl.pallas_call(
        functools.partial(_s5_kernel, tt=tt),
        grid=(B, L // tt),
        in_specs=[
            _slab_spec(SLAB_S5, tt),
            pl.BlockSpec((2, S5_HALF_CH, NS2), lambda b, t: (0, 0, 0)),
            pl.BlockSpec((2, NS2, S5_HALF_CH), lambda b, t: (0, 0, 0)),
            pl.BlockSpec((2, tt, NS2), lambda b, t: (0, 0, 0)),
            pl.BlockSpec((2, 2 * S5_SEGS, NS2), lambda b, t: (0, 0, 0)),
            pl.BlockSpec((1, W), lambda b, t: (0, 0)),
            pl.BlockSpec((W, 2 * W), lambda b, t: (0, 0)),
        ],
        out_specs=pl.BlockSpec((None, tt, W), lambda b, t: (b, t, 0)),
        out_shape=jax.ShapeDtypeStruct((B, L, W), BF16),
        scratch_shapes=[pltpu.VMEM((2, tt, NS2), F32),
                        pltpu.VMEM((W // 128, S5_SEGS * S5_PITCH, 128), F32),
                        pltpu.VMEM((2, S5_SEGS, NS2), F32)],
        compiler_params=_params(("parallel", "arbitrary")),
        name="s5",
    )(proj, wb, wc, pj, pq, d_skip.reshape(1, W), w_glu)


def _s5_tables(a_re, a_im, log_dt, b_re, b_im, c_re, c_im):
    dt = jnp.exp(log_dt)[:, None]
    mag = jnp.exp(a_re * dt)
    ab_re = mag * jnp.cos(a_im * dt)
    ab_im = mag * jnp.sin(a_im * dt)
    den = a_re * a_re + a_im * a_im
    zr = ((ab_re - 1.0) * a_re + ab_im * a_im) / den
    zi = (ab_im * a_re - (ab_re - 1.0) * a_im) / den
    bb_re = zr[..., None] * b_re - zi[..., None] * b_im
    bb_im = zr[..., None] * b_im + zi[..., None] * b_re
    GH = S5_GROUPS // 2
    eye = jnp.eye(GH, dtype=F32)

    def bdiag_in(m):
        m = m.reshape(2, GH, S5_STATE, S5_GROUP)
        return jnp.einsum('hgpc,gk->hgckp', m, eye).reshape(2, GH * S5_GROUP, GH * S5_STATE)

    def bdiag_out(m):
        m = m.reshape(2, GH, S5_GROUP, S5_STATE)
        return jnp.einsum('hgcp,gk->hgpkc', m, eye).reshape(2, GH * S5_STATE, GH * S5_GROUP)

    wb = jnp.concatenate([bdiag_in(bb_re), bdiag_in(bb_im)], axis=-1).astype(BF16)
    wc = jnp.concatenate([bdiag_out(c_re), -bdiag_out(c_im)], axis=1).astype(BF16)
    sr = S5_TILE // S5_SEGS

    def powers(n):
        n = jnp.asarray(n, F32)[:, None, None]
        mag = jnp.exp(n * (a_re * dt)[None])
        ang = n * (a_im * dt)[None]
        split = lambda p: p.reshape(p.shape[0], 2, S5_HALF_STATES).transpose(1, 0, 2)
        return jnp.concatenate([split(mag * jnp.cos(ang)), split(mag * jnp.sin(ang))], axis=-1)

    pj = powers(np.repeat(np.arange(1, sr + 1), S5_SEGS))
    pq = powers(np.concatenate([sr * np.arange(S5_SEGS), sr * np.arange(1, S5_SEGS + 1)]))
    return wb, wc, pj, pq


def _conv_kernel(a_ref, b_ref, w_ref, cb_ref, lg_ref, lbias_ref, o_ref, z_ref, *, tt, rb):
    H = CONV_HALO

    @pl.when(pl.program_id(1) == 0)
    def _():
        z_ref[0:H, :] = jnp.zeros((H, BRANCH_WIDTH), F32)

    z_ref[H:H + tt, :] = a_ref[...].astype(F32) * _sigmoid(b_ref[...].astype(F32))
    cb = cb_ref[...]
    lg = lg_ref[...]
    lbias = lbias_ref[...]

    def block(k, carry):
        r0 = pl.multiple_of(k * rb, rb)
        parts = []
        for lc in range(0, BRANCH_WIDTH, 128):
            win = z_ref[pl.ds(r0, rb + H), lc:lc + 128]
            acc = jnp.zeros((rb, 128), F32)
            for c in range(8):
                shifted = win if c == 0 else pltpu.roll(win, rb + H - c, 0)
                for off in range(c, H + 1, 8):
                    w = off - (H - CONV_WIDTH + 1)
                    if 0 <= w < CONV_WIDTH:
                        acc = acc + shifted[off - c:off - c + rb, :] * w_ref[w:w + 1, lc:lc + 128]
            parts.append(acc)
        acc = jnp.concatenate(parts, axis=-1) + cb
        xc = acc - jnp.mean(acc, axis=-1, keepdims=True)
        var = jnp.mean(xc * xc, axis=-1, keepdims=True)
        y = xc * lax.rsqrt(var + NORM_EPS) * lg + lbias
        o_ref[pl.ds(r0, rb), :] = (y * _sigmoid(y)).astype(o_ref.dtype)
        return carry

    lax.fori_loop(0, tt // rb, block, 0)
    z_ref[0:H, :] = z_ref[tt:tt + H, :]


def _conv(proj, conv_w, conv_b, ln_g, ln_b, *, tt=512, rb=32):
    _, B, L, _ = proj.shape
    W = BRANCH_WIDTH
    w_pad = jnp.zeros((CONV_HALO, W), F32).at[:CONV_WIDTH].set(conv_w)
    vec = lambda: pl.BlockSpec((1, W), lambda b, t: (0, 0))
    return pl.pallas_call(
        functools.partial(_conv_kernel, tt=tt, rb=rb),
        grid=(B, L // tt),
        in_specs=[_slab_spec(SLAB_CONV_A, tt), _slab_spec(SLAB_CONV_B, tt),
                  pl.BlockSpec((CONV_HALO, W), lambda b, t: (0, 0)), vec(), vec(), vec()],
        out_specs=pl.BlockSpec((None, tt, W), lambda b, t: (b, t, 0)),
        out_shape=jax.ShapeDtypeStruct((B, L, W), BF16),
        scratch_shapes=[pltpu.VMEM((tt + CONV_HALO, W), F32)],
        compiler_params=_params(("parallel", "arbitrary")),
        name="conformer_conv",
    )(proj, proj, w_pad, conv_b.reshape(1, W), ln_g.reshape(1, W), ln_b.reshape(1, W))


def _rope_table_kernel(pos_ref, inv_ref, cos_ref, sin_ref):
    ang = pos_ref[...] * inv_ref[...]
    lane = lax.broadcasted_iota(jnp.int32, ang.shape, 1)
    cos_ref[...] = jnp.cos(ang)
    sin_ref[...] = jnp.where(lane < ROPE_HALF, -jnp.sin(ang), jnp.sin(ang))


def _rope_tables(positions, *, tm=1024):
    T = positions.size
    inv = ROPE_THETA ** (-jnp.arange(ROPE_HALF, dtype=F32) / ROPE_HALF)
    inv = jnp.concatenate([inv, inv, jnp.zeros((ATT_HEAD_DIM - ROPE_DIM,), F32)]).reshape(1, ATT_HEAD_DIM)
    pos = positions.astype(F32).reshape(T, 1)
    return pl.pallas_call(
        _rope_table_kernel,
        grid=(T // tm,),
        in_specs=[pl.BlockSpec((tm, 1), lambda i: (i, 0)), pl.BlockSpec((1, ATT_HEAD_DIM), lambda i: (0, 0))],
        out_specs=[pl.BlockSpec((tm, ATT_HEAD_DIM), lambda i: (i, 0))] * 2,
        out_shape=[jax.ShapeDtypeStruct((T, ATT_HEAD_DIM), F32)] * 2,
        compiler_params=_params(("parallel",)),
        name="rope_tables",
    )(pos, inv)


def _rope(x, cos, sin):
    lane = lax.broadcasted_iota(jnp.int32, x.shape, 1)
    partner = jnp.where(lane < ROPE_HALF, pltpu.roll(x, ATT_HEAD_DIM - ROPE_HALF, 1), pltpu.roll(x, ROPE_HALF, 1))
    return x * cos + partner * sin


DEINT_STRIDE = 4


def _qkv_kernel(h_ref, wq_ref, wk_ref, wv_ref, cos_ref, sin_ref, o_ref, w16_ref, s_ref, t_ref, *, d):
    tm = h_ref.shape[0]

    @pl.when(pl.program_id(0) == 0)
    def _():
        for j, w_ref in enumerate((wq_ref, wk_ref, wv_ref)):
            w16_ref[j] = w_ref[...].astype(BF16)

    h16 = h_ref[...]
    cos, sin = cos_ref[...], sin_ref[...]
    for j in range(3):
        res = jnp.dot(h16, w16_ref[j], preferred_element_type=F32)
        for h in range(ATT_HEADS_PER_GROUP):
            hs = slice(h * ATT_HEAD_DIM, (h + 1) * ATT_HEAD_DIM)
            s_ref[j, h] = _rope(res[:, hs], cos, sin) if j < 2 else res[:, hs]
            if d <= DEINT_STRIDE:
                for r in range(d):
                    o_ref[j, r, :, hs] = s_ref[j, h, pl.ds(r, tm // d, stride=d), :].astype(o_ref.dtype)
            else:
                q = tm // DEINT_STRIDE
                for r0 in range(DEINT_STRIDE):
                    t_ref[j, h, pl.ds(r0 * q, q), :] = s_ref[j, h, pl.ds(r0, q, stride=DEINT_STRIDE), :]
                for r0 in range(DEINT_STRIDE):
                    for r1 in range(d // DEINT_STRIDE):
                        o_ref[j, DEINT_STRIDE * r1 + r0, :, hs] = t_ref[
                            j, h, pl.ds(r0 * q + r1, tm // d, stride=d // DEINT_STRIDE), :].astype(o_ref.dtype)


def _qkv(h, w_in, layer, cos_t, sin_t, gi, d, B, *, tm=1024):
    T, D = h.shape
    L = T // B
    tiles = L // tm
    W = ATT_HEADS_PER_GROUP * ATT_HEAD_DIM
    assert d <= DEINT_STRIDE or d == DEINT_STRIDE * DEINT_STRIDE
    stage = (3, ATT_HEADS_PER_GROUP, tm, ATT_HEAD_DIM)
    w_spec = lambda j: pl.BlockSpec((None, D, SLAB), lambda i: (layer, 0, W_SLAB_Q + gi + 3 * j))
    return pl.pallas_call(
        functools.partial(_qkv_kernel, d=d),
        grid=(T // tm,),
        in_specs=[
            pl.BlockSpec((tm, D), lambda i: (i, 0)),
            w_spec(0), w_spec(1), w_spec(2),
            pl.BlockSpec((tm, ATT_HEAD_DIM), lambda i: (i, 0)),
            pl.BlockSpec((tm, ATT_HEAD_DIM), lambda i: (i, 0)),
        ],
        out_specs=pl.BlockSpec((3, None, d, tm // d, W), lambda i: (0, i // tiles, 0, i % tiles, 0)),
        out_shape=jax.ShapeDtypeStruct((3, B, d, L // d, W), BF16),
        scratch_shapes=[pltpu.VMEM((3, D, SLAB), BF16),
                        pltpu.VMEM(stage, F32),
                        pltpu.VMEM(stage if d > DEINT_STRIDE else (1, 1, 8, ATT_HEAD_DIM), F32)],
        compiler_params=_params(("arbitrary",)),
        name=f"qkv_proj_g{gi}",
    )(h, w_in, w_in, w_in, cos_t, sin_t)


ATT_UNROLL = 4


def _att_kernel(q_ref, kc_ref, kp_ref, vc_ref, vp_ref, o_ref, lse_ref, *, d, nq):
    Q = ATT_QBLOCK
    n = pl.program_id(1)
    qi = lax.broadcasted_iota(jnp.int32, (Q, 2 * Q), 0)
    kj = lax.broadcasted_iota(jnp.int32, (Q, 2 * Q), 1)
    rel = Q + qi - kj
    band = (rel >= 0) & (rel <= Q)
    band_first = band & ((n > 0) | (kj >= Q))
    head_of_lane = lax.broadcasted_iota(jnp.int32, (Q, ATT_HEAD_DIM), 1) // (ATT_HEAD_DIM // ATT_HEADS_PER_GROUP)
    nt = (((1,), (1,)), ((), ()))

    def block(r, i):
        rows = slice(i * Q, (i + 1) * Q)
        outs = []
        lse = jnp.zeros((Q, ATT_HEAD_DIM), F32)
        for h in range(ATT_HEADS_PER_GROUP):
            hs = slice(h * ATT_HEAD_DIM, (h + 1) * ATT_HEAD_DIM)
            if i == 0:
                k = jnp.concatenate([kp_ref[r, :, hs], kc_ref[r, rows, hs]], axis=0)
                v = jnp.concatenate([vp_ref[r, :, hs], vc_ref[r, rows, hs]], axis=0)
            else:
                k = kc_ref[r, (i - 1) * Q:(i + 1) * Q, hs]
                v = vc_ref[r, (i - 1) * Q:(i + 1) * Q, hs]
            s = lax.dot_general(q_ref[r, rows, hs], k, nt, preferred_element_type=F32) * (ATT_HEAD_DIM ** -0.5)
            s = jnp.where(band_first if i == 0 else band, s, MASK_VALUE)
            m = jnp.max(s, axis=-1, keepdims=True)
            p = jnp.exp(s - m)
            l = jnp.sum(p, axis=-1, keepdims=True)
            outs.append(jnp.dot(p.astype(BF16), v, preferred_element_type=F32) / l)
            lse = jnp.where(head_of_lane == h, m + jnp.log(l), lse)
        o_ref[r, rows, :] = jnp.concatenate(outs, axis=-1).astype(o_ref.dtype)
        lse_ref[r, rows, :] = lse

    if d <= ATT_UNROLL:
        for r in range(d):
            for i in range(nq):
                block(r, i)
    else:
        def residues(it, carry):
            for u in range(ATT_UNROLL):
                for i in range(nq):
                    block(it * ATT_UNROLL + u, i)
            return carry

        lax.fori_loop(0, d // ATT_UNROLL, residues, 0)


def _attention_group(qkv, gi):
    _, B, d, n_sub, W = qkv.shape
    Q = ATT_QBLOCK
    nq = max(1, ATT_UNROLL // d)

    def cur(s):
        return pl.BlockSpec((None, None, d, nq * Q, W), lambda b, n: (s, b, 0, n, 0))

    def prev(s):
        return pl.BlockSpec((None, None, d, Q, W), lambda b, n: (s, b, 0, jnp.maximum(nq * n - 1, 0), 0))

    return pl.pallas_call(
        functools.partial(_att_kernel, d=d, nq=nq),
        grid=(B, n_sub // (nq * Q)),
        in_specs=[cur(0), cur(1), prev(1), cur(2), prev(2)],
        out_specs=[pl.BlockSpec((None, d, nq * Q, W), lambda b, n: (b, 0, n, 0)),
                   pl.BlockSpec((None, d, nq * Q, ATT_HEAD_DIM), lambda b, n: (b, 0, n, 0))],
        out_shape=[jax.ShapeDtypeStruct((B, d, n_sub, W), BF16),
                   jax.ShapeDtypeStruct((B, d, n_sub, ATT_HEAD_DIM), F32)],
        compiler_params=_params(("parallel", "arbitrary")),
        name=f"dilated_attention_g{gi}",
    )(qkv, qkv, qkv, qkv, qkv)


def _merge_kernel(x_ref, ya_ref, yb_ref, yc_ref, o1_ref, o2_ref, o3_ref, l1_ref, l2_ref, l3_ref,
                  g0_ref, g1_ref, g2_ref, g3_ref, g4_ref, g5_ref, g6_ref, g7_ref, wb_ref, wo_ref, out_ref,
                  os_ref, ls_ref):
    for g, (o_ref, l_ref) in enumerate(((o1_ref, l1_ref), (o2_ref, l2_ref), (o3_ref, l3_ref))):
        d, rows = o_ref.shape[0], o_ref.shape[1]
        for r in range(d):
            o = o_ref[r].astype(F32)
            for h in range(ATT_HEADS_PER_GROUP):
                os_ref[g, h, pl.ds(r, rows, stride=d), :] = o[:, h * ATT_HEAD_DIM:(h + 1) * ATT_HEAD_DIM]
            ls_ref[g, pl.ds(r, rows, stride=d), :] = l_ref[r]
    lanes = ATT_HEAD_DIM // ATT_HEADS_PER_GROUP
    parts = []
    for h in range(ATT_HEADS_PER_GROUP):
        l1, l2, l3 = (ls_ref[g, :, h * lanes:h * lanes + 1] for g in range(3))
        m = jnp.maximum(jnp.maximum(l1, l2), l3)
        e1, e2, e3 = jnp.exp(l1 - m), jnp.exp(l2 - m), jnp.exp(l3 - m)
        parts.append((e1 * os_ref[0, h] + e2 * os_ref[1, h] + e3 * os_ref[2, h]) / (e1 + e2 + e3))
    yd = jnp.concatenate(parts, axis=-1)
    ys = (ya_ref[...], yb_ref[...], yc_ref[...], yd.astype(BF16))
    gates = ((g0_ref, g1_ref), (g2_ref, g3_ref), (g4_ref, g5_ref), (g6_ref, g7_ref))
    merged = None
    for k in range(N_BRANCHES):
        gate = _sigmoid(jnp.concatenate([gates[k][0][...], gates[k][1][...]], axis=-1).astype(F32))
        term = gate * jnp.dot(ys[k], wb_ref[k], preferred_element_type=F32)
        merged = term if merged is None else merged + term
    out_ref[...] = x_ref[...] + jnp.dot(merged.astype(BF16), wo_ref[...], preferred_element_type=F32)


def _merge(x, ya, yb, yc, att, proj, w_branch, w_out, *, tm=512):
    B, L, D = x.shape
    W = BRANCH_WIDTH
    (o1, l1), (o2, l2), (o3, l3) = att
    row = lambda: pl.BlockSpec((None, tm, W), lambda b, t: (b, t, 0))

    def res(a):
        d, last = a.shape[1], a.shape[3]
        return pl.BlockSpec((None, d, tm // d, last), lambda b, t: (b, 0, t, 0))

    return pl.pallas_call(
        _merge_kernel,
        grid=(B, L // tm),
        in_specs=[pl.BlockSpec((None, tm, D), lambda b, t: (b, t, 0))] + [row() for _ in range(3)]
                 + [res(a) for a in (o1, o2, o3, l1, l2, l3)]
                 + [_slab_spec(SLAB_GATE + s, tm) for s in range(8)]
                 + [pl.BlockSpec((N_BRANCHES, W, D), lambda b, t: (0, 0, 0)), pl.BlockSpec((D, D), lambda b, t: (0, 0))],
        out_specs=pl.BlockSpec((None, tm, D), lambda b, t: (b, t, 0)),
        out_shape=jax.ShapeDtypeStruct((B, L, D), F32),
        scratch_shapes=[pltpu.VMEM((3, ATT_HEADS_PER_GROUP, tm, ATT_HEAD_DIM), F32),
                        pltpu.VMEM((3, tm, ATT_HEAD_DIM), F32)],
        compiler_params=_params(("parallel", "parallel")),
        name="gated_merge",
    )(x, ya, yb, yc, o1, o2, o3, l1, l2, l3, *([proj] * 8), w_branch, w_out)


def kernel(x, positions, ffn1_norm, ffn1_w_gate, ffn1_w_up, ffn1_w_down, mix_norm, w_in, hg_lb_logits, hg_gnorm, s5_a_re, s5_a_im, s5_log_dt, s5_b_re, s5_b_im, s5_c_re, s5_c_im, s5_d, s5_w_glu, conv_w, conv_b, conv_ln_g, conv_ln_b, w_branch, w_out, ffn2_norm, ffn2_w_gate, ffn2_w_up, ffn2_w_down, final_norm):
    B, L, D = x.shape
    T = B * L
    depth = w_in.shape[0]
    lb_soft = jax.nn.softmax(hg_lb_logits.astype(F32), axis=0)
    lb_all = jnp.cumsum(lb_soft, axis=0) - lb_soft[0]
    cos_t, sin_t = _rope_tables(positions)
    bf = lambda w: w.astype(BF16)

    xt = x.reshape(T, D)
    for l in range(depth):
        xt = _ffn(xt, ffn1_norm[l], bf(ffn1_w_gate[l]), bf(ffn1_w_up[l]), bf(ffn1_w_down[l]))
        proj, h_mix = _proj(xt, mix_norm[l], w_in, l)
        proj = proj.reshape(N_SLABS, B, L, SLAB)
        ya = _hgrn(proj, lb_all[l], hg_gnorm[l])
        wb, wc, pj, pq = _s5_tables(s5_a_re[l], s5_a_im[l], s5_log_dt[l], s5_b_re[l], s5_b_im[l], s5_c_re[l], s5_c_im[l])
        yb = _s5(proj, wb, wc, pj, pq, s5_d[l], bf(s5_w_glu[l]))
        yc = _conv(proj, conv_w[l], conv_b[l], conv_ln_g[l], conv_ln_b[l])
        att = [_attention_group(_qkv(h_mix, w_in, l, cos_t, sin_t, gi, dil, B), gi)
               for gi, (_, dil) in enumerate(ATT_CONFIGS)]
        xt = _merge(xt.reshape(B, L, D), ya, yb, yc, att, proj, bf(w_branch[l]), bf(w_out[l])).reshape(T, D)
        last = l == depth - 1
        xt = _ffn(xt, ffn2_norm[l], bf(ffn2_w_gate[l]), bf(ffn2_w_up[l]), bf(ffn2_w_down[l]),
                  final_norm if last else None)
    return xt.reshape(B, L, D)
```

```python
import functools
import math

import numpy as np
import jax
import jax.numpy as jnp
from jax import lax
from jax.experimental import pallas as pl
from jax.experimental.pallas import tpu as pltpu

F32 = jnp.float32
BF16 = jnp.bfloat16

NORM_EPS = 1e-6
MASK_VALUE = -1e30
D_MODEL = 1024
D_FF = 2816
N_BRANCHES = 4
BRANCH_WIDTH = 512

HG_HEADS = 4
HG_KDIM = 128
HG_VDIM = 128
HG_CHUNK = 64
HG_SUB = 16

S5_GROUP = 16
S5_GROUPS = 32
S5_STATE = 64
S5_HALF_CH = 256
S5_HALF_STATES = (S5_GROUPS // 2) * S5_STATE
S5_TILE = 256
S5_SEGS = 8
S5_PITCH = S5_TILE // S5_SEGS + 8
S5_SCAN_LANES = 512

CONV_WIDTH = 31
CONV_HALO = 32

ATT_HEAD_DIM = 128
ATT_CONFIGS = ((128, 1), (512, 4), (2048, 16))
ATT_HEADS_PER_GROUP = 4
ATT_HEADS = 12
ATT_QBLOCK = 128
ROPE_THETA = 500000.0
ROPE_DIM = 32
ROPE_HALF = 16

SLAB = 512
SLAB_HQ, SLAB_HF, SLAB_HI, SLAB_HG, SLAB_S5, SLAB_CONV_A, SLAB_CONV_B = 0, 1, 2, 3, 4, 5, 6
W_SLAB_Q = 7
W_SLAB_GATE = 16
SLABS_PER_STEP = 2
SLAB_PAD = 7
SLAB_GATE = 8
N_SLABS = 16

VMEM_LIMIT = 56 * 1024 * 1024


def _params(sem):
    return pltpu.CompilerParams(dimension_semantics=sem, vmem_limit_bytes=VMEM_LIMIT)


def _rms(x):
    return x * lax.rsqrt(jnp.mean(x * x, axis=-1, keepdims=True) + NORM_EPS)


def _sigmoid(x):
    return jax.nn.sigmoid(x)


def _ffn_kernel(x_ref, gain_ref, wg_ref, wu_ref, wd_ref, *rest, n_ff, final):
    if final:
        fgain_ref, o_ref, h_ref, acc_ref = rest
    else:
        o_ref, h_ref, acc_ref = rest
    j = pl.program_id(1)

    @pl.when(j == 0)
    def _():
        h_ref[...] = (_rms(x_ref[...]) * gain_ref[...]).astype(BF16)
        acc_ref[...] = jnp.zeros_like(acc_ref)

    h = h_ref[...]
    g = jnp.dot(h, wg_ref[...], preferred_element_type=F32)
    u = jnp.dot(h, wu_ref[...], preferred_element_type=F32)
    a = (g * _sigmoid(g) * u).astype(BF16)
    acc_ref[...] += jnp.dot(a, wd_ref[...], preferred_element_type=F32)

    @pl.when(j == n_ff - 1)
    def _():
        y = x_ref[...] + 0.5 * acc_ref[...]
        if final:
            y = _rms(y) * fgain_ref[...]
        o_ref[...] = y


def _ffn(x, gain, wg, wu, wd, final_gain=None, *, tm=512, tf=1408):
    T, D = x.shape
    FF = wg.shape[1]
    n_ff = FF // tf
    final = final_gain is not None
    in_specs = [
        pl.BlockSpec((tm, D), lambda i, j: (i, 0)),
        pl.BlockSpec((1, D), lambda i, j: (0, 0)),
        pl.BlockSpec((D, tf), lambda i, j: (0, j)),
        pl.BlockSpec((D, tf), lambda i, j: (0, j)),
        pl.BlockSpec((tf, D), lambda i, j: (j, 0)),
    ]
    args = [x, gain.reshape(1, D), wg, wu, wd]
    if final:
        in_specs.append(pl.BlockSpec((1, D), lambda i, j: (0, 0)))
        args.append(final_gain.reshape(1, D))
    return pl.pallas_call(
        functools.partial(_ffn_kernel, n_ff=n_ff, final=final),
        grid=(T // tm, n_ff),
        in_specs=in_specs,
        out_specs=pl.BlockSpec((tm, D), lambda i, j: (i, 0)),
        out_shape=jax.ShapeDtypeStruct((T, D), F32),
        scratch_shapes=[pltpu.VMEM((tm, D), BF16), pltpu.VMEM((tm, D), F32)],
        compiler_params=_params(("parallel", "arbitrary")),
        name="ffn_final" if final else "ffn",
    )(*args)


def _proj_kernel(x_ref, gain_ref, w_ref, o_ref, h_ref):
    j = pl.program_id(1)

    @pl.when(j == 0)
    def _():
        h_ref[...] = (_rms(x_ref[...]) * gain_ref[...]).astype(BF16)

    def slab(s):
        w = w_ref[:, s * SLAB:(s + 1) * SLAB].astype(BF16)
        o_ref[s] = jnp.dot(h_ref[...], w, preferred_element_type=F32).astype(o_ref.dtype)

    for s in range(SLABS_PER_STEP):
        if s == SLAB_PAD % SLABS_PER_STEP:
            is_pad = j == SLAB_PAD // SLABS_PER_STEP
            pl.when(jnp.logical_not(is_pad))(functools.partial(slab, s))

            @pl.when(is_pad)
            def _():
                o_ref[s] = jnp.zeros(o_ref.shape[1:], o_ref.dtype)
        else:
            slab(s)


def _proj(x, gain, w_in, layer, *, tm=2048):
    T, D = x.shape
    first_gate_step = SLAB_GATE // SLABS_PER_STEP
    gate_shift = (W_SLAB_GATE - SLAB_GATE) // SLABS_PER_STEP
    slabs, h = pl.pallas_call(
        _proj_kernel,
        grid=(T // tm, N_SLABS // SLABS_PER_STEP),
        in_specs=[
            pl.BlockSpec((tm, D), lambda i, j: (i, 0)),
            pl.BlockSpec((1, D), lambda i, j: (0, 0)),
            pl.BlockSpec((None, D, SLABS_PER_STEP * SLAB),
                         lambda i, j: (layer, 0, jnp.where(j < first_gate_step, j, j + gate_shift))),
        ],
        out_specs=[pl.BlockSpec((SLABS_PER_STEP, tm, SLAB), lambda i, j: (j, i, 0)),
                   pl.BlockSpec((tm, D), lambda i, j: (i, 0))],
        out_shape=[jax.ShapeDtypeStruct((N_SLABS, T, SLAB), BF16), jax.ShapeDtypeStruct((T, D), BF16)],
        compiler_params=_params(("parallel", "arbitrary")),
        name="in_proj",
    )(x, gain.reshape(1, D), w_in)
    return slabs, h


def _slab_spec(slab, rows):
    return pl.BlockSpec((None, None, rows, SLAB), lambda b, t: (slab, b, t, 0))


def _split3_bf16(x):
    hi = x.astype(BF16)
    r = x - hi.astype(F32)
    mid = r.astype(BF16)
    lo = (r - mid.astype(F32)).astype(BF16)
    return hi, mid, lo


def _hgrn_kernel(q_ref, f_ref, i_ref, g_ref, lb_ref, gn_ref, o_ref, st_ref, b_ref, k_ref, v_ref, *, n_chunks):
    C, S = HG_CHUNK, HG_SUB

    @pl.when(pl.program_id(1) == 0)
    def _():
        st_ref[...] = jnp.zeros_like(st_ref)

    lb = lb_ref[...]
    gn = gn_ref[...]
    row = lax.broadcasted_iota(jnp.int32, (C, C), 0)
    col = lax.broadcasted_iota(jnp.int32, (C, C), 1)
    tri = (col <= row).astype(BF16)
    row8 = lax.broadcasted_iota(jnp.int32, (8, 1), 0)
    nt = (((1,), (1,)), ((), ()))
    tn = (((0,), (0,)), ((), ()))

    def chunk(c):
        r0 = c * C
        q = q_ref[pl.ds(r0, C), :].astype(F32)
        f = f_ref[pl.ds(r0, C), :].astype(F32)
        v = i_ref[pl.ds(r0, C), :].astype(F32)
        g = g_ref[pl.ds(r0, C), :].astype(F32)
        v16 = v.astype(BF16)

        qf = q * _sigmoid(q) * (HG_KDIM ** -0.5)
        sf = _sigmoid(f)
        kf = (1.0 - lb) * (1.0 - sf)
        logf = jnp.log(lb + (1.0 - lb) * sf)
        b = sum(jnp.dot(tri, p, preferred_element_type=F32) for p in _split3_bf16(logf))
        b_last = b[C - 1:C, :]

        qd = (qf * jnp.exp(b)).astype(BF16)
        kd_last = (kf * jnp.exp(b_last - b)).astype(BF16)

        diag = []
        for h in range(HG_HEADS):
            hs = slice(h * HG_KDIM, (h + 1) * HG_KDIM)
            b_ref[c, h] = b[:, hs]
            k_ref[c, h] = kf[:, hs]
            v_ref[c, h] = v[:, hs]
            tiles = []
            for base in range(0, C, S):
                for lo in range(0, S, 8):
                    qq = qf[base + lo:base + lo + 8, hs]
                    bb = b[base + lo:base + lo + 8, hs]
                    acc = jnp.zeros((8, HG_VDIM), F32)
                    for j in range(lo + 8):
                        row = pl.ds(base + j, 8, stride=0)
                        p = qq * k_ref[c, h, row, :] * jnp.exp(jnp.minimum(bb - b_ref[c, h, row, :], 0.0))
                        a = jnp.sum(p, axis=-1, keepdims=True)
                        if j >= lo:
                            a = jnp.where(row8 >= j - lo, a, 0.0)
                        acc = acc + a * v_ref[c, h, row, :]
                    tiles.append(acc)
            diag.append(jnp.concatenate(tiles, axis=0))

        outs = []
        for h in range(HG_HEADS):
            hs = slice(h * HG_KDIM, (h + 1) * HG_KDIM)
            st = st_ref[h]
            o = lax.dot_general(qd[:, hs], st.astype(BF16), nt, preferred_element_type=F32) + diag[h]
            off = [jnp.zeros((S, HG_VDIM), F32)]
            for i in range(1, C // S):
                r = b[S * i - 1:S * i, hs]
                qi = (qf[S * i:S * (i + 1), hs] * jnp.exp(b[S * i:S * (i + 1), hs] - r)).astype(BF16)
                kj = (kf[:S * i, hs] * jnp.exp(r - b[:S * i, hs])).astype(BF16)
                att = lax.dot_general(qi, kj, nt, preferred_element_type=F32)
                off.append(jnp.dot(att.astype(BF16), v16[:S * i, hs], preferred_element_type=F32))
            o = o + jnp.concatenate(off, axis=0)
            st_ref[h] = jnp.exp(b_last[:, hs]) * st + lax.dot_general(
                v16[:, hs], kd_last[:, hs], tn, preferred_element_type=F32)
            o = o * lax.rsqrt(jnp.mean(o * o, axis=-1, keepdims=True) + NORM_EPS)
            outs.append(o)
        o = jnp.concatenate(outs, axis=-1) * gn * (g * _sigmoid(g))
        o_ref[pl.ds(r0, C), :] = o.astype(o_ref.dtype)

    for c in range(n_chunks):
        chunk(c)


def _hgrn(proj, lb, gnorm, *, tt=256):
    _, B, L, _ = proj.shape
    W = HG_HEADS * HG_KDIM
    return pl.pallas_call(
        functools.partial(_hgrn_kernel, n_chunks=tt // HG_CHUNK),
        grid=(B, L // tt),
        in_specs=[
            _slab_spec(SLAB_HQ, tt), _slab_spec(SLAB_HF, tt), _slab_spec(SLAB_HI, tt), _slab_spec(SLAB_HG, tt),
            pl.BlockSpec((1, W), lambda b, t: (0, 0)),
            pl.BlockSpec((1, W), lambda b, t: (0, 0)),
        ],
        out_specs=pl.BlockSpec((None, tt, W), lambda b, t: (b, t, 0)),
        out_shape=jax.ShapeDtypeStruct((B, L, W), BF16),
        scratch_shapes=[pltpu.VMEM((HG_HEADS, HG_VDIM, HG_KDIM), F32)] + [pltpu.VMEM((tt // HG_CHUNK, HG_HEADS, HG_CHUNK, HG_KDIM), F32)] * 3,
        compiler_params=_params(("parallel", "arbitrary")),
        name="hgrn2",
    )(proj, proj, proj, proj, lb.reshape(1, W), gnorm.reshape(1, W))


def _cmul(ar, ai, br, bi):
    return ar * br - ai * bi, ar * bi + ai * br


def _s5_kernel(u_ref, wb_ref, wc_ref, pj_ref, pq_ref, d_ref, wglu_ref, o_ref, x_ref, rq_ref, carry_ref, *, tt):
    SEGS, SR, PITCH, NS, LC = S5_SEGS, tt // S5_SEGS, S5_PITCH, S5_HALF_STATES, S5_SCAN_LANES
    row8 = lax.broadcasted_iota(jnp.int32, (SEGS, LC), 0)
    lane_tiles = range(BRANCH_WIDTH // 128)

    @pl.when(pl.program_id(1) == 0)
    def _():
        carry_ref[...] = jnp.zeros_like(carry_ref)

    u = u_ref[...].astype(F32)
    for lt in lane_tiles:
        for s in range(SEGS):
            rq_ref[lt, pl.ds(PITCH * s, SR), :] = u[SR * s:SR * (s + 1), lt * 128:(lt + 1) * 128]
    u = jnp.concatenate(
        [jnp.concatenate([rq_ref[lt, pl.ds(j, SEGS, stride=PITCH), :] for lt in lane_tiles], axis=-1)
         for j in range(SR)], axis=0)
    u16 = u.astype(BF16)

    for hf in range(2):
        x_ref[hf] = jnp.dot(u16[:, hf * S5_HALF_CH:(hf + 1) * S5_HALF_CH], wb_ref[hf],
                            preferred_element_type=F32)

    for hf in range(2):
        for lc in range(NS // LC):
            re = slice(lc * LC, (lc + 1) * LC)
            im = slice(NS + lc * LC, NS + (lc + 1) * LC)
            a1r, a1i = pj_ref[hf, 0:SEGS, re], pj_ref[hf, 0:SEGS, im]

            def recur(j, state, hf=hf, re=re, im=im, a1r=a1r, a1i=a1i):
                r0 = pl.multiple_of(j * SEGS, SEGS)
                xr, xi = _cmul(a1r, a1i, state[0], state[1])
                xr = xr + x_ref[hf, pl.ds(r0, SEGS), re]
                xi = xi + x_ref[hf, pl.ds(r0, SEGS), im]
                x_ref[hf, pl.ds(r0, SEGS), re] = xr
                x_ref[hf, pl.ds(r0, SEGS), im] = xi
                return xr, xi

            zero = jnp.zeros((SEGS, LC), F32)
            er, ei = lax.fori_loop(0, SR, recur, (zero, zero))

            for shift in (1, 2, 4):
                cr = jnp.where(row8 >= shift, pq_ref[hf, shift:shift + 1, re], 0.0)
                ci = jnp.where(row8 >= shift, pq_ref[hf, shift:shift + 1, im], 0.0)
                dr, di = _cmul(cr, ci, pltpu.roll(er, shift, 0), pltpu.roll(ei, shift, 0))
                er, ei = er + dr, ei + di
            c0r = carry_ref[hf, SEGS - 1:SEGS, re]
            c0i = carry_ref[hf, SEGS - 1:SEGS, im]
            sr, si = _cmul(pq_ref[hf, 0:SEGS, re], pq_ref[hf, 0:SEGS, im], c0r, c0i)
            sr = sr + jnp.where(row8 >= 1, pltpu.roll(er, 1, 0), 0.0)
            si = si + jnp.where(row8 >= 1, pltpu.roll(ei, 1, 0), 0.0)
            nr, ni = _cmul(pq_ref[hf, SEGS:2 * SEGS, re], pq_ref[hf, SEGS:2 * SEGS, im], c0r, c0i)
            carry_ref[hf, :, re] = er + nr
            carry_ref[hf, :, im] = ei + ni

            def fix(j, carry, hf=hf, re=re, im=im, sr=sr, si=si):
                r0 = pl.multiple_of(j * SEGS, SEGS)
                dr, di = _cmul(pj_ref[hf, pl.ds(r0, SEGS), re], pj_ref[hf, pl.ds(r0, SEGS), im], sr, si)
                x_ref[hf, pl.ds(r0, SEGS), re] = x_ref[hf, pl.ds(r0, SEGS), re] + dr
                x_ref[hf, pl.ds(r0, SEGS), im] = x_ref[hf, pl.ds(r0, SEGS), im] + di
                return carry

            lax.fori_loop(0, SR, fix, 0)

    y = jnp.concatenate(
        [jnp.dot(x_ref[hf].astype(BF16), wc_ref[hf], preferred_element_type=F32) for hf in range(2)], axis=-1)
    y = y + d_ref[...] * u
    z = jax.nn.gelu(y).astype(BF16)
    zz = jnp.dot(z, wglu_ref[...], preferred_element_type=F32)
    out = zz[:, :BRANCH_WIDTH] * _sigmoid(zz[:, BRANCH_WIDTH:])
    for lt in lane_tiles:
        for j in range(SR):
            rq_ref[lt, pl.ds(j, SEGS, stride=PITCH), :] = out[SEGS * j:SEGS * (j + 1), lt * 128:(lt + 1) * 128]
    for lt in lane_tiles:
        for s in range(SEGS):
            o_ref[SR * s:SR * (s + 1), lt * 128:(lt + 1) * 128] = rq_ref[lt, pl.ds(PITCH * s, SR), :].astype(o_ref.dtype)


def _s5(proj, wb, wc, pj, pq, d_skip, w_glu, *, tt=S5_TILE):
    _, B, L, _ = proj.shape
    W = BRANCH_WIDTH
    NS2 = 2 * S5_HALF_STATES
    return pl.pallas_call(
        functools.partial(_s5_kernel, tt=tt),
        grid=(B, L // tt),
        in_specs=[
            _slab_spec(SLAB_S5, tt),
            pl.BlockSpec((2, S5_HALF_CH, NS2), lambda b, t: (0, 0, 0)),
            pl.BlockSpec((2, NS2, S5_HALF_CH), lambda b, t: (0, 0, 0)),
            pl.BlockSpec((2, tt, NS2), lambda b, t: (0, 0, 0)),
            pl.BlockSpec((2, 2 * S5_SEGS, NS2), lambda b, t: (0, 0, 0)),
            pl.BlockSpec((1, W), lambda b, t: (0, 0)),
            pl.BlockSpec((W, 2 * W), lambda b, t: (0, 0)),
        ],
        out_specs=pl.BlockSpec((None, tt, W), lambda b, t: (b, t, 0)),
        out_shape=jax.ShapeDtypeStruct((B, L, W), BF16),
        scratch_shapes=[pltpu.VMEM((2, tt, NS2), F32),
                        pltpu.VMEM((W // 128, S5_SEGS * S5_PITCH, 128), F32),
                        pltpu.VMEM((2, S5_SEGS, NS2), F32)],
        compiler_params=_params(("parallel", "arbitrary")),
        name="s5",
    )(proj, wb, wc, pj, pq, d_skip.reshape(1, W), w_glu)


def _s5_tables(a_re, a_im, log_dt, b_re, b_im, c_re, c_im):
    dt = jnp.exp(log_dt)[:, None]
    mag = jnp.exp(a_re * dt)
    ab_re = mag * jnp.cos(a_im * dt)
    ab_im = mag * jnp.sin(a_im * dt)
    den = a_re * a_re + a_im * a_im
    zr = ((ab_re - 1.0) * a_re + ab_im * a_im) / den
    zi = (ab_im * a_re - (ab_re - 1.0) * a_im) / den
    bb_re = zr[..., None] * b_re - zi[..., None] * b_im
    bb_im = zr[..., None] * b_im + zi[..., None] * b_re
    GH = S5_GROUPS // 2
    eye = jnp.eye(GH, dtype=F32)

    def bdiag_in(m):
        m = m.reshape(2, GH, S5_STATE, S5_GROUP)
        return jnp.einsum('hgpc,gk->hgckp', m, eye).reshape(2, GH * S5_GROUP, GH * S5_STATE)

    def bdiag_out(m):
        m = m.reshape(2, GH, S5_GROUP, S5_STATE)
        return jnp.einsum('hgcp,gk->hgpkc', m, eye).reshape(2, GH * S5_STATE, GH * S5_GROUP)

    wb = jnp.concatenate([bdiag_in(bb_re), bdiag_in(bb_im)], axis=-1).astype(BF16)
    wc = jnp.concatenate([bdiag_out(c_re), -bdiag_out(c_im)], axis=1).astype(BF16)
    sr = S5_TILE // S5_SEGS

    def powers(n):
        n = jnp.asarray(n, F32)[:, None, None]
        mag = jnp.exp(n * (a_re * dt)[None])
        ang = n * (a_im * dt)[None]
        split = lambda p: p.reshape(p.shape[0], 2, S5_HALF_STATES).transpose(1, 0, 2)
        return jnp.concatenate([split(mag * jnp.cos(ang)), split(mag * jnp.sin(ang))], axis=-1)

    pj = powers(np.repeat(np.arange(1, sr + 1), S5_SEGS))
    pq = powers(np.concatenate([sr * np.arange(S5_SEGS), sr * np.arange(1, S5_SEGS + 1)]))
    return wb, wc, pj, pq


def _conv_kernel(a_ref, b_ref, w_ref, cb_ref, lg_ref, lbias_ref, o_ref, z_ref, y_ref, *, tt, rb):
    H = CONV_HALO

    @pl.when(pl.program_id(1) == 0)
    def _():
        z_ref[0:H, :] = jnp.zeros((H, BRANCH_WIDTH), F32)

    z_ref[H:H + tt, :] = a_ref[...].astype(F32) * _sigmoid(b_ref[...].astype(F32))
    cb = cb_ref[...]
    lg = lg_ref[...]
    lbias = lbias_ref[...]

    def block(k, carry):
        r0 = pl.multiple_of(k * rb, rb)
        for lc in range(0, BRANCH_WIDTH, 128):
            win = z_ref[pl.ds(r0, rb + H), lc:lc + 128]
            acc = [jnp.zeros((rb, 128), F32), jnp.zeros((rb, 128), F32)]
            for c in range(8):
                shifted = win if c == 0 else pltpu.roll(win, rb + H - c, 0)
                for off in range(c, H + 1, 8):
                    w = off - (H - CONV_WIDTH + 1)
                    if 0 <= w < CONV_WIDTH:
                        acc[w % 2] = acc[w % 2] + shifted[off - c:off - c + rb, :] * w_ref[w:w + 1, lc:lc + 128]
            y_ref[pl.ds(r0, rb), lc:lc + 128] = acc[0] + acc[1]
        return carry

    lax.fori_loop(0, tt // rb, block, 0)
    z_ref[0:H, :] = z_ref[tt:tt + H, :]

    acc = y_ref[...] + cb
    xc = acc - jnp.mean(acc, axis=-1, keepdims=True)
    var = jnp.mean(xc * xc, axis=-1, keepdims=True)
    y = xc * lax.rsqrt(var + NORM_EPS) * lg + lbias
    o_ref[...] = (y * _sigmoid(y)).astype(o_ref.dtype)


def _conv(proj, conv_w, conv_b, ln_g, ln_b, *, tt=512, rb=32):
    _, B, L, _ = proj.shape
    W = BRANCH_WIDTH
    w_pad = jnp.zeros((CONV_HALO, W), F32).at[:CONV_WIDTH].set(conv_w)
    vec = lambda: pl.BlockSpec((1, W), lambda b, t: (0, 0))
    return pl.pallas_call(
        functools.partial(_conv_kernel, tt=tt, rb=rb),
        grid=(B, L // tt),
        in_specs=[_slab_spec(SLAB_CONV_A, tt), _slab_spec(SLAB_CONV_B, tt),
                  pl.BlockSpec((CONV_HALO, W), lambda b, t: (0, 0)), vec(), vec(), vec()],
        out_specs=pl.BlockSpec((None, tt, W), lambda b, t: (b, t, 0)),
        out_shape=jax.ShapeDtypeStruct((B, L, W), BF16),
        scratch_shapes=[pltpu.VMEM((tt + CONV_HALO, W), F32), pltpu.VMEM((tt, W), F32)],
        compiler_params=_params(("parallel", "arbitrary")),
        name="conformer_conv",
    )(proj, proj, w_pad, conv_b.reshape(1, W), ln_g.reshape(1, W), ln_b.reshape(1, W))


def _rope_table_kernel(pos_ref, inv_ref, cos_ref, sin_ref):
    ang = pos_ref[...] * inv_ref[...]
    lane = lax.broadcasted_iota(jnp.int32, ang.shape, 1)
    cos_ref[...] = jnp.cos(ang)
    sin_ref[...] = jnp.where(lane < ROPE_HALF, -jnp.sin(ang), jnp.sin(ang))


def _rope_tables(positions, *, tm=1024):
    T = positions.size
    inv = ROPE_THETA ** (-jnp.arange(ROPE_HALF, dtype=F32) / ROPE_HALF)
    inv = jnp.concatenate([inv, inv, jnp.zeros((ATT_HEAD_DIM - ROPE_DIM,), F32)]).reshape(1, ATT_HEAD_DIM)
    pos = positions.astype(F32).reshape(T, 1)
    return pl.pallas_call(
        _rope_table_kernel,
        grid=(T // tm,),
        in_specs=[pl.BlockSpec((tm, 1), lambda i: (i, 0)), pl.BlockSpec((1, ATT_HEAD_DIM), lambda i: (0, 0))],
        out_specs=[pl.BlockSpec((tm, ATT_HEAD_DIM), lambda i: (i, 0))] * 2,
        out_shape=[jax.ShapeDtypeStruct((T, ATT_HEAD_DIM), F32)] * 2,
        compiler_params=_params(("parallel",)),
        name="rope_tables",
    )(pos, inv)


def _rope(x, cos, sin):
    lane = lax.broadcasted_iota(jnp.int32, x.shape, 1)
    partner = jnp.where(lane < ROPE_HALF, pltpu.roll(x, ATT_HEAD_DIM - ROPE_HALF, 1), pltpu.roll(x, ROPE_HALF, 1))
    return x * cos + partner * sin


DEINT_STRIDE = 4


def _qkv_kernel(h_ref, wq_ref, wk_ref, wv_ref, cos_ref, sin_ref, o_ref, w16_ref, s_ref, t_ref, *, d):
    tm = h_ref.shape[0]

    @pl.when(pl.program_id(0) == 0)
    def _():
        for j, w_ref in enumerate((wq_ref, wk_ref, wv_ref)):
            w16_ref[j] = w_ref[...].astype(BF16)

    h16 = h_ref[...]
    cos, sin = cos_ref[...], sin_ref[...]
    for j in range(3):
        res = jnp.dot(h16, w16_ref[j], preferred_element_type=F32)
        for h in range(ATT_HEADS_PER_GROUP):
            hs = slice(h * ATT_HEAD_DIM, (h + 1) * ATT_HEAD_DIM)
            s_ref[j, h] = _rope(res[:, hs], cos, sin) if j < 2 else res[:, hs]
            if d <= DEINT_STRIDE:
                for r in range(d):
                    o_ref[j, r, :, hs] = s_ref[j, h, pl.ds(r, tm // d, stride=d), :].astype(o_ref.dtype)
            else:
                q = tm // DEINT_STRIDE
                for r0 in range(DEINT_STRIDE):
                    t_ref[j, h, pl.ds(r0 * q, q), :] = s_ref[j, h, pl.ds(r0, q, stride=DEINT_STRIDE), :]
                for r0 in range(DEINT_STRIDE):
                    for r1 in range(d // DEINT_STRIDE):
                        o_ref[j, DEINT_STRIDE * r1 + r0, :, hs] = t_ref[
                            j, h, pl.ds(r0 * q + r1, tm // d, stride=d // DEINT_STRIDE), :].astype(o_ref.dtype)


def _qkv(h, w_in, layer, cos_t, sin_t, gi, d, B, *, tm=1024):
    T, D = h.shape
    L = T // B
    tiles = L // tm
    W = ATT_HEADS_PER_GROUP * ATT_HEAD_DIM
    assert d <= DEINT_STRIDE or d == DEINT_STRIDE * DEINT_STRIDE
    stage = (3, ATT_HEADS_PER_GROUP, tm, ATT_HEAD_DIM)
    w_spec = lambda j: pl.BlockSpec((None, D, SLAB), lambda i: (layer, 0, W_SLAB_Q + gi + 3 * j))
    return pl.pallas_call(
        functools.partial(_qkv_kernel, d=d),
        grid=(T // tm,),
        in_specs=[
            pl.BlockSpec((tm, D), lambda i: (i, 0)),
            w_spec(0), w_spec(1), w_spec(2),
            pl.BlockSpec((tm, ATT_HEAD_DIM), lambda i: (i, 0)),
            pl.BlockSpec((tm, ATT_HEAD_DIM), lambda i: (i, 0)),
        ],
        out_specs=pl.BlockSpec((3, None, d, tm // d, W), lambda i: (0, i // tiles, 0, i % tiles, 0)),
        out_shape=jax.ShapeDtypeStruct((3, B, d, L // d, W), BF16),
        scratch_shapes=[pltpu.VMEM((3, D, SLAB), BF16),
                        pltpu.VMEM(stage, F32),
                        pltpu.VMEM(stage if d > DEINT_STRIDE else (1, 1, 8, ATT_HEAD_DIM), F32)],
        compiler_params=_params(("arbitrary",)),
        name=f"qkv_proj_g{gi}",
    )(h, w_in, w_in, w_in, cos_t, sin_t)


ATT_UNROLL = 4


def _att_kernel(q_ref, kc_ref, kp_ref, vc_ref, vp_ref, o_ref, lse_ref, *, d, nq):
    Q = ATT_QBLOCK
    n = pl.program_id(1)
    qi = lax.broadcasted_iota(jnp.int32, (Q, 2 * Q), 0)
    kj = lax.broadcasted_iota(jnp.int32, (Q, 2 * Q), 1)
    rel = Q + qi - kj
    band = (rel >= 0) & (rel <= Q)
    band_first = band & ((n > 0) | (kj >= Q))
    head_of_lane = lax.broadcasted_iota(jnp.int32, (Q, ATT_HEAD_DIM), 1) // (ATT_HEAD_DIM // ATT_HEADS_PER_GROUP)
    nt = (((1,), (1,)), ((), ()))

    def block(r, i):
        rows = slice(i * Q, (i + 1) * Q)
        outs = []
        lse = jnp.zeros((Q, ATT_HEAD_DIM), F32)
        for h in range(ATT_HEADS_PER_GROUP):
            hs = slice(h * ATT_HEAD_DIM, (h + 1) * ATT_HEAD_DIM)
            if i == 0:
                k = jnp.concatenate([kp_ref[r, :, hs], kc_ref[r, rows, hs]], axis=0)
                v = jnp.concatenate([vp_ref[r, :, hs], vc_ref[r, rows, hs]], axis=0)
            else:
                k = kc_ref[r, (i - 1) * Q:(i + 1) * Q, hs]
                v = vc_ref[r, (i - 1) * Q:(i + 1) * Q, hs]
            s = lax.dot_general(q_ref[r, rows, hs], k, nt, preferred_element_type=F32) * (ATT_HEAD_DIM ** -0.5)
            s = jnp.where(band_first if i == 0 else band, s, MASK_VALUE)
            m = jnp.max(s, axis=-1, keepdims=True)
            p = jnp.exp(s - m)
            l = jnp.sum(p, axis=-1, keepdims=True)
            outs.append(jnp.dot(p.astype(BF16), v, preferred_element_type=F32) / l)
            lse = jnp.where(head_of_lane == h, m + jnp.log(l), lse)
        o_ref[r, rows, :] = jnp.concatenate(outs, axis=-1).astype(o_ref.dtype)
        lse_ref[r, rows, :] = lse

    if d <= ATT_UNROLL:
        for r in range(d):
            for i in range(nq):
                block(r, i)
    else:
        def residues(it, carry):
            for u in range(ATT_UNROLL):
                for i in range(nq):
                    block(it * ATT_UNROLL + u, i)
            return carry

        lax.fori_loop(0, d // ATT_UNROLL, residues, 0)


def _attention_group(qkv, gi):
    _, B, d, n_sub, W = qkv.shape
    Q = ATT_QBLOCK
    nq = max(1, ATT_UNROLL // d)

    def cur(s):
        return pl.BlockSpec((None, None, d, nq * Q, W), lambda b, n: (s, b, 0, n, 0))

    def prev(s):
        return pl.BlockSpec((None, None, d, Q, W), lambda b, n: (s, b, 0, jnp.maximum(nq * n - 1, 0), 0))

    return pl.pallas_call(
        functools.partial(_att_kernel, d=d, nq=nq),
        grid=(B, n_sub // (nq * Q)),
        in_specs=[cur(0), cur(1), prev(1), cur(2), prev(2)],
        out_specs=[pl.BlockSpec((None, d, nq * Q, W), lambda b, n: (b, 0, n, 0)),
                   pl.BlockSpec((None, d, nq * Q, ATT_HEAD_DIM), lambda b, n: (b, 0, n, 0))],
        out_shape=[jax.ShapeDtypeStruct((B, d, n_sub, W), BF16),
                   jax.ShapeDtypeStruct((B, d, n_sub, ATT_HEAD_DIM), F32)],
        compiler_params=_params(("parallel", "arbitrary")),
        name=f"dilated_attention_g{gi}",
    )(qkv, qkv, qkv, qkv, qkv)


def _merge_kernel(x_ref, ya_ref, yb_ref, yc_ref, o1_ref, o2_ref, o3_ref, l1_ref, l2_ref, l3_ref,
                  g0_ref, g1_ref, g2_ref, g3_ref, g4_ref, g5_ref, g6_ref, g7_ref, wb_ref, wo_ref, out_ref,
                  os_ref, ls_ref):
    for g, (o_ref, l_ref) in enumerate(((o1_ref, l1_ref), (o2_ref, l2_ref), (o3_ref, l3_ref))):
        d, rows = o_ref.shape[0], o_ref.shape[1]
        for r in range(d):
            o = o_ref[r].astype(F32)
            for h in range(ATT_HEADS_PER_GROUP):
                os_ref[g, h, pl.ds(r, rows, stride=d), :] = o[:, h * ATT_HEAD_DIM:(h + 1) * ATT_HEAD_DIM]
            ls_ref[g, pl.ds(r, rows, stride=d), :] = l_ref[r]
    lanes = ATT_HEAD_DIM // ATT_HEADS_PER_GROUP
    parts = []
    for h in range(ATT_HEADS_PER_GROUP):
        l1, l2, l3 = (ls_ref[g, :, h * lanes:h * lanes + 1] for g in range(3))
        m = jnp.maximum(jnp.maximum(l1, l2), l3)
        e1, e2, e3 = jnp.exp(l1 - m), jnp.exp(l2 - m), jnp.exp(l3 - m)
        parts.append((e1 * os_ref[0, h] + e2 * os_ref[1, h] + e3 * os_ref[2, h]) / (e1 + e2 + e3))
    yd = jnp.concatenate(parts, axis=-1)
    ys = (ya_ref[...], yb_ref[...], yc_ref[...], yd.astype(BF16))
    gates = ((g0_ref, g1_ref), (g2_ref, g3_ref), (g4_ref, g5_ref), (g6_ref, g7_ref))
    merged = None
    for k in range(N_BRANCHES):
        gate = _sigmoid(jnp.concatenate([gates[k][0][...], gates[k][1][...]], axis=-1).astype(F32))
        term = gate * jnp.dot(ys[k], wb_ref[k], preferred_element_type=F32)
        merged = term if merged is None else merged + term
    out_ref[...] = x_ref[...] + jnp.dot(merged.astype(BF16), wo_ref[...], preferred_element_type=F32)


def _merge(x, ya, yb, yc, att, proj, w_branch, w_out, *, tm=512):
    B, L, D = x.shape
    W = BRANCH_WIDTH
    (o1, l1), (o2, l2), (o3, l3) = att
    row = lambda: pl.BlockSpec((None, tm, W), lambda b, t: (b, t, 0))

    def res(a):
        d, last = a.shape[1], a.shape[3]
        return pl.BlockSpec((None, d, tm // d, last), lambda b, t: (b, 0, t, 0))

    return pl.pallas_call(
        _merge_kernel,
        grid=(B, L // tm),
        in_specs=[pl.BlockSpec((None, tm, D), lambda b, t: (b, t, 0))] + [row() for _ in range(3)]
                 + [res(a) for a in (o1, o2, o3, l1, l2, l3)]
                 + [_slab_spec(SLAB_GATE + s, tm) for s in range(8)]
                 + [pl.BlockSpec((N_BRANCHES, W, D), lambda b, t: (0, 0, 0)), pl.BlockSpec((D, D), lambda b, t: (0, 0))],
        out_specs=pl.BlockSpec((None, tm, D), lambda b, t: (b, t, 0)),
        out_shape=jax.ShapeDtypeStruct((B, L, D), F32),
        scratch_shapes=[pltpu.VMEM((3, ATT_HEADS_PER_GROUP, tm, ATT_HEAD_DIM), F32),
                        pltpu.VMEM((3, tm, ATT_HEAD_DIM), F32)],
        compiler_params=_params(("parallel", "parallel")),
        name="gated_merge",
    )(x, ya, yb, yc, o1, o2, o3, l1, l2, l3, *([proj] * 8), w_branch, w_out)


def kernel(x, positions, ffn1_norm, ffn1_w_gate, ffn1_w_up, ffn1_w_down, mix_norm, w_in, hg_lb_logits, hg_gnorm, s5_a_re, s5_a_im, s5_log_dt, s5_b_re, s5_b_im, s5_c_re, s5_c_im, s5_d, s5_w_glu, conv_w, conv_b, conv_ln_g, conv_ln_b, w_branch, w_out, ffn2_norm, ffn2_w_gate, ffn2_w_up, ffn2_w_down, final_norm):
    B, L, D = x.shape
    T = B * L
    depth = w_in.shape[0]
    lb_soft = jax.nn.softmax(hg_lb_logits.astype(F32), axis=0)
    lb_all = jnp.cumsum(lb_soft, axis=0) - lb_soft[0]
    cos_t, sin_t = _rope_tables(positions)
    bf = lambda w: w.astype(BF16)

    xt = x.reshape(T, D)
    for l in range(depth):
        xt = _ffn(xt, ffn1_norm[l], bf(ffn1_w_gate[l]), bf(ffn1_w_up[l]), bf(ffn1_w_down[l]))
        proj, h_mix = _proj(xt, mix_norm[l], w_in, l)
        proj = proj.reshape(N_SLABS, B, L, SLAB)
        ya = _hgrn(proj, lb_all[l], hg_gnorm[l])
        wb, wc, pj, pq = _s5_tables(s5_a_re[l], s5_a_im[l], s5_log_dt[l], s5_b_re[l], s5_b_im[l], s5_c_re[l], s5_c_im[l])
        yb = _s5(proj, wb, wc, pj, pq, s5_d[l], bf(s5_w_glu[l]))
        yc = _conv(proj, conv_w[l], conv_b[l], conv_ln_g[l], conv_ln_b[l])
        att = [_attention_group(_qkv(h_mix, w_in, l, cos_t, sin_t, gi, dil, B), gi)
               for gi, (_, dil) in enumerate(ATT_CONFIGS)]
        xt = _merge(xt.reshape(B, L, D), ya, yb, yc, att, proj, bf(w_branch[l]), bf(w_out[l])).reshape(T, D)
        last = l == depth - 1
        xt = _ffn(xt, ffn2_norm[l], bf(ffn2_w_gate[l]), bf(ffn2_w_up[l]), bf(ffn2_w_down[l]),
                  final_norm if last else None)
    return xt.reshape(B, L, D)
```

```python
import functools
import math

import numpy as np
import jax
import jax.numpy as jnp
from jax import lax
from jax.experimental import pallas as pl
from jax.experimental.pallas import tpu as pltpu

F32 = jnp.float32
BF16 = jnp.bfloat16

NORM_EPS = 1e-6
MASK_VALUE = -1e30
D_MODEL = 1024
D_FF = 2816
N_BRANCHES = 4
BRANCH_WIDTH = 512

HG_HEADS = 4
HG_KDIM = 128
HG_VDIM = 128
HG_CHUNK = 64
HG_SUB = 16
HG_ACC_CHAINS = 4

S5_GROUP = 16
S5_GROUPS = 32
S5_STATE = 64
S5_HALF_CH = 256
S5_HALF_STATES = (S5_GROUPS // 2) * S5_STATE
S5_TILE = 256
S5_SEGS = 8
S5_PITCH = S5_TILE // S5_SEGS + 8
S5_SCAN_LANES = 512

CONV_WIDTH = 31
CONV_HALO = 32

ATT_HEAD_DIM = 128
ATT_CONFIGS = ((128, 1), (512, 4), (2048, 16))
ATT_HEADS_PER_GROUP = 4
ATT_HEADS = 12
ATT_QBLOCK = 128
ROPE_THETA = 500000.0
ROPE_DIM = 32
ROPE_HALF = 16

SLAB = 512
SLAB_HQ, SLAB_HF, SLAB_HI, SLAB_HG, SLAB_S5, SLAB_CONV_A, SLAB_CONV_B = 0, 1, 2, 3, 4, 5, 6
W_SLAB_Q = 7
W_SLAB_GATE = 16
SLABS_PER_STEP = 2
SLAB_PAD = 7
SLAB_GATE = 8
N_SLABS = 16

VMEM_LIMIT = 56 * 1024 * 1024


def _params(sem):
    return pltpu.CompilerParams(dimension_semantics=sem, vmem_limit_bytes=VMEM_LIMIT)


def _rms(x):
    return x * lax.rsqrt(jnp.mean(x * x, axis=-1, keepdims=True) + NORM_EPS)


def _sigmoid(x):
    return jax.nn.sigmoid(x)


def _ffn_kernel(x_ref, gain_ref, wg_ref, wu_ref, wd_ref, *rest, n_ff, final):
    if final:
        fgain_ref, o_ref, h_ref, acc_ref = rest
    else:
        o_ref, h_ref, acc_ref = rest
    j = pl.program_id(1)

    @pl.when(j == 0)
    def _():
        h_ref[...] = (_rms(x_ref[...]) * gain_ref[...]).astype(BF16)
        acc_ref[...] = jnp.zeros_like(acc_ref)

    h = h_ref[...]
    g = jnp.dot(h, wg_ref[...], preferred_element_type=F32)
    u = jnp.dot(h, wu_ref[...], preferred_element_type=F32)
    a = (g * _sigmoid(g) * u).astype(BF16)
    acc_ref[...] += jnp.dot(a, wd_ref[...], preferred_element_type=F32)

    @pl.when(j == n_ff - 1)
    def _():
        y = x_ref[...] + 0.5 * acc_ref[...]
        if final:
            y = _rms(y) * fgain_ref[...]
        o_ref[...] = y


def _ffn(x, gain, wg, wu, wd, final_gain=None, *, tm=512, tf=1408):
    T, D = x.shape
    FF = wg.shape[1]
    n_ff = FF // tf
    final = final_gain is not None
    in_specs = [
        pl.BlockSpec((tm, D), lambda i, j: (i, 0)),
        pl.BlockSpec((1, D), lambda i, j: (0, 0)),
        pl.BlockSpec((D, tf), lambda i, j: (0, j)),
        pl.BlockSpec((D, tf), lambda i, j: (0, j)),
        pl.BlockSpec((tf, D), lambda i, j: (j, 0)),
    ]
    args = [x, gain.reshape(1, D), wg, wu, wd]
    if final:
        in_specs.append(pl.BlockSpec((1, D), lambda i, j: (0, 0)))
        args.append(final_gain.reshape(1, D))
    return pl.pallas_call(
        functools.partial(_ffn_kernel, n_ff=n_ff, final=final),
        grid=(T // tm, n_ff),
        in_specs=in_specs,
        out_specs=pl.BlockSpec((tm, D), lambda i, j: (i, 0)),
        out_shape=jax.ShapeDtypeStruct((T, D), F32),
        scratch_shapes=[pltpu.VMEM((tm, D), BF16), pltpu.VMEM((tm, D), F32)],
        compiler_params=_params(("parallel", "arbitrary")),
        name="ffn_final" if final else "ffn",
    )(*args)


def _proj_kernel(x_ref, gain_ref, w_ref, o_ref, h_ref):
    j = pl.program_id(1)

    @pl.when(j == 0)
    def _():
        h_ref[...] = (_rms(x_ref[...]) * gain_ref[...]).astype(BF16)

    def slab(s):
        w = w_ref[:, s * SLAB:(s + 1) * SLAB].astype(BF16)
        o_ref[s] = jnp.dot(h_ref[...], w, preferred_element_type=F32).astype(o_ref.dtype)

    for s in range(SLABS_PER_STEP):
        if s == SLAB_PAD % SLABS_PER_STEP:
            is_pad = j == SLAB_PAD // SLABS_PER_STEP
            pl.when(jnp.logical_not(is_pad))(functools.partial(slab, s))

            @pl.when(is_pad)
            def _():
                o_ref[s] = jnp.zeros(o_ref.shape[1:], o_ref.dtype)
        else:
            slab(s)


def _proj(x, gain, w_in, layer, *, tm=2048):
    T, D = x.shape
    first_gate_step = SLAB_GATE // SLABS_PER_STEP
    gate_shift = (W_SLAB_GATE - SLAB_GATE) // SLABS_PER_STEP
    slabs, h = pl.pallas_call(
        _proj_kernel,
        grid=(T // tm, N_SLABS // SLABS_PER_STEP),
        in_specs=[
            pl.BlockSpec((tm, D), lambda i, j: (i, 0)),
            pl.BlockSpec((1, D), lambda i, j: (0, 0)),
            pl.BlockSpec((None, D, SLABS_PER_STEP * SLAB),
                         lambda i, j: (layer, 0, jnp.where(j < first_gate_step, j, j + gate_shift))),
        ],
        out_specs=[pl.BlockSpec((SLABS_PER_STEP, tm, SLAB), lambda i, j: (j, i, 0)),
                   pl.BlockSpec((tm, D), lambda i, j: (i, 0))],
        out_shape=[jax.ShapeDtypeStruct((N_SLABS, T, SLAB), BF16), jax.ShapeDtypeStruct((T, D), BF16)],
        compiler_params=_params(("parallel", "arbitrary")),
        name="in_proj",
    )(x, gain.reshape(1, D), w_in)
    return slabs, h


def _slab_spec(slab, rows):
    return pl.BlockSpec((None, None, rows, SLAB), lambda b, t: (slab, b, t, 0))


def _split3_bf16(x):
    hi = x.astype(BF16)
    r = x - hi.astype(F32)
    mid = r.astype(BF16)
    lo = (r - mid.astype(F32)).astype(BF16)
    return hi, mid, lo


def _hgrn_kernel(q_ref, f_ref, i_ref, g_ref, lb_ref, gn_ref, o_ref, st_ref, b_ref, k_ref, v_ref, *, n_chunks):
    C, S = HG_CHUNK, HG_SUB
    tt = n_chunks * C
    heads = [slice(h * HG_KDIM, (h + 1) * HG_KDIM) for h in range(HG_HEADS)]

    @pl.when(pl.program_id(1) == 0)
    def _():
        st_ref[...] = jnp.zeros_like(st_ref)

    lb = lb_ref[...]
    row = lax.broadcasted_iota(jnp.int32, (tt, tt), 0)
    col = lax.broadcasted_iota(jnp.int32, (tt, tt), 1)
    tri = ((col <= row) & (row // C == col // C)).astype(BF16)
    row8 = lax.broadcasted_iota(jnp.int32, (8, 1), 0)
    nt = (((1,), (1,)), ((), ()))
    tn = (((0,), (0,)), ((), ()))

    q = q_ref[...].astype(F32)
    f = f_ref[...].astype(F32)
    v = i_ref[...].astype(F32)
    v16 = v.astype(BF16)
    qf = q * _sigmoid(q) * (HG_KDIM ** -0.5)
    sf = _sigmoid(f)
    kf = (1.0 - lb) * (1.0 - sf)
    logf = jnp.log(lb + (1.0 - lb) * sf)
    b = sum(jnp.dot(tri, p, preferred_element_type=F32) for p in _split3_bf16(logf))
    for h, hs in enumerate(heads):
        b_ref[h] = b[:, hs]
        k_ref[h] = kf[:, hs]
        v_ref[h] = v[:, hs]

    state = [st_ref[h] for h in range(HG_HEADS)]
    mm = []
    for c in range(n_chunks):
        rows = slice(c * C, (c + 1) * C)
        bc, qc, kc, vc = b[rows], qf[rows], kf[rows], v16[rows]
        b_last = bc[C - 1:C, :]
        qd = (qc * jnp.exp(bc)).astype(BF16)
        kd_last = (kc * jnp.exp(b_last - bc)).astype(BF16)
        outs = []
        for h, hs in enumerate(heads):
            outs.append(lax.dot_general(qd[:, hs], state[h].astype(BF16), nt, preferred_element_type=F32))
            state[h] = jnp.exp(b_last[:, hs]) * state[h] + lax.dot_general(
                vc[:, hs], kd_last[:, hs], tn, preferred_element_type=F32)
        mm.append(outs)
    for h in range(HG_HEADS):
        st_ref[h] = state[h]

    zero = lambda n: jnp.zeros((n, HG_KDIM), F32)
    off = [jnp.zeros((tt, HG_VDIM), F32) for _ in range(HG_HEADS)]
    n = S
    while n < C:
        same_pair = (row // (2 * n)) == (col // (2 * n))
        for h, hs in enumerate(heads):
            qs, ks = [], []
            for lo in range(0, tt, 2 * n):
                mid, hi = lo + n, lo + 2 * n
                r = b_ref[h, pl.ds(mid - 1, n, stride=0), :]
                qs += [zero(n), qf[mid:hi, hs] * jnp.exp(b[mid:hi, hs] - r)]
                ks += [kf[lo:mid, hs] * jnp.exp(r - b[lo:mid, hs]), zero(n)]
            att = lax.dot_general(jnp.concatenate(qs, axis=0).astype(BF16), jnp.concatenate(ks, axis=0).astype(BF16),
                                  nt, preferred_element_type=F32)
            att = jnp.where(same_pair, att, 0.0).astype(BF16)
            off[h] = off[h] + jnp.dot(att, v16[:, hs], preferred_element_type=F32)
        n *= 2

    normed = []
    for h, hs in enumerate(heads):
        tiles = []
        for base in range(0, tt, S):
            for lo in range(0, S, 8):
                qq = qf[base + lo:base + lo + 8, hs]
                bb = b[base + lo:base + lo + 8, hs]
                acc = [jnp.zeros((8, HG_VDIM), F32) for _ in range(HG_ACC_CHAINS)]
                for j in range(lo + 8):
                    key = pl.ds(base + j, 8, stride=0)
                    p = qq * k_ref[h, key, :] * jnp.exp(jnp.minimum(bb - b_ref[h, key, :], 0.0))
                    a = jnp.sum(p, axis=-1, keepdims=True)
                    if j >= lo:
                        a = jnp.where(row8 >= j - lo, a, 0.0)
                    acc[j % HG_ACC_CHAINS] = acc[j % HG_ACC_CHAINS] + a * v_ref[h, key, :]
                tiles.append((acc[0] + acc[1]) + (acc[2] + acc[3]))
        o = jnp.concatenate(tiles, axis=0) + (jnp.concatenate([mm[c][h] for c in range(n_chunks)], axis=0) + off[h])
        normed.append(o * lax.rsqrt(jnp.mean(o * o, axis=-1, keepdims=True) + NORM_EPS))
    g = g_ref[...].astype(F32)
    o_ref[...] = (jnp.concatenate(normed, axis=-1) * gn_ref[...] * (g * _sigmoid(g))).astype(o_ref.dtype)


def _hgrn(proj, lb, gnorm, *, tt=256):
    _, B, L, _ = proj.shape
    W = HG_HEADS * HG_KDIM
    return pl.pallas_call(
        functools.partial(_hgrn_kernel, n_chunks=tt // HG_CHUNK),
        grid=(B, L // tt),
        in_specs=[
            _slab_spec(SLAB_HQ, tt), _slab_spec(SLAB_HF, tt), _slab_spec(SLAB_HI, tt), _slab_spec(SLAB_HG, tt),
            pl.BlockSpec((1, W), lambda b, t: (0, 0)),
            pl.BlockSpec((1, W), lambda b, t: (0, 0)),
        ],
        out_specs=pl.BlockSpec((None, tt, W), lambda b, t: (b, t, 0)),
        out_shape=jax.ShapeDtypeStruct((B, L, W), BF16),
        scratch_shapes=[pltpu.VMEM((HG_HEADS, HG_VDIM, HG_KDIM), F32)] + [pltpu.VMEM((HG_HEADS, tt, HG_KDIM), F32)] * 3,
        compiler_params=_params(("parallel", "arbitrary")),
        name="hgrn2",
    )(proj, proj, proj, proj, lb.reshape(1, W), gnorm.reshape(1, W))


def _cmul(ar, ai, br, bi):
    return ar * br - ai * bi, ar * bi + ai * br


def _s5_kernel(u_ref, wb_ref, wc_ref, pj_ref, pq_ref, d_ref, wglu_ref, o_ref, x_ref, rq_ref, carry_ref, *, tt):
    SEGS, SR, PITCH, NS, LC = S5_SEGS, tt // S5_SEGS, S5_PITCH, S5_HALF_STATES, S5_SCAN_LANES
    row8 = lax.broadcasted_iota(jnp.int32, (SEGS, LC), 0)
    lane_tiles = range(BRANCH_WIDTH // 128)

    @pl.when(pl.program_id(1) == 0)
    def _():
        carry_ref[...] = jnp.zeros_like(carry_ref)

    u = u_ref[...].astype(F32)
    for lt in lane_tiles:
        for s in range(SEGS):
            rq_ref[lt, pl.ds(PITCH * s, SR), :] = u[SR * s:SR * (s + 1), lt * 128:(lt + 1) * 128]
    u = jnp.concatenate(
        [jnp.concatenate([rq_ref[lt, pl.ds(j, SEGS, stride=PITCH), :] for lt in lane_tiles], axis=-1)
         for j in range(SR)], axis=0)
    u16 = u.astype(BF16)

    for hf in range(2):
        x_ref[hf] = jnp.dot(u16[:, hf * S5_HALF_CH:(hf + 1) * S5_HALF_CH], wb_ref[hf],
                            preferred_element_type=F32)

    for hf in range(2):
        for lc in range(NS // LC):
            re = slice(lc * LC, (lc + 1) * LC)
            im = slice(NS + lc * LC, NS + (lc + 1) * LC)
            a1r, a1i = pj_ref[hf, 0:SEGS, re], pj_ref[hf, 0:SEGS, im]

            def recur(j, state, hf=hf, re=re, im=im, a1r=a1r, a1i=a1i):
                r0 = pl.multiple_of(j * SEGS, SEGS)
                xr, xi = _cmul(a1r, a1i, state[0], state[1])
                xr = xr + x_ref[hf, pl.ds(r0, SEGS), re]
                xi = xi + x_ref[hf, pl.ds(r0, SEGS), im]
                x_ref[hf, pl.ds(r0, SEGS), re] = xr
                x_ref[hf, pl.ds(r0, SEGS), im] = xi
                return xr, xi

            zero = jnp.zeros((SEGS, LC), F32)
            er, ei = lax.fori_loop(0, SR, recur, (zero, zero))

            for shift in (1, 2, 4):
                cr = jnp.where(row8 >= shift, pq_ref[hf, shift:shift + 1, re], 0.0)
                ci = jnp.where(row8 >= shift, pq_ref[hf, shift:shift + 1, im], 0.0)
                dr, di = _cmul(cr, ci, pltpu.roll(er, shift, 0), pltpu.roll(ei, shift, 0))
                er, ei = er + dr, ei + di
            c0r = carry_ref[hf, SEGS - 1:SEGS, re]
            c0i = carry_ref[hf, SEGS - 1:SEGS, im]
            sr, si = _cmul(pq_ref[hf, 0:SEGS, re], pq_ref[hf, 0:SEGS, im], c0r, c0i)
            sr = sr + jnp.where(row8 >= 1, pltpu.roll(er, 1, 0), 0.0)
            si = si + jnp.where(row8 >= 1, pltpu.roll(ei, 1, 0), 0.0)
            nr, ni = _cmul(pq_ref[hf, SEGS:2 * SEGS, re], pq_ref[hf, SEGS:2 * SEGS, im], c0r, c0i)
            carry_ref[hf, :, re] = er + nr
            carry_ref[hf, :, im] = ei + ni

            def fix(j, carry, hf=hf, re=re, im=im, sr=sr, si=si):
                r0 = pl.multiple_of(j * SEGS, SEGS)
                dr, di = _cmul(pj_ref[hf, pl.ds(r0, SEGS), re], pj_ref[hf, pl.ds(r0, SEGS), im], sr, si)
                x_ref[hf, pl.ds(r0, SEGS), re] = x_ref[hf, pl.ds(r0, SEGS), re] + dr
                x_ref[hf, pl.ds(r0, SEGS), im] = x_ref[hf, pl.ds(r0, SEGS), im] + di
                return carry

            lax.fori_loop(0, SR, fix, 0)

    y = jnp.concatenate(
        [jnp.dot(x_ref[hf].astype(BF16), wc_ref[hf], preferred_element_type=F32) for hf in range(2)], axis=-1)
    y = y + d_ref[...] * u
    z = jax.nn.gelu(y).astype(BF16)
    zz = jnp.dot(z, wglu_ref[...], preferred_element_type=F32)
    out = zz[:, :BRANCH_WIDTH] * _sigmoid(zz[:, BRANCH_WIDTH:])
    for lt in lane_tiles:
        for j in range(SR):
            rq_ref[lt, pl.ds(j, SEGS, stride=PITCH), :] = out[SEGS * j:SEGS * (j + 1), lt * 128:(lt + 1) * 128]
    for lt in lane_tiles:
        for s in range(SEGS):
            o_ref[SR * s:SR * (s + 1), lt * 128:(lt + 1) * 128] = rq_ref[lt, pl.ds(PITCH * s, SR), :].astype(o_ref.dtype)


def _s5(proj, wb, wc, pj, pq, d_skip, w_glu, *, tt=S5_TILE):
    _, B, L, _ = proj.shape
    W = BRANCH_WIDTH
    NS2 = 2 * S5_HALF_STATES
    return pl.pallas_call(
        functools.partial(_s5_kernel, tt=tt),
        grid=(B, L // tt),
        in_specs=[
            _slab_spec(SLAB_S5, tt),
            pl.BlockSpec((2, S5_HALF_CH, NS2), lambda b, t: (0, 0, 0)),
            pl.BlockSpec((2, NS2, S5_HALF_CH), lambda b, t: (0, 0, 0)),
            pl.BlockSpec((2, tt, NS2), lambda b, t: (0, 0, 0)),
            pl.BlockSpec((2, 2 * S5_SEGS, NS2), lambda b, t: (0, 0, 0)),
            pl.BlockSpec((1, W), lambda b, t: (0, 0)),
            pl.BlockSpec((W, 2 * W), lambda b, t: (0, 0)),
        ],
        out_specs=pl.BlockSpec((None, tt, W), lambda b, t: (b, t, 0)),
        out_shape=jax.ShapeDtypeStruct((B, L, W), BF16),
        scratch_shapes=[pltpu.VMEM((2, tt, NS2), F32),
                        pltpu.VMEM((W // 128, S5_SEGS * S5_PITCH, 128), F32),
                        pltpu.VMEM((2, S5_SEGS, NS2), F32)],
        compiler_params=_params(("parallel", "arbitrary")),
        name="s5",
    )(proj, wb, wc, pj, pq, d_skip.reshape(1, W), w_glu)


def _s5_tables(a_re, a_im, log_dt, b_re, b_im, c_re, c_im):
    dt = jnp.exp(log_dt)[:, None]
    mag = jnp.exp(a_re * dt)
    ab_re = mag * jnp.cos(a_im * dt)
    ab_im = mag * jnp.sin(a_im * dt)
    den = a_re * a_re + a_im * a_im
    zr = ((ab_re - 1.0) * a_re + ab_im * a_im) / den
    zi = (ab_im * a_re - (ab_re - 1.0) * a_im) / den
    bb_re = zr[..., None] * b_re - zi[..., None] * b_im
    bb_im = zr[..., None] * b_im + zi[..., None] * b_re
    GH = S5_GROUPS // 2
    eye = jnp.eye(GH, dtype=F32)

    def bdiag_in(m):
        m = m.reshape(2, GH, S5_STATE, S5_GROUP)
        return jnp.einsum('hgpc,gk->hgckp', m, eye).reshape(2, GH * S5_GROUP, GH * S5_STATE)

    def bdiag_out(m):
        m = m.reshape(2, GH, S5_GROUP, S5_STATE)
        return jnp.einsum('hgcp,gk->hgpkc', m, eye).reshape(2, GH * S5_STATE, GH * S5_GROUP)

    wb = jnp.concatenate([bdiag_in(bb_re), bdiag_in(bb_im)], axis=-1).astype(BF16)
    wc = jnp.concatenate([bdiag_out(c_re), -bdiag_out(c_im)], axis=1).astype(BF16)
    sr = S5_TILE // S5_SEGS

    def powers(n):
        n = jnp.asarray(n, F32)[:, None, None]
        mag = jnp.exp(n * (a_re * dt)[None])
        ang = n * (a_im * dt)[None]
        split = lambda p: p.reshape(p.shape[0], 2, S5_HALF_STATES).transpose(1, 0, 2)
        return jnp.concatenate([split(mag * jnp.cos(ang)), split(mag * jnp.sin(ang))], axis=-1)

    pj = powers(np.repeat(np.arange(1, sr + 1), S5_SEGS))
    pq = powers(np.concatenate([sr * np.arange(S5_SEGS), sr * np.arange(1, S5_SEGS + 1)]))
    return wb, wc, pj, pq


def _conv_kernel(a_ref, b_ref, w_ref, cb_ref, lg_ref, lbias_ref, o_ref, z_ref, y_ref, *, tt, rb):
    H = CONV_HALO

    @pl.when(pl.program_id(1) == 0)
    def _():
        z_ref[0:H, :] = jnp.zeros((H, BRANCH_WIDTH), F32)

    z_ref[H:H + tt, :] = a_ref[...].astype(F32) * _sigmoid(b_ref[...].astype(F32))
    cb = cb_ref[...]
    lg = lg_ref[...]
    lbias = lbias_ref[...]

    def block(k, carry):
        r0 = pl.multiple_of(k * rb, rb)
        for lc in range(0, BRANCH_WIDTH, 128):
            win = z_ref[pl.ds(r0, rb + H), lc:lc + 128]
            acc = [jnp.zeros((rb, 128), F32), jnp.zeros((rb, 128), F32)]
            for c in range(8):
                shifted = win if c == 0 else pltpu.roll(win, rb + H - c, 0)
                for off in range(c, H + 1, 8):
                    w = off - (H - CONV_WIDTH + 1)
                    if 0 <= w < CONV_WIDTH:
                        acc[w % 2] = acc[w % 2] + shifted[off - c:off - c + rb, :] * w_ref[w:w + 1, lc:lc + 128]
            y_ref[pl.ds(r0, rb), lc:lc + 128] = acc[0] + acc[1]
        return carry

    lax.fori_loop(0, tt // rb, block, 0)
    z_ref[0:H, :] = z_ref[tt:tt + H, :]

    acc = y_ref[...] + cb
    xc = acc - jnp.mean(acc, axis=-1, keepdims=True)
    var = jnp.mean(xc * xc, axis=-1, keepdims=True)
    y = xc * lax.rsqrt(var + NORM_EPS) * lg + lbias
    o_ref[...] = (y * _sigmoid(y)).astype(o_ref.dtype)


def _conv(proj, conv_w, conv_b, ln_g, ln_b, *, tt=512, rb=32):
    _, B, L, _ = proj.shape
    W = BRANCH_WIDTH
    w_pad = jnp.zeros((CONV_HALO, W), F32).at[:CONV_WIDTH].set(conv_w)
    vec = lambda: pl.BlockSpec((1, W), lambda b, t: (0, 0))
    return pl.pallas_call(
        functools.partial(_conv_kernel, tt=tt, rb=rb),
        grid=(B, L // tt),
        in_specs=[_slab_spec(SLAB_CONV_A, tt), _slab_spec(SLAB_CONV_B, tt),
                  pl.BlockSpec((CONV_HALO, W), lambda b, t: (0, 0)), vec(), vec(), vec()],
        out_specs=pl.BlockSpec((None, tt, W), lambda b, t: (b, t, 0)),
        out_shape=jax.ShapeDtypeStruct((B, L, W), BF16),
        scratch_shapes=[pltpu.VMEM((tt + CONV_HALO, W), F32), pltpu.VMEM((tt, W), F32)],
        compiler_params=_params(("parallel", "arbitrary")),
        name="conformer_conv",
    )(proj, proj, w_pad, conv_b.reshape(1, W), ln_g.reshape(1, W), ln_b.reshape(1, W))


def _rope_table_kernel(pos_ref, inv_ref, cos_ref, sin_ref):
    ang = pos_ref[...] * inv_ref[...]
    lane = lax.broadcasted_iota(jnp.int32, ang.shape, 1)
    cos_ref[...] = jnp.cos(ang)
    sin_ref[...] = jnp.where(lane < ROPE_HALF, -jnp.sin(ang), jnp.sin(ang))


def _rope_tables(positions, *, tm=1024):
    T = positions.size
    inv = ROPE_THETA ** (-jnp.arange(ROPE_HALF, dtype=F32) / ROPE_HALF)
    inv = jnp.concatenate([inv, inv, jnp.zeros((ATT_HEAD_DIM - ROPE_DIM,), F32)]).reshape(1, ATT_HEAD_DIM)
    pos = positions.astype(F32).reshape(T, 1)
    return pl.pallas_call(
        _rope_table_kernel,
        grid=(T // tm,),
        in_specs=[pl.BlockSpec((tm, 1), lambda i: (i, 0)), pl.BlockSpec((1, ATT_HEAD_DIM), lambda i: (0, 0))],
        out_specs=[pl.BlockSpec((tm, ATT_HEAD_DIM), lambda i: (i, 0))] * 2,
        out_shape=[jax.ShapeDtypeStruct((T, ATT_HEAD_DIM), F32)] * 2,
        compiler_params=_params(("parallel",)),
        name="rope_tables",
    )(pos, inv)


def _rope(x, cos, sin):
    lane = lax.broadcasted_iota(jnp.int32, x.shape, 1)
    partner = jnp.where(lane < ROPE_HALF, pltpu.roll(x, ATT_HEAD_DIM - ROPE_HALF, 1), pltpu.roll(x, ROPE_HALF, 1))
    return x * cos + partner * sin


DEINT_STRIDE = 4


def _qkv_kernel(h_ref, wq_ref, wk_ref, wv_ref, cos_ref, sin_ref, o_ref, w16_ref, s_ref, t_ref, *, d):
    tm = h_ref.shape[0]

    @pl.when(pl.program_id(0) == 0)
    def _():
        for j, w_ref in enumerate((wq_ref, wk_ref, wv_ref)):
            w16_ref[j] = w_ref[...].astype(BF16)

    h16 = h_ref[...]
    cos, sin = cos_ref[...], sin_ref[...]
    for j in range(3):
        res = jnp.dot(h16, w16_ref[j], preferred_element_type=F32)
        for h in range(ATT_HEADS_PER_GROUP):
            hs = slice(h * ATT_HEAD_DIM, (h + 1) * ATT_HEAD_DIM)
            s_ref[j, h] = _rope(res[:, hs], cos, sin) if j < 2 else res[:, hs]
            if d <= DEINT_STRIDE:
                for r in range(d):
                    o_ref[j, r, :, hs] = s_ref[j, h, pl.ds(r, tm // d, stride=d), :].astype(o_ref.dtype)
            else:
                q = tm // DEINT_STRIDE
                for r0 in range(DEINT_STRIDE):
                    t_ref[j, h, pl.ds(r0 * q, q), :] = s_ref[j, h, pl.ds(r0, q, stride=DEINT_STRIDE), :]
                for r0 in range(DEINT_STRIDE):
                    for r1 in range(d // DEINT_STRIDE):
                        o_ref[j, DEINT_STRIDE * r1 + r0, :, hs] = t_ref[
                            j, h, pl.ds(r0 * q + r1, tm // d, stride=d // DEINT_STRIDE), :].astype(o_ref.dtype)


def _qkv(h, w_in, layer, cos_t, sin_t, gi, d, B, *, tm=1024):
    T, D = h.shape
    L = T // B
    tiles = L // tm
    W = ATT_HEADS_PER_GROUP * ATT_HEAD_DIM
    assert d <= DEINT_STRIDE or d == DEINT_STRIDE * DEINT_STRIDE
    stage = (3, ATT_HEADS_PER_GROUP, tm, ATT_HEAD_DIM)
    w_spec = lambda j: pl.BlockSpec((None, D, SLAB), lambda i: (layer, 0, W_SLAB_Q + gi + 3 * j))
    return pl.pallas_call(
        functools.partial(_qkv_kernel, d=d),
        grid=(T // tm,),
        in_specs=[
            pl.BlockSpec((tm, D), lambda i: (i, 0)),
            w_spec(0), w_spec(1), w_spec(2),
            pl.BlockSpec((tm, ATT_HEAD_DIM), lambda i: (i, 0)),
            pl.BlockSpec((tm, ATT_HEAD_DIM), lambda i: (i, 0)),
        ],
        out_specs=pl.BlockSpec((3, None, d, tm // d, W), lambda i: (0, i // tiles, 0, i % tiles, 0)),
        out_shape=jax.ShapeDtypeStruct((3, B, d, L // d, W), BF16),
        scratch_shapes=[pltpu.VMEM((3, D, SLAB), BF16),
                        pltpu.VMEM(stage, F32),
                        pltpu.VMEM(stage if d > DEINT_STRIDE else (1, 1, 8, ATT_HEAD_DIM), F32)],
        compiler_params=_params(("arbitrary",)),
        name=f"qkv_proj_g{gi}",
    )(h, w_in, w_in, w_in, cos_t, sin_t)


ATT_UNROLL = 8


def _att_kernel(q_ref, kc_ref, kp_ref, vc_ref, vp_ref, o_ref, lse_ref, *, d, nq):
    Q = ATT_QBLOCK
    n = pl.program_id(1)
    qi = lax.broadcasted_iota(jnp.int32, (Q, 2 * Q), 0)
    kj = lax.broadcasted_iota(jnp.int32, (Q, 2 * Q), 1)
    rel = Q + qi - kj
    band = (rel >= 0) & (rel <= Q)
    band_first = band & ((n > 0) | (kj >= Q))
    head_of_lane = lax.broadcasted_iota(jnp.int32, (Q, ATT_HEAD_DIM), 1) // (ATT_HEAD_DIM // ATT_HEADS_PER_GROUP)
    nt = (((1,), (1,)), ((), ()))

    def block(r, i):
        rows = slice(i * Q, (i + 1) * Q)
        outs = []
        lse = jnp.zeros((Q, ATT_HEAD_DIM), F32)
        for h in range(ATT_HEADS_PER_GROUP):
            hs = slice(h * ATT_HEAD_DIM, (h + 1) * ATT_HEAD_DIM)
            if i == 0:
                k = jnp.concatenate([kp_ref[r, :, hs], kc_ref[r, rows, hs]], axis=0)
                v = jnp.concatenate([vp_ref[r, :, hs], vc_ref[r, rows, hs]], axis=0)
            else:
                k = kc_ref[r, (i - 1) * Q:(i + 1) * Q, hs]
                v = vc_ref[r, (i - 1) * Q:(i + 1) * Q, hs]
            s = lax.dot_general(q_ref[r, rows, hs], k, nt, preferred_element_type=F32) * (ATT_HEAD_DIM ** -0.5)
            s = jnp.where(band_first if i == 0 else band, s, MASK_VALUE)
            m = jnp.max(s, axis=-1, keepdims=True)
            p = jnp.exp(s - m)
            l = jnp.sum(p, axis=-1, keepdims=True)
            outs.append(jnp.dot(p.astype(BF16), v, preferred_element_type=F32) / l)
            lse = jnp.where(head_of_lane == h, m + jnp.log(l), lse)
        o_ref[r, rows, :] = jnp.concatenate(outs, axis=-1).astype(o_ref.dtype)
        lse_ref[r, rows, :] = lse

    if d <= ATT_UNROLL:
        for r in range(d):
            for i in range(nq):
                block(r, i)
    else:
        def residues(it, carry):
            for u in range(ATT_UNROLL):
                for i in range(nq):
                    block(it * ATT_UNROLL + u, i)
            return carry

        lax.fori_loop(0, d // ATT_UNROLL, residues, 0)


def _attention_group(qkv, gi):
    _, B, d, n_sub, W = qkv.shape
    Q = ATT_QBLOCK
    nq = max(1, ATT_UNROLL // d)

    def cur(s):
        return pl.BlockSpec((None, None, d, nq * Q, W), lambda b, n: (s, b, 0, n, 0))

    def prev(s):
        return pl.BlockSpec((None, None, d, Q, W), lambda b, n: (s, b, 0, jnp.maximum(nq * n - 1, 0), 0))

    return pl.pallas_call(
        functools.partial(_att_kernel, d=d, nq=nq),
        grid=(B, n_sub // (nq * Q)),
        in_specs=[cur(0), cur(1), prev(1), cur(2), prev(2)],
        out_specs=[pl.BlockSpec((None, d, nq * Q, W), lambda b, n: (b, 0, n, 0)),
                   pl.BlockSpec((None, d, nq * Q, ATT_HEAD_DIM), lambda b, n: (b, 0, n, 0))],
        out_shape=[jax.ShapeDtypeStruct((B, d, n_sub, W), BF16),
                   jax.ShapeDtypeStruct((B, d, n_sub, ATT_HEAD_DIM), F32)],
        compiler_params=_params(("parallel", "arbitrary")),
        name=f"dilated_attention_g{gi}",
    )(qkv, qkv, qkv, qkv, qkv)


def _merge_kernel(x_ref, ya_ref, yb_ref, yc_ref, o1_ref, o2_ref, o3_ref, l1_ref, l2_ref, l3_ref,
                  g0_ref, g1_ref, g2_ref, g3_ref, g4_ref, g5_ref, g6_ref, g7_ref, wb_ref, wo_ref, out_ref,
                  os_ref, ls_ref):
    for g, (o_ref, l_ref) in enumerate(((o1_ref, l1_ref), (o2_ref, l2_ref), (o3_ref, l3_ref))):
        d, rows = o_ref.shape[0], o_ref.shape[1]
        for r in range(d):
            o = o_ref[r].astype(F32)
            for h in range(ATT_HEADS_PER_GROUP):
                os_ref[g, h, pl.ds(r, rows, stride=d), :] = o[:, h * ATT_HEAD_DIM:(h + 1) * ATT_HEAD_DIM]
            ls_ref[g, pl.ds(r, rows, stride=d), :] = l_ref[r]
    lanes = ATT_HEAD_DIM // ATT_HEADS_PER_GROUP
    parts = []
    for h in range(ATT_HEADS_PER_GROUP):
        l1, l2, l3 = (ls_ref[g, :, h * lanes:h * lanes + 1] for g in range(3))
        m = jnp.maximum(jnp.maximum(l1, l2), l3)
        e1, e2, e3 = jnp.exp(l1 - m), jnp.exp(l2 - m), jnp.exp(l3 - m)
        parts.append((e1 * os_ref[0, h] + e2 * os_ref[1, h] + e3 * os_ref[2, h]) / (e1 + e2 + e3))
    yd = jnp.concatenate(parts, axis=-1)
    ys = (ya_ref[...], yb_ref[...], yc_ref[...], yd.astype(BF16))
    gates = ((g0_ref, g1_ref), (g2_ref, g3_ref), (g4_ref, g5_ref), (g6_ref, g7_ref))
    merged = None
    for k in range(N_BRANCHES):
        gate = _sigmoid(jnp.concatenate([gates[k][0][...], gates[k][1][...]], axis=-1).astype(F32))
        term = gate * jnp.dot(ys[k], wb_ref[k], preferred_element_type=F32)
        merged = term if merged is None else merged + term
    out_ref[...] = x_ref[...] + jnp.dot(merged.astype(BF16), wo_ref[...], preferred_element_type=F32)


def _merge(x, ya, yb, yc, att, proj, w_branch, w_out, *, tm=512):
    B, L, D = x.shape
    W = BRANCH_WIDTH
    (o1, l1), (o2, l2), (o3, l3) = att
    row = lambda: pl.BlockSpec((None, tm, W), lambda b, t: (b, t, 0))

    def res(a):
        d, last = a.shape[1], a.shape[3]
        return pl.BlockSpec((None, d, tm // d, last), lambda b, t: (b, 0, t, 0))

    return pl.pallas_call(
        _merge_kernel,
        grid=(B, L // tm),
        in_specs=[pl.BlockSpec((None, tm, D), lambda b, t: (b, t, 0))] + [row() for _ in range(3)]
                 + [res(a) for a in (o1, o2, o3, l1, l2, l3)]
                 + [_slab_spec(SLAB_GATE + s, tm) for s in range(8)]
                 + [pl.BlockSpec((N_BRANCHES, W, D), lambda b, t: (0, 0, 0)), pl.BlockSpec((D, D), lambda b, t: (0, 0))],
        out_specs=pl.BlockSpec((None, tm, D), lambda b, t: (b, t, 0)),
        out_shape=jax.ShapeDtypeStruct((B, L, D), F32),
        scratch_shapes=[pltpu.VMEM((3, ATT_HEADS_PER_GROUP, tm, ATT_HEAD_DIM), F32),
                        pltpu.VMEM((3, tm, ATT_HEAD_DIM), F32)],
        compiler_params=_params(("parallel", "parallel")),
        name="gated_merge",
    )(x, ya, yb, yc, o1, o2, o3, l1, l2, l3, *([proj] * 8), w_branch, w_out)


def kernel(x, positions, ffn1_norm, ffn1_w_gate, ffn1_w_up, ffn1_w_down, mix_norm, w_in, hg_lb_logits, hg_gnorm, s5_a_re, s5_a_im, s5_log_dt, s5_b_re, s5_b_im, s5_c_re, s5_c_im, s5_d, s5_w_glu, conv_w, conv_b, conv_ln_g, conv_ln_b, w_branch, w_out, ffn2_norm, ffn2_w_gate, ffn2_w_up, ffn2_w_down, final_norm):
    B, L, D = x.shape
    T = B * L
    depth = w_in.shape[0]
    lb_soft = jax.nn.softmax(hg_lb_logits.astype(F32), axis=0)
    lb_all = jnp.cumsum(lb_soft, axis=0) - lb_soft[0]
    cos_t, sin_t = _rope_tables(positions)
    bf = lambda w: w.astype(BF16)

    xt = x.reshape(T, D)
    for l in range(depth):
        xt = _ffn(xt, ffn1_norm[l], bf(ffn1_w_gate[l]), bf(ffn1_w_up[l]), bf(ffn1_w_down[l]))
        proj, h_mix = _proj(xt, mix_norm[l], w_in, l)
        proj = proj.reshape(N_SLABS, B, L, SLAB)
        ya = _hgrn(proj, lb_all[l], hg_gnorm[l])
        wb, wc, pj, pq = _s5_tables(s5_a_re[l], s5_a_im[l], s5_log_dt[l], s5_b_re[l], s5_b_im[l], s5_c_re[l], s5_c_im[l])
        yb = _s5(proj, wb, wc, pj, pq, s5_d[l], bf(s5_w_glu[l]))
        yc = _conv(proj, conv_w[l], conv_b[l], conv_ln_g[l], conv_ln_b[l])
        att = [_attention_group(_qkv(h_mix, w_in, l, cos_t, sin_t, gi, dil, B), gi)
               for gi, (_, dil) in enumerate(ATT_CONFIGS)]
        xt = _merge(xt.reshape(B, L, D), ya, yb, yc, att, proj, bf(w_branch[l]), bf(w_out[l])).reshape(T, D)
        last = l == depth - 1
        xt = _ffn(xt, ffn2_norm[l], bf(ffn2_w_gate[l]), bf(ffn2_w_up[l]), bf(ffn2_w_down[l]),
                  final_norm if last else None)
    return xt.reshape(B, L, D)
```

```python
import functools
import math

import numpy as np
import jax
import jax.numpy as jnp
from jax import lax
from jax.experimental import pallas as pl
from jax.experimental.pallas import tpu as pltpu

F32 = jnp.float32
BF16 = jnp.bfloat16

NORM_EPS = 1e-6
MASK_VALUE = -1e30
D_MODEL = 1024
D_FF = 2816
N_BRANCHES = 4
BRANCH_WIDTH = 512

HG_HEADS = 4
HG_KDIM = 128
HG_VDIM = 128
HG_CHUNK = 64
HG_SUB = 16
HG_ACC_CHAINS = 4

S5_GROUP = 16
S5_GROUPS = 32
S5_STATE = 64
S5_HALF_CH = 256
S5_HALF_STATES = (S5_GROUPS // 2) * S5_STATE
S5_TILE = 256
S5_SEGS = 8
S5_PITCH = S5_TILE // S5_SEGS + 8
S5_SCAN_LANES = 512

CONV_WIDTH = 31
CONV_HALO = 32

ATT_HEAD_DIM = 128
ATT_CONFIGS = ((128, 1), (512, 4), (2048, 16))
ATT_HEADS_PER_GROUP = 4
ATT_HEADS = 12
ATT_QBLOCK = 128
ROPE_THETA = 500000.0
ROPE_DIM = 32
ROPE_HALF = 16

SLAB = 512
SLAB_HQ, SLAB_HF, SLAB_HI, SLAB_HG, SLAB_S5, SLAB_CONV_A, SLAB_CONV_B = 0, 1, 2, 3, 4, 5, 6
W_SLAB_Q = 7
W_SLAB_GATE = 16
SLABS_PER_STEP = 2
SLAB_PAD = 7
SLAB_GATE = 8
N_SLABS = 16

VMEM_LIMIT = 56 * 1024 * 1024


def _params(sem):
    return pltpu.CompilerParams(dimension_semantics=sem, vmem_limit_bytes=VMEM_LIMIT)


def _rms(x):
    return x * lax.rsqrt(jnp.mean(x * x, axis=-1, keepdims=True) + NORM_EPS)


def _sigmoid(x):
    return jax.nn.sigmoid(x)


def _ffn_kernel(x_ref, gain_ref, wg_ref, wu_ref, wd_ref, *rest, n_ff, final):
    if final:
        fgain_ref, o_ref, h_ref, acc_ref = rest
    else:
        o_ref, h_ref, acc_ref = rest
    j = pl.program_id(1)

    @pl.when(j == 0)
    def _():
        h_ref[...] = (_rms(x_ref[...]) * gain_ref[...]).astype(BF16)
        acc_ref[...] = jnp.zeros_like(acc_ref)

    h = h_ref[...]
    g = jnp.dot(h, wg_ref[...], preferred_element_type=F32)
    u = jnp.dot(h, wu_ref[...], preferred_element_type=F32)
    a = (g * _sigmoid(g) * u).astype(BF16)
    acc_ref[...] += jnp.dot(a, wd_ref[...], preferred_element_type=F32)

    @pl.when(j == n_ff - 1)
    def _():
        y = x_ref[...] + 0.5 * acc_ref[...]
        if final:
            y = _rms(y) * fgain_ref[...]
        o_ref[...] = y


def _ffn(x, gain, wg, wu, wd, final_gain=None, *, tm=256, tf=2816):
    T, D = x.shape
    FF = wg.shape[1]
    n_ff = FF // tf
    final = final_gain is not None
    in_specs = [
        pl.BlockSpec((tm, D), lambda i, j: (i, 0)),
        pl.BlockSpec((1, D), lambda i, j: (0, 0)),
        pl.BlockSpec((D, tf), lambda i, j: (0, j)),
        pl.BlockSpec((D, tf), lambda i, j: (0, j)),
        pl.BlockSpec((tf, D), lambda i, j: (j, 0)),
    ]
    args = [x, gain.reshape(1, D), wg, wu, wd]
    if final:
        in_specs.append(pl.BlockSpec((1, D), lambda i, j: (0, 0)))
        args.append(final_gain.reshape(1, D))
    return pl.pallas_call(
        functools.partial(_ffn_kernel, n_ff=n_ff, final=final),
        grid=(T // tm, n_ff),
        in_specs=in_specs,
        out_specs=pl.BlockSpec((tm, D), lambda i, j: (i, 0)),
        out_shape=jax.ShapeDtypeStruct((T, D), F32),
        scratch_shapes=[pltpu.VMEM((tm, D), BF16), pltpu.VMEM((tm, D), F32)],
        compiler_params=_params(("parallel", "arbitrary")),
        name="ffn_final" if final else "ffn",
    )(*args)


def _proj_kernel(x_ref, gain_ref, w_ref, o_ref, h_ref):
    j = pl.program_id(1)

    @pl.when(j == 0)
    def _():
        h_ref[...] = (_rms(x_ref[...]) * gain_ref[...]).astype(BF16)

    def slab(s):
        w = w_ref[:, s * SLAB:(s + 1) * SLAB].astype(BF16)
        o_ref[s] = jnp.dot(h_ref[...], w, preferred_element_type=F32).astype(o_ref.dtype)

    for s in range(SLABS_PER_STEP):
        if s == SLAB_PAD % SLABS_PER_STEP:
            is_pad = j == SLAB_PAD // SLABS_PER_STEP
            pl.when(jnp.logical_not(is_pad))(functools.partial(slab, s))

            @pl.when(is_pad)
            def _():
                o_ref[s] = jnp.zeros(o_ref.shape[1:], o_ref.dtype)
        else:
            slab(s)


def _proj(x, gain, w_in, layer, *, tm=2048):
    T, D = x.shape
    first_gate_step = SLAB_GATE // SLABS_PER_STEP
    gate_shift = (W_SLAB_GATE - SLAB_GATE) // SLABS_PER_STEP
    slabs, h = pl.pallas_call(
        _proj_kernel,
        grid=(T // tm, N_SLABS // SLABS_PER_STEP),
        in_specs=[
            pl.BlockSpec((tm, D), lambda i, j: (i, 0)),
            pl.BlockSpec((1, D), lambda i, j: (0, 0)),
            pl.BlockSpec((None, D, SLABS_PER_STEP * SLAB),
                         lambda i, j: (layer, 0, jnp.where(j < first_gate_step, j, j + gate_shift))),
        ],
        out_specs=[pl.BlockSpec((SLABS_PER_STEP, tm, SLAB), lambda i, j: (j, i, 0)),
                   pl.BlockSpec((tm, D), lambda i, j: (i, 0))],
        out_shape=[jax.ShapeDtypeStruct((N_SLABS, T, SLAB), BF16), jax.ShapeDtypeStruct((T, D), BF16)],
        compiler_params=_params(("parallel", "arbitrary")),
        name="in_proj",
    )(x, gain.reshape(1, D), w_in)
    return slabs, h


def _slab_spec(slab, rows):
    return pl.BlockSpec((None, None, rows, SLAB), lambda b, t: (slab, b, t, 0))


def _split3_bf16(x):
    hi = x.astype(BF16)
    r = x - hi.astype(F32)
    mid = r.astype(BF16)
    lo = (r - mid.astype(F32)).astype(BF16)
    return hi, mid, lo


def _hgrn_kernel(q_ref, f_ref, i_ref, g_ref, lb_ref, gn_ref, o_ref, st_ref, b_ref, k_ref, v_ref, *, n_chunks):
    C, S = HG_CHUNK, HG_SUB
    tt = n_chunks * C
    heads = [slice(h * HG_KDIM, (h + 1) * HG_KDIM) for h in range(HG_HEADS)]

    @pl.when(pl.program_id(1) == 0)
    def _():
        st_ref[...] = jnp.zeros_like(st_ref)

    lb = lb_ref[...]
    row = lax.broadcasted_iota(jnp.int32, (tt, tt), 0)
    col = lax.broadcasted_iota(jnp.int32, (tt, tt), 1)
    tri = ((col <= row) & (row // C == col // C)).astype(BF16)
    row8 = lax.broadcasted_iota(jnp.int32, (8, 1), 0)
    nt = (((1,), (1,)), ((), ()))
    tn = (((0,), (0,)), ((), ()))

    q = q_ref[...].astype(F32)
    f = f_ref[...].astype(F32)
    v = i_ref[...].astype(F32)
    v16 = v.astype(BF16)
    qf = q * _sigmoid(q) * (HG_KDIM ** -0.5)
    sf = _sigmoid(f)
    kf = (1.0 - lb) * (1.0 - sf)
    logf = jnp.log(lb + (1.0 - lb) * sf)
    b = sum(jnp.dot(tri, p, preferred_element_type=F32) for p in _split3_bf16(logf))
    for h, hs in enumerate(heads):
        b_ref[h] = b[:, hs]
        k_ref[h] = kf[:, hs]
        v_ref[h] = v[:, hs]

    state = [st_ref[h] for h in range(HG_HEADS)]
    mm = []
    for c in range(n_chunks):
        rows = slice(c * C, (c + 1) * C)
        bc, qc, kc, vc = b[rows], qf[rows], kf[rows], v16[rows]
        b_last = bc[C - 1:C, :]
        qd = (qc * jnp.exp(bc)).astype(BF16)
        kd_last = (kc * jnp.exp(b_last - bc)).astype(BF16)
        outs = []
        for h, hs in enumerate(heads):
            outs.append(lax.dot_general(qd[:, hs], state[h].astype(BF16), nt, preferred_element_type=F32))
            state[h] = jnp.exp(b_last[:, hs]) * state[h] + lax.dot_general(
                vc[:, hs], kd_last[:, hs], tn, preferred_element_type=F32)
        mm.append(outs)
    for h in range(HG_HEADS):
        st_ref[h] = state[h]

    zero = lambda n: jnp.zeros((n, HG_KDIM), F32)
    off = [jnp.zeros((tt, HG_VDIM), F32) for _ in range(HG_HEADS)]
    n = S
    while n < C:
        same_pair = (row // (2 * n)) == (col // (2 * n))
        for h, hs in enumerate(heads):
            qs, ks = [], []
            for lo in range(0, tt, 2 * n):
                mid, hi = lo + n, lo + 2 * n
                r = b_ref[h, pl.ds(mid - 1, n, stride=0), :]
                qs += [zero(n), qf[mid:hi, hs] * jnp.exp(b[mid:hi, hs] - r)]
                ks += [kf[lo:mid, hs] * jnp.exp(r - b[lo:mid, hs]), zero(n)]
            att = lax.dot_general(jnp.concatenate(qs, axis=0).astype(BF16), jnp.concatenate(ks, axis=0).astype(BF16),
                                  nt, preferred_element_type=F32)
            att = jnp.where(same_pair, att, 0.0).astype(BF16)
            off[h] = off[h] + jnp.dot(att, v16[:, hs], preferred_element_type=F32)
        n *= 2

    normed = []
    for h, hs in enumerate(heads):
        tiles = []
        for base in range(0, tt, S):
            for lo in range(0, S, 8):
                qq = qf[base + lo:base + lo + 8, hs]
                bb = b[base + lo:base + lo + 8, hs]
                acc = [jnp.zeros((8, HG_VDIM), F32) for _ in range(HG_ACC_CHAINS)]
                for j in range(lo + 8):
                    key = pl.ds(base + j, 8, stride=0)
                    p = qq * k_ref[h, key, :] * jnp.exp(jnp.minimum(bb - b_ref[h, key, :], 0.0))
                    a = jnp.sum(p, axis=-1, keepdims=True)
                    if j >= lo:
                        a = jnp.where(row8 >= j - lo, a, 0.0)
                    acc[j % HG_ACC_CHAINS] = acc[j % HG_ACC_CHAINS] + a * v_ref[h, key, :]
                tiles.append((acc[0] + acc[1]) + (acc[2] + acc[3]))
        o = jnp.concatenate(tiles, axis=0) + (jnp.concatenate([mm[c][h] for c in range(n_chunks)], axis=0) + off[h])
        normed.append(o * lax.rsqrt(jnp.mean(o * o, axis=-1, keepdims=True) + NORM_EPS))
    g = g_ref[...].astype(F32)
    o_ref[...] = (jnp.concatenate(normed, axis=-1) * gn_ref[...] * (g * _sigmoid(g))).astype(o_ref.dtype)


def _hgrn(proj, lb, gnorm, *, tt=256):
    _, B, L, _ = proj.shape
    W = HG_HEADS * HG_KDIM
    return pl.pallas_call(
        functools.partial(_hgrn_kernel, n_chunks=tt // HG_CHUNK),
        grid=(B, L // tt),
        in_specs=[
            _slab_spec(SLAB_HQ, tt), _slab_spec(SLAB_HF, tt), _slab_spec(SLAB_HI, tt), _slab_spec(SLAB_HG, tt),
            pl.BlockSpec((1, W), lambda b, t: (0, 0)),
            pl.BlockSpec((1, W), lambda b, t: (0, 0)),
        ],
        out_specs=pl.BlockSpec((None, tt, W), lambda b, t: (b, t, 0)),
        out_shape=jax.ShapeDtypeStruct((B, L, W), BF16),
        scratch_shapes=[pltpu.VMEM((HG_HEADS, HG_VDIM, HG_KDIM), F32)] + [pltpu.VMEM((HG_HEADS, tt, HG_KDIM), F32)] * 3,
        compiler_params=_params(("parallel", "arbitrary")),
        name="hgrn2",
    )(proj, proj, proj, proj, lb.reshape(1, W), gnorm.reshape(1, W))


def _cmul(ar, ai, br, bi):
    return ar * br - ai * bi, ar * bi + ai * br


def _s5_kernel(u_ref, wb_ref, wc_ref, pj_ref, pq_ref, d_ref, wglu_ref, o_ref, x_ref, rq_ref, carry_ref, *, tt):
    SEGS, SR, PITCH, NS, LC = S5_SEGS, tt // S5_SEGS, S5_PITCH, S5_HALF_STATES, S5_SCAN_LANES
    row8 = lax.broadcasted_iota(jnp.int32, (SEGS, LC), 0)
    lane_tiles = range(BRANCH_WIDTH // 128)

    @pl.when(pl.program_id(1) == 0)
    def _():
        carry_ref[...] = jnp.zeros_like(carry_ref)

    u = u_ref[...].astype(F32)
    for lt in lane_tiles:
        for s in range(SEGS):
            rq_ref[lt, pl.ds(PITCH * s, SR), :] = u[SR * s:SR * (s + 1), lt * 128:(lt + 1) * 128]
    u = jnp.concatenate(
        [jnp.concatenate([rq_ref[lt, pl.ds(j, SEGS, stride=PITCH), :] for lt in lane_tiles], axis=-1)
         for j in range(SR)], axis=0)
    u16 = u.astype(BF16)

    for hf in range(2):
        x_ref[hf] = jnp.dot(u16[:, hf * S5_HALF_CH:(hf + 1) * S5_HALF_CH], wb_ref[hf],
                            preferred_element_type=F32)

    for hf in range(2):
        for lc in range(NS // LC):
            re = slice(lc * LC, (lc + 1) * LC)
            im = slice(NS + lc * LC, NS + (lc + 1) * LC)
            a1r, a1i = pj_ref[hf, 0:SEGS, re], pj_ref[hf, 0:SEGS, im]

            local = [(x_ref[hf, 0:SEGS, re], x_ref[hf, 0:SEGS, im])]
            for j in range(1, SR):
                xr, xi = _cmul(a1r, a1i, *local[-1])
                local.append((xr + x_ref[hf, j * SEGS:(j + 1) * SEGS, re], xi + x_ref[hf, j * SEGS:(j + 1) * SEGS, im]))
            er, ei = local[-1]

            for shift in (1, 2, 4):
                cr = jnp.where(row8 >= shift, pq_ref[hf, shift:shift + 1, re], 0.0)
                ci = jnp.where(row8 >= shift, pq_ref[hf, shift:shift + 1, im], 0.0)
                dr, di = _cmul(cr, ci, pltpu.roll(er, shift, 0), pltpu.roll(ei, shift, 0))
                er, ei = er + dr, ei + di
            c0r = carry_ref[hf, SEGS - 1:SEGS, re]
            c0i = carry_ref[hf, SEGS - 1:SEGS, im]
            sr, si = _cmul(pq_ref[hf, 0:SEGS, re], pq_ref[hf, 0:SEGS, im], c0r, c0i)
            sr = sr + jnp.where(row8 >= 1, pltpu.roll(er, 1, 0), 0.0)
            si = si + jnp.where(row8 >= 1, pltpu.roll(ei, 1, 0), 0.0)
            nr, ni = _cmul(pq_ref[hf, SEGS:2 * SEGS, re], pq_ref[hf, SEGS:2 * SEGS, im], c0r, c0i)
            carry_ref[hf, :, re] = er + nr
            carry_ref[hf, :, im] = ei + ni

            for j in range(SR):
                rows = slice(j * SEGS, (j + 1) * SEGS)
                dr, di = _cmul(pj_ref[hf, rows, re], pj_ref[hf, rows, im], sr, si)
                x_ref[hf, rows, re] = local[j][0] + dr
                x_ref[hf, rows, im] = local[j][1] + di

    y = jnp.concatenate(
        [jnp.dot(x_ref[hf].astype(BF16), wc_ref[hf], preferred_element_type=F32) for hf in range(2)], axis=-1)
    y = y + d_ref[...] * u
    z = jax.nn.gelu(y).astype(BF16)
    zz = jnp.dot(z, wglu_ref[...], preferred_element_type=F32)
    out = zz[:, :BRANCH_WIDTH] * _sigmoid(zz[:, BRANCH_WIDTH:])
    for lt in lane_tiles:
        for j in range(SR):
            rq_ref[lt, pl.ds(j, SEGS, stride=PITCH), :] = out[SEGS * j:SEGS * (j + 1), lt * 128:(lt + 1) * 128]
    for lt in lane_tiles:
        for s in range(SEGS):
            o_ref[SR * s:SR * (s + 1), lt * 128:(lt + 1) * 128] = rq_ref[lt, pl.ds(PITCH * s, SR), :].astype(o_ref.dtype)


def _s5(proj, wb, wc, pj, pq, d_skip, w_glu, *, tt=S5_TILE):
    _, B, L, _ = proj.shape
    W = BRANCH_WIDTH
    NS2 = 2 * S5_HALF_STATES
    return pl.pallas_call(
        functools.partial(_s5_kernel, tt=tt),
        grid=(B, L // tt),
        in_specs=[
            _slab_spec(SLAB_S5, tt),
            pl.BlockSpec((2, S5_HALF_CH, NS2), lambda b, t: (0, 0, 0)),
            pl.BlockSpec((2, NS2, S5_HALF_CH), lambda b, t: (0, 0, 0)),
            pl.BlockSpec((2, tt, NS2), lambda b, t: (0, 0, 0)),
            pl.BlockSpec((2, 2 * S5_SEGS, NS2), lambda b, t: (0, 0, 0)),
            pl.BlockSpec((1, W), lambda b, t: (0, 0)),
            pl.BlockSpec((W, 2 * W), lambda b, t: (0, 0)),
        ],
        out_specs=pl.BlockSpec((None, tt, W), lambda b, t: (b, t, 0)),
        out_shape=jax.ShapeDtypeStruct((B, L, W), BF16),
        scratch_shapes=[pltpu.VMEM((2, tt, NS2), F32),
                        pltpu.VMEM((W // 128, S5_SEGS * S5_PITCH, 128), F32),
                        pltpu.VMEM((2, S5_SEGS, NS2), F32)],
        compiler_params=_params(("parallel", "arbitrary")),
        name="s5",
    )(proj, wb, wc, pj, pq, d_skip.reshape(1, W), w_glu)


def _s5_tables(a_re, a_im, log_dt, b_re, b_im, c_re, c_im):
    dt = jnp.exp(log_dt)[:, None]
    mag = jnp.exp(a_re * dt)
    ab_re = mag * jnp.cos(a_im * dt)
    ab_im = mag * jnp.sin(a_im * dt)
    den = a_re * a_re + a_im * a_im
    zr = ((ab_re - 1.0) * a_re + ab_im * a_im) / den
    zi = (ab_im * a_re - (ab_re - 1.0) * a_im) / den
    bb_re = zr[..., None] * b_re - zi[..., None] * b_im
    bb_im = zr[..., None] * b_im + zi[..., None] * b_re
    GH = S5_GROUPS // 2
    eye = jnp.eye(GH, dtype=F32)

    def bdiag_in(m):
        m = m.reshape(2, GH, S5_STATE, S5_GROUP)
        return jnp.einsum('hgpc,gk->hgckp', m, eye).reshape(2, GH * S5_GROUP, GH * S5_STATE)

    def bdiag_out(m):
        m = m.reshape(2, GH, S5_GROUP, S5_STATE)
        return jnp.einsum('hgcp,gk->hgpkc', m, eye).reshape(2, GH * S5_STATE, GH * S5_GROUP)

    wb = jnp.concatenate([bdiag_in(bb_re), bdiag_in(bb_im)], axis=-1).astype(BF16)
    wc = jnp.concatenate([bdiag_out(c_re), -bdiag_out(c_im)], axis=1).astype(BF16)
    sr = S5_TILE // S5_SEGS

    def powers(n):
        n = jnp.asarray(n, F32)[:, None, None]
        mag = jnp.exp(n * (a_re * dt)[None])
        ang = n * (a_im * dt)[None]
        split = lambda p: p.reshape(p.shape[0], 2, S5_HALF_STATES).transpose(1, 0, 2)
        return jnp.concatenate([split(mag * jnp.cos(ang)), split(mag * jnp.sin(ang))], axis=-1)

    pj = powers(np.repeat(np.arange(1, sr + 1), S5_SEGS))
    pq = powers(np.concatenate([sr * np.arange(S5_SEGS), sr * np.arange(1, S5_SEGS + 1)]))
    return wb, wc, pj, pq


def _conv_kernel(a_ref, b_ref, w_ref, cb_ref, lg_ref, lbias_ref, o_ref, z_ref, y_ref, *, tt, rb):
    H = CONV_HALO

    @pl.when(pl.program_id(1) == 0)
    def _():
        z_ref[0:H, :] = jnp.zeros((H, BRANCH_WIDTH), F32)

    z_ref[H:H + tt, :] = a_ref[...].astype(F32) * _sigmoid(b_ref[...].astype(F32))
    cb = cb_ref[...]
    lg = lg_ref[...]
    lbias = lbias_ref[...]

    def block(k, carry):
        r0 = pl.multiple_of(k * rb, rb)
        for lc in range(0, BRANCH_WIDTH, 128):
            win = z_ref[pl.ds(r0, rb + H), lc:lc + 128]
            acc = [jnp.zeros((rb, 128), F32), jnp.zeros((rb, 128), F32)]
            for c in range(8):
                shifted = win if c == 0 else pltpu.roll(win, rb + H - c, 0)
                for off in range(c, H + 1, 8):
                    w = off - (H - CONV_WIDTH + 1)
                    if 0 <= w < CONV_WIDTH:
                        acc[w % 2] = acc[w % 2] + shifted[off - c:off - c + rb, :] * w_ref[w:w + 1, lc:lc + 128]
            y_ref[pl.ds(r0, rb), lc:lc + 128] = acc[0] + acc[1]
        return carry

    lax.fori_loop(0, tt // rb, block, 0)
    z_ref[0:H, :] = z_ref[tt:tt + H, :]

    acc = y_ref[...] + cb
    xc = acc - jnp.mean(acc, axis=-1, keepdims=True)
    var = jnp.mean(xc * xc, axis=-1, keepdims=True)
    y = xc * lax.rsqrt(var + NORM_EPS) * lg + lbias
    o_ref[...] = (y * _sigmoid(y)).astype(o_ref.dtype)


def _conv(proj, conv_w, conv_b, ln_g, ln_b, *, tt=512, rb=32):
    _, B, L, _ = proj.shape
    W = BRANCH_WIDTH
    w_pad = jnp.zeros((CONV_HALO, W), F32).at[:CONV_WIDTH].set(conv_w)
    vec = lambda: pl.BlockSpec((1, W), lambda b, t: (0, 0))
    return pl.pallas_call(
        functools.partial(_conv_kernel, tt=tt, rb=rb),
        grid=(B, L // tt),
        in_specs=[_slab_spec(SLAB_CONV_A, tt), _slab_spec(SLAB_CONV_B, tt),
                  pl.BlockSpec((CONV_HALO, W), lambda b, t: (0, 0)), vec(), vec(), vec()],
        out_specs=pl.BlockSpec((None, tt, W), lambda b, t: (b, t, 0)),
        out_shape=jax.ShapeDtypeStruct((B, L, W), BF16),
        scratch_shapes=[pltpu.VMEM((tt + CONV_HALO, W), F32), pltpu.VMEM((tt, W), F32)],
        compiler_params=_params(("parallel", "arbitrary")),
        name="conformer_conv",
    )(proj, proj, w_pad, conv_b.reshape(1, W), ln_g.reshape(1, W), ln_b.reshape(1, W))


def _rope_table_kernel(pos_ref, inv_ref, cos_ref, sin_ref):
    ang = pos_ref[...] * inv_ref[...]
    lane = lax.broadcasted_iota(jnp.int32, ang.shape, 1)
    cos_ref[...] = jnp.cos(ang)
    sin_ref[...] = jnp.where(lane < ROPE_HALF, -jnp.sin(ang), jnp.sin(ang))


def _rope_tables(positions, *, tm=1024):
    T = positions.size
    inv = ROPE_THETA ** (-jnp.arange(ROPE_HALF, dtype=F32) / ROPE_HALF)
    inv = jnp.concatenate([inv, inv, jnp.zeros((ATT_HEAD_DIM - ROPE_DIM,), F32)]).reshape(1, ATT_HEAD_DIM)
    pos = positions.astype(F32).reshape(T, 1)
    return pl.pallas_call(
        _rope_table_kernel,
        grid=(T // tm,),
        in_specs=[pl.BlockSpec((tm, 1), lambda i: (i, 0)), pl.BlockSpec((1, ATT_HEAD_DIM), lambda i: (0, 0))],
        out_specs=[pl.BlockSpec((tm, ATT_HEAD_DIM), lambda i: (i, 0))] * 2,
        out_shape=[jax.ShapeDtypeStruct((T, ATT_HEAD_DIM), F32)] * 2,
        compiler_params=_params(("parallel",)),
        name="rope_tables",
    )(pos, inv)


def _rope(x, cos, sin):
    lane = lax.broadcasted_iota(jnp.int32, x.shape, 1)
    partner = jnp.where(lane < ROPE_HALF, pltpu.roll(x, ATT_HEAD_DIM - ROPE_HALF, 1), pltpu.roll(x, ROPE_HALF, 1))
    return x * cos + partner * sin


DEINT_STRIDE = 4


def _qkv_kernel(h_ref, wq_ref, wk_ref, wv_ref, cos_ref, sin_ref, o_ref, w16_ref, s_ref, t_ref, *, d):
    tm = h_ref.shape[0]

    @pl.when(pl.program_id(0) == 0)
    def _():
        for j, w_ref in enumerate((wq_ref, wk_ref, wv_ref)):
            w16_ref[j] = w_ref[...].astype(BF16)

    h16 = h_ref[...]
    cos, sin = cos_ref[...], sin_ref[...]
    for j in range(3):
        res = jnp.dot(h16, w16_ref[j], preferred_element_type=F32)
        for h in range(ATT_HEADS_PER_GROUP):
            hs = slice(h * ATT_HEAD_DIM, (h + 1) * ATT_HEAD_DIM)
            s_ref[j, h] = _rope(res[:, hs], cos, sin) if j < 2 else res[:, hs]
            if d <= DEINT_STRIDE:
                for r in range(d):
                    o_ref[j, r, :, hs] = s_ref[j, h, pl.ds(r, tm // d, stride=d), :].astype(o_ref.dtype)
            else:
                q = tm // DEINT_STRIDE
                for r0 in range(DEINT_STRIDE):
                    t_ref[j, h, pl.ds(r0 * q, q), :] = s_ref[j, h, pl.ds(r0, q, stride=DEINT_STRIDE), :]
                for r0 in range(DEINT_STRIDE):
                    for r1 in range(d // DEINT_STRIDE):
                        o_ref[j, DEINT_STRIDE * r1 + r0, :, hs] = t_ref[
                            j, h, pl.ds(r0 * q + r1, tm // d, stride=d // DEINT_STRIDE), :].astype(o_ref.dtype)


def _qkv(h, w_in, layer, cos_t, sin_t, gi, d, B, *, tm=1024):
    T, D = h.shape
    L = T // B
    tiles = L // tm
    W = ATT_HEADS_PER_GROUP * ATT_HEAD_DIM
    assert d <= DEINT_STRIDE or d == DEINT_STRIDE * DEINT_STRIDE
    stage = (3, ATT_HEADS_PER_GROUP, tm, ATT_HEAD_DIM)
    w_spec = lambda j: pl.BlockSpec((None, D, SLAB), lambda i: (layer, 0, W_SLAB_Q + gi + 3 * j))
    return pl.pallas_call(
        functools.partial(_qkv_kernel, d=d),
        grid=(T // tm,),
        in_specs=[
            pl.BlockSpec((tm, D), lambda i: (i, 0)),
            w_spec(0), w_spec(1), w_spec(2),
            pl.BlockSpec((tm, ATT_HEAD_DIM), lambda i: (i, 0)),
            pl.BlockSpec((tm, ATT_HEAD_DIM), lambda i: (i, 0)),
        ],
        out_specs=pl.BlockSpec((3, None, d, tm // d, W), lambda i: (0, i // tiles, 0, i % tiles, 0)),
        out_shape=jax.ShapeDtypeStruct((3, B, d, L // d, W), BF16),
        scratch_shapes=[pltpu.VMEM((3, D, SLAB), BF16),
                        pltpu.VMEM(stage, F32),
                        pltpu.VMEM(stage if d > DEINT_STRIDE else (1, 1, 8, ATT_HEAD_DIM), F32)],
        compiler_params=_params(("arbitrary",)),
        name=f"qkv_proj_g{gi}",
    )(h, w_in, w_in, w_in, cos_t, sin_t)


ATT_UNROLL = 8


def _att_kernel(q_ref, kc_ref, kp_ref, vc_ref, vp_ref, o_ref, lse_ref, *, d, nq):
    Q = ATT_QBLOCK
    n = pl.program_id(1)
    qi = lax.broadcasted_iota(jnp.int32, (Q, 2 * Q), 0)
    kj = lax.broadcasted_iota(jnp.int32, (Q, 2 * Q), 1)
    rel = Q + qi - kj
    band = (rel >= 0) & (rel <= Q)
    band_first = band & ((n > 0) | (kj >= Q))
    head_of_lane = lax.broadcasted_iota(jnp.int32, (Q, ATT_HEAD_DIM), 1) // (ATT_HEAD_DIM // ATT_HEADS_PER_GROUP)
    nt = (((1,), (1,)), ((), ()))

    def block(r, i):
        rows = slice(i * Q, (i + 1) * Q)
        outs = []
        lse = jnp.zeros((Q, ATT_HEAD_DIM), F32)
        for h in range(ATT_HEADS_PER_GROUP):
            hs = slice(h * ATT_HEAD_DIM, (h + 1) * ATT_HEAD_DIM)
            if i == 0:
                k = jnp.concatenate([kp_ref[r, :, hs], kc_ref[r, rows, hs]], axis=0)
                v = jnp.concatenate([vp_ref[r, :, hs], vc_ref[r, rows, hs]], axis=0)
            else:
                k = kc_ref[r, (i - 1) * Q:(i + 1) * Q, hs]
                v = vc_ref[r, (i - 1) * Q:(i + 1) * Q, hs]
            s = lax.dot_general(q_ref[r, rows, hs], k, nt, preferred_element_type=F32) * (ATT_HEAD_DIM ** -0.5)
            s = jnp.where(band_first if i == 0 else band, s, MASK_VALUE)
            m = jnp.max(s, axis=-1, keepdims=True)
            p = jnp.exp(s - m)
            l = jnp.sum(p, axis=-1, keepdims=True)
            outs.append(jnp.dot(p.astype(BF16), v, preferred_element_type=F32) / l)
            lse = jnp.where(head_of_lane == h, m + jnp.log(l), lse)
        o_ref[r, rows, :] = jnp.concatenate(outs, axis=-1).astype(o_ref.dtype)
        lse_ref[r, rows, :] = lse

    if d <= ATT_UNROLL:
        for r in range(d):
            for i in range(nq):
                block(r, i)
    else:
        def residues(it, carry):
            for u in range(ATT_UNROLL):
                for i in range(nq):
                    block(it * ATT_UNROLL + u, i)
            return carry

        lax.fori_loop(0, d // ATT_UNROLL, residues, 0)


def _attention_group(qkv, gi):
    _, B, d, n_sub, W = qkv.shape
    Q = ATT_QBLOCK
    nq = max(1, ATT_UNROLL // d)

    def cur(s):
        return pl.BlockSpec((None, None, d, nq * Q, W), lambda b, n: (s, b, 0, n, 0))

    def prev(s):
        return pl.BlockSpec((None, None, d, Q, W), lambda b, n: (s, b, 0, jnp.maximum(nq * n - 1, 0), 0))

    return pl.pallas_call(
        functools.partial(_att_kernel, d=d, nq=nq),
        grid=(B, n_sub // (nq * Q)),
        in_specs=[cur(0), cur(1), prev(1), cur(2), prev(2)],
        out_specs=[pl.BlockSpec((None, d, nq * Q, W), lambda b, n: (b, 0, n, 0)),
                   pl.BlockSpec((None, d, nq * Q, ATT_HEAD_DIM), lambda b, n: (b, 0, n, 0))],
        out_shape=[jax.ShapeDtypeStruct((B, d, n_sub, W), BF16),
                   jax.ShapeDtypeStruct((B, d, n_sub, ATT_HEAD_DIM), F32)],
        compiler_params=_params(("parallel", "arbitrary")),
        name=f"dilated_attention_g{gi}",
    )(qkv, qkv, qkv, qkv, qkv)


def _merge_kernel(x_ref, ya_ref, yb_ref, yc_ref, o1_ref, o2_ref, o3_ref, l1_ref, l2_ref, l3_ref,
                  g0_ref, g1_ref, g2_ref, g3_ref, g4_ref, g5_ref, g6_ref, g7_ref, wb_ref, wo_ref, out_ref,
                  os_ref, ls_ref):
    for g, (o_ref, l_ref) in enumerate(((o1_ref, l1_ref), (o2_ref, l2_ref), (o3_ref, l3_ref))):
        d, rows = o_ref.shape[0], o_ref.shape[1]
        for r in range(d):
            o = o_ref[r].astype(F32)
            for h in range(ATT_HEADS_PER_GROUP):
                os_ref[g, h, pl.ds(r, rows, stride=d), :] = o[:, h * ATT_HEAD_DIM:(h + 1) * ATT_HEAD_DIM]
            ls_ref[g, pl.ds(r, rows, stride=d), :] = l_ref[r]
    lanes = ATT_HEAD_DIM // ATT_HEADS_PER_GROUP
    parts = []
    for h in range(ATT_HEADS_PER_GROUP):
        l1, l2, l3 = (ls_ref[g, :, h * lanes:h * lanes + 1] for g in range(3))
        m = jnp.maximum(jnp.maximum(l1, l2), l3)
        e1, e2, e3 = jnp.exp(l1 - m), jnp.exp(l2 - m), jnp.exp(l3 - m)
        parts.append((e1 * os_ref[0, h] + e2 * os_ref[1, h] + e3 * os_ref[2, h]) / (e1 + e2 + e3))
    yd = jnp.concatenate(parts, axis=-1)
    ys = (ya_ref[...], yb_ref[...], yc_ref[...], yd.astype(BF16))
    gates = ((g0_ref, g1_ref), (g2_ref, g3_ref), (g4_ref, g5_ref), (g6_ref, g7_ref))
    merged = None
    for k in range(N_BRANCHES):
        gate = _sigmoid(jnp.concatenate([gates[k][0][...], gates[k][1][...]], axis=-1).astype(F32))
        term = gate * jnp.dot(ys[k], wb_ref[k], preferred_element_type=F32)
        merged = term if merged is None else merged + term
    out_ref[...] = x_ref[...] + jnp.dot(merged.astype(BF16), wo_ref[...], preferred_element_type=F32)


def _merge(x, ya, yb, yc, att, proj, w_branch, w_out, *, tm=512):
    B, L, D = x.shape
    W = BRANCH_WIDTH
    (o1, l1), (o2, l2), (o3, l3) = att
    row = lambda: pl.BlockSpec((None, tm, W), lambda b, t: (b, t, 0))

    def res(a):
        d, last = a.shape[1], a.shape[3]
        return pl.BlockSpec((None, d, tm // d, last), lambda b, t: (b, 0, t, 0))

    return pl.pallas_call(
        _merge_kernel,
        grid=(B, L // tm),
        in_specs=[pl.BlockSpec((None, tm, D), lambda b, t: (b, t, 0))] + [row() for _ in range(3)]
                 + [res(a) for a in (o1, o2, o3, l1, l2, l3)]
                 + [_slab_spec(SLAB_GATE + s, tm) for s in range(8)]
                 + [pl.BlockSpec((N_BRANCHES, W, D), lambda b, t: (0, 0, 0)), pl.BlockSpec((D, D), lambda b, t: (0, 0))],
        out_specs=pl.BlockSpec((None, tm, D), lambda b, t: (b, t, 0)),
        out_shape=jax.ShapeDtypeStruct((B, L, D), F32),
        scratch_shapes=[pltpu.VMEM((3, ATT_HEADS_PER_GROUP, tm, ATT_HEAD_DIM), F32),
                        pltpu.VMEM((3, tm, ATT_HEAD_DIM), F32)],
        compiler_params=_params(("parallel", "parallel")),
        name="gated_merge",
    )(x, ya, yb, yc, o1, o2, o3, l1, l2, l3, *([proj] * 8), w_branch, w_out)


def kernel(x, positions, ffn1_norm, ffn1_w_gate, ffn1_w_up, ffn1_w_down, mix_norm, w_in, hg_lb_logits, hg_gnorm, s5_a_re, s5_a_im, s5_log_dt, s5_b_re, s5_b_im, s5_c_re, s5_c_im, s5_d, s5_w_glu, conv_w, conv_b, conv_ln_g, conv_ln_b, w_branch, w_out, ffn2_norm, ffn2_w_gate, ffn2_w_up, ffn2_w_down, final_norm):
    B, L, D = x.shape
    T = B * L
    depth = w_in.shape[0]
    lb_soft = jax.nn.softmax(hg_lb_logits.astype(F32), axis=0)
    lb_all = jnp.cumsum(lb_soft, axis=0) - lb_soft[0]
    cos_t, sin_t = _rope_tables(positions)
    bf = lambda w: w.astype(BF16)

    xt = x.reshape(T, D)
    for l in range(depth):
        xt = _ffn(xt, ffn1_norm[l], bf(ffn1_w_gate[l]), bf(ffn1_w_up[l]), bf(ffn1_w_down[l]))
        proj, h_mix = _proj(xt, mix_norm[l], w_in, l)
        proj = proj.reshape(N_SLABS, B, L, SLAB)
        ya = _hgrn(proj, lb_all[l], hg_gnorm[l])
        wb, wc, pj, pq = _s5_tables(s5_a_re[l], s5_a_im[l], s5_log_dt[l], s5_b_re[l], s5_b_im[l], s5_c_re[l], s5_c_im[l])
        yb = _s5(proj, wb, wc, pj, pq, s5_d[l], bf(s5_w_glu[l]))
        yc = _conv(proj, conv_w[l], conv_b[l], conv_ln_g[l], conv_ln_b[l])
        att = [_attention_group(_qkv(h_mix, w_in, l, cos_t, sin_t, gi, dil, B), gi)
               for gi, (_, dil) in enumerate(ATT_CONFIGS)]
        xt = _merge(xt.reshape(B, L, D), ya, yb, yc, att, proj, bf(w_branch[l]), bf(w_out[l])).reshape(T, D)
        last = l == depth - 1
        xt = _ffn(xt, ffn2_norm[l], bf(ffn2_w_gate[l]), bf(ffn2_w_up[l]), bf(ffn2_w_down[l]),
                  final_norm if last else None)
    return xt.reshape(B, L, D)
```

```python
import functools
import math

import numpy as np
import jax
import jax.numpy as jnp
from jax import lax
from jax.experimental import pallas as pl
from jax.experimental.pallas import tpu as pltpu

F32 = jnp.float32
BF16 = jnp.bfloat16

NORM_EPS = 1e-6
MASK_VALUE = -1e30
D_MODEL = 1024
D_FF = 2816
N_BRANCHES = 4
BRANCH_WIDTH = 512

HG_HEADS = 4
HG_KDIM = 128
HG_VDIM = 128
HG_CHUNK = 64
HG_SUB = 16
HG_ACC_CHAINS = 4

S5_GROUP = 16
S5_GROUPS = 32
S5_STATE = 64
S5_HALF_CH = 256
S5_HALF_STATES = (S5_GROUPS // 2) * S5_STATE
S5_TILE = 256
S5_SEGS = 8
S5_PITCH = S5_TILE // S5_SEGS + 8
S5_SCAN_LANES = 512

CONV_WIDTH = 31
CONV_HALO = 32

ATT_HEAD_DIM = 128
ATT_CONFIGS = ((128, 1), (512, 4), (2048, 16))
ATT_HEADS_PER_GROUP = 4
ATT_HEADS = 12
ATT_QBLOCK = 128
ROPE_THETA = 500000.0
ROPE_DIM = 32
ROPE_HALF = 16

SLAB = 512
SLAB_HQ, SLAB_HF, SLAB_HI, SLAB_HG, SLAB_S5, SLAB_CONV_A, SLAB_CONV_B = 0, 1, 2, 3, 4, 5, 6
W_SLAB_Q = 7
W_SLAB_GATE = 16
SLABS_PER_STEP = 3
SLAB_GATE = 7
N_SLABS = 15

VMEM_LIMIT = 56 * 1024 * 1024


def _params(sem):
    return pltpu.CompilerParams(dimension_semantics=sem, vmem_limit_bytes=VMEM_LIMIT)


def _rms(x):
    return x * lax.rsqrt(jnp.mean(x * x, axis=-1, keepdims=True) + NORM_EPS)


def _sigmoid(x):
    return jax.nn.sigmoid(x)


def _ffn_kernel(x_ref, gain_ref, wg_ref, wu_ref, wd_ref, *rest, final, emit_h):
    rest = list(rest)
    fgain_ref = rest.pop(0) if final else None
    hgain_ref = rest.pop(0) if emit_h else None
    o_ref = rest.pop(0)
    x = x_ref[...]
    h = (_rms(x) * gain_ref[...]).astype(BF16)
    g = jnp.dot(h, wg_ref[...], preferred_element_type=F32)
    u = jnp.dot(h, wu_ref[...], preferred_element_type=F32)
    a = (g * _sigmoid(g) * u).astype(BF16)
    y = x + 0.5 * jnp.dot(a, wd_ref[...], preferred_element_type=F32)
    if final:
        y = _rms(y) * fgain_ref[...]
    o_ref[...] = y
    if emit_h:
        rest[0][...] = (_rms(y) * hgain_ref[...]).astype(BF16)


def _ffn(x, gain, wg, wu, wd, *, final_gain=None, next_gain=None, tm=256):
    T, D = x.shape
    FF = wg.shape[1]
    final, emit_h = final_gain is not None, next_gain is not None
    vec = lambda: pl.BlockSpec((1, D), lambda i: (0, 0))
    row = lambda: pl.BlockSpec((tm, D), lambda i: (i, 0))
    in_specs = [row(), vec(), pl.BlockSpec((D, FF), lambda i: (0, 0)), pl.BlockSpec((D, FF), lambda i: (0, 0)),
                pl.BlockSpec((FF, D), lambda i: (0, 0))]
    args = [x, gain.reshape(1, D), wg, wu, wd]
    for extra in (final_gain, next_gain):
        if extra is not None:
            in_specs.append(vec())
            args.append(extra.reshape(1, D))
    out_specs, out_shape = [row()], [jax.ShapeDtypeStruct((T, D), F32)]
    if emit_h:
        out_specs.append(row())
        out_shape.append(jax.ShapeDtypeStruct((T, D), BF16))
    out = pl.pallas_call(
        functools.partial(_ffn_kernel, final=final, emit_h=emit_h),
        grid=(T // tm,),
        in_specs=in_specs,
        out_specs=out_specs,
        out_shape=out_shape,
        compiler_params=_params(("parallel",)),
        name="ffn_final" if final else "ffn",
    )(*args)
    return out if emit_h else out[0]


def _proj_kernel(h_ref, *refs):
    w_refs, o_ref = refs[:SLABS_PER_STEP], refs[SLABS_PER_STEP]
    h = h_ref[...]
    for s, w_ref in enumerate(w_refs):
        o_ref[s] = jnp.dot(h, w_ref[...].astype(BF16), preferred_element_type=F32).astype(o_ref.dtype)


def _proj(h, w_in, layer, *, tm=2048):
    T, D = h.shape

    def w_spec(k):
        def index(i, j):
            s = j * SLABS_PER_STEP + k
            return layer, 0, jnp.where(s < SLAB_GATE, s, s + (W_SLAB_GATE - SLAB_GATE))
        return pl.BlockSpec((None, D, SLAB), index)

    return pl.pallas_call(
        _proj_kernel,
        grid=(T // tm, N_SLABS // SLABS_PER_STEP),
        in_specs=[pl.BlockSpec((tm, D), lambda i, j: (i, 0))] + [w_spec(k) for k in range(SLABS_PER_STEP)],
        out_specs=pl.BlockSpec((SLABS_PER_STEP, tm, SLAB), lambda i, j: (j, i, 0)),
        out_shape=jax.ShapeDtypeStruct((N_SLABS, T, SLAB), BF16),
        compiler_params=_params(("parallel", "arbitrary")),
        name="in_proj",
    )(h, *([w_in] * SLABS_PER_STEP))


def _slab_spec(slab, rows):
    return pl.BlockSpec((None, None, rows, SLAB), lambda b, t: (slab, b, t, 0))


def _split3_bf16(x):
    hi = x.astype(BF16)
    r = x - hi.astype(F32)
    mid = r.astype(BF16)
    lo = (r - mid.astype(F32)).astype(BF16)
    return hi, mid, lo


def _hgrn_kernel(q_ref, f_ref, i_ref, g_ref, lb_ref, gn_ref, o_ref, st_ref, b_ref, k_ref, v_ref, *, n_chunks):
    C, S = HG_CHUNK, HG_SUB
    tt = n_chunks * C
    heads = [slice(h * HG_KDIM, (h + 1) * HG_KDIM) for h in range(HG_HEADS)]

    @pl.when(pl.program_id(1) == 0)
    def _():
        st_ref[...] = jnp.zeros_like(st_ref)

    lb = lb_ref[...]
    row = lax.broadcasted_iota(jnp.int32, (tt, tt), 0)
    col = lax.broadcasted_iota(jnp.int32, (tt, tt), 1)
    tri = ((col <= row) & (row // C == col // C)).astype(BF16)
    row8 = lax.broadcasted_iota(jnp.int32, (8, 1), 0)
    nt = (((1,), (1,)), ((), ()))
    tn = (((0,), (0,)), ((), ()))

    q = q_ref[...].astype(F32)
    f = f_ref[...].astype(F32)
    v = i_ref[...].astype(F32)
    v16 = v.astype(BF16)
    qf = q * _sigmoid(q) * (HG_KDIM ** -0.5)
    sf = _sigmoid(f)
    kf = (1.0 - lb) * (1.0 - sf)
    logf = jnp.log(lb + (1.0 - lb) * sf)
    b = sum(jnp.dot(tri, p, preferred_element_type=F32) for p in _split3_bf16(logf))
    for h, hs in enumerate(heads):
        b_ref[h] = b[:, hs]
        k_ref[h] = kf[:, hs]
        v_ref[h] = v[:, hs]

    state = [st_ref[h] for h in range(HG_HEADS)]
    mm = []
    for c in range(n_chunks):
        rows = slice(c * C, (c + 1) * C)
        bc, qc, kc, vc = b[rows], qf[rows], kf[rows], v16[rows]
        b_last = bc[C - 1:C, :]
        qd = (qc * jnp.exp(bc)).astype(BF16)
        kd_last = (kc * jnp.exp(b_last - bc)).astype(BF16)
        outs = []
        for h, hs in enumerate(heads):
            outs.append(lax.dot_general(qd[:, hs], state[h].astype(BF16), nt, preferred_element_type=F32))
            state[h] = jnp.exp(b_last[:, hs]) * state[h] + lax.dot_general(
                vc[:, hs], kd_last[:, hs], tn, preferred_element_type=F32)
        mm.append(outs)
    for h in range(HG_HEADS):
        st_ref[h] = state[h]

    zero = lambda n: jnp.zeros((n, HG_KDIM), F32)
    off = [jnp.zeros((tt, HG_VDIM), F32) for _ in range(HG_HEADS)]
    n = S
    while n < C:
        same_pair = (row // (2 * n)) == (col // (2 * n))
        for h, hs in enumerate(heads):
            qs, ks = [], []
            for lo in range(0, tt, 2 * n):
                mid, hi = lo + n, lo + 2 * n
                r = b_ref[h, pl.ds(mid - 1, n, stride=0), :]
                qs += [zero(n), qf[mid:hi, hs] * jnp.exp(b[mid:hi, hs] - r)]
                ks += [kf[lo:mid, hs] * jnp.exp(r - b[lo:mid, hs]), zero(n)]
            att = lax.dot_general(jnp.concatenate(qs, axis=0).astype(BF16), jnp.concatenate(ks, axis=0).astype(BF16),
                                  nt, preferred_element_type=F32)
            att = jnp.where(same_pair, att, 0.0).astype(BF16)
            off[h] = off[h] + jnp.dot(att, v16[:, hs], preferred_element_type=F32)
        n *= 2

    normed = []
    for h, hs in enumerate(heads):
        tiles = []
        for base in range(0, tt, S):
            for lo in range(0, S, 8):
                qq = qf[base + lo:base + lo + 8, hs]
                bb = b[base + lo:base + lo + 8, hs]
                acc = [jnp.zeros((8, HG_VDIM), F32) for _ in range(HG_ACC_CHAINS)]
                for j in range(lo + 8):
                    key = pl.ds(base + j, 8, stride=0)
                    p = qq * k_ref[h, key, :] * jnp.exp(jnp.minimum(bb - b_ref[h, key, :], 0.0))
                    a = jnp.sum(p, axis=-1, keepdims=True)
                    if j >= lo:
                        a = jnp.where(row8 >= j - lo, a, 0.0)
                    acc[j % HG_ACC_CHAINS] = acc[j % HG_ACC_CHAINS] + a * v_ref[h, key, :]
                tiles.append((acc[0] + acc[1]) + (acc[2] + acc[3]))
        o = jnp.concatenate(tiles, axis=0) + (jnp.concatenate([mm[c][h] for c in range(n_chunks)], axis=0) + off[h])
        normed.append(o * lax.rsqrt(jnp.mean(o * o, axis=-1, keepdims=True) + NORM_EPS))
    g = g_ref[...].astype(F32)
    o_ref[...] = (jnp.concatenate(normed, axis=-1) * gn_ref[...] * (g * _sigmoid(g))).astype(o_ref.dtype)


def _hgrn(proj, lb, gnorm, *, tt=256):
    _, B, L, _ = proj.shape
    W = HG_HEADS * HG_KDIM
    return pl.pallas_call(
        functools.partial(_hgrn_kernel, n_chunks=tt // HG_CHUNK),
        grid=(B, L // tt),
        in_specs=[
            _slab_spec(SLAB_HQ, tt), _slab_spec(SLAB_HF, tt), _slab_spec(SLAB_HI, tt), _slab_spec(SLAB_HG, tt),
            pl.BlockSpec((1, W), lambda b, t: (0, 0)),
            pl.BlockSpec((1, W), lambda b, t: (0, 0)),
        ],
        out_specs=pl.BlockSpec((None, tt, W), lambda b, t: (b, t, 0)),
        out_shape=jax.ShapeDtypeStruct((B, L, W), BF16),
        scratch_shapes=[pltpu.VMEM((HG_HEADS, HG_VDIM, HG_KDIM), F32)] + [pltpu.VMEM((HG_HEADS, tt, HG_KDIM), F32)] * 3,
        compiler_params=_params(("parallel", "arbitrary")),
        name="hgrn2",
    )(proj, proj, proj, proj, lb.reshape(1, W), gnorm.reshape(1, W))


def _cmul(ar, ai, br, bi):
    return ar * br - ai * bi, ar * bi + ai * br


def _s5_kernel(u_ref, wb_ref, wc_ref, pj_ref, pq_ref, d_ref, wglu_ref, o_ref, x_ref, rq_ref, carry_ref, *, tt):
    SEGS, SR, PITCH, NS, LC = S5_SEGS, tt // S5_SEGS, S5_PITCH, S5_HALF_STATES, S5_SCAN_LANES
    row8 = lax.broadcasted_iota(jnp.int32, (SEGS, LC), 0)
    lane_tiles = range(BRANCH_WIDTH // 128)

    @pl.when(pl.program_id(1) == 0)
    def _():
        carry_ref[...] = jnp.zeros_like(carry_ref)

    u = u_ref[...].astype(F32)
    for lt in lane_tiles:
        for s in range(SEGS):
            rq_ref[lt, pl.ds(PITCH * s, SR), :] = u[SR * s:SR * (s + 1), lt * 128:(lt + 1) * 128]
    u = jnp.concatenate(
        [jnp.concatenate([rq_ref[lt, pl.ds(j, SEGS, stride=PITCH), :] for lt in lane_tiles], axis=-1)
         for j in range(SR)], axis=0)
    u16 = u.astype(BF16)

    for hf in range(2):
        x_ref[hf] = jnp.dot(u16[:, hf * S5_HALF_CH:(hf + 1) * S5_HALF_CH], wb_ref[hf],
                            preferred_element_type=F32)

    for hf in range(2):
        for lc in range(NS // LC):
            re = slice(lc * LC, (lc + 1) * LC)
            im = slice(NS + lc * LC, NS + (lc + 1) * LC)
            a1r, a1i = pj_ref[hf, 0:SEGS, re], pj_ref[hf, 0:SEGS, im]

            local = [(x_ref[hf, 0:SEGS, re], x_ref[hf, 0:SEGS, im])]
            for j in range(1, SR):
                xr, xi = _cmul(a1r, a1i, *local[-1])
                local.append((xr + x_ref[hf, j * SEGS:(j + 1) * SEGS, re], xi + x_ref[hf, j * SEGS:(j + 1) * SEGS, im]))
            er, ei = local[-1]

            for shift in (1, 2, 4):
                cr = jnp.where(row8 >= shift, pq_ref[hf, shift:shift + 1, re], 0.0)
                ci = jnp.where(row8 >= shift, pq_ref[hf, shift:shift + 1, im], 0.0)
                dr, di = _cmul(cr, ci, pltpu.roll(er, shift, 0), pltpu.roll(ei, shift, 0))
                er, ei = er + dr, ei + di
            c0r = carry_ref[hf, SEGS - 1:SEGS, re]
            c0i = carry_ref[hf, SEGS - 1:SEGS, im]
            sr, si = _cmul(pq_ref[hf, 0:SEGS, re], pq_ref[hf, 0:SEGS, im], c0r, c0i)
            sr = sr + jnp.where(row8 >= 1, pltpu.roll(er, 1, 0), 0.0)
            si = si + jnp.where(row8 >= 1, pltpu.roll(ei, 1, 0), 0.0)
            nr, ni = _cmul(pq_ref[hf, SEGS:2 * SEGS, re], pq_ref[hf, SEGS:2 * SEGS, im], c0r, c0i)
            carry_ref[hf, :, re] = er + nr
            carry_ref[hf, :, im] = ei + ni

            for j in range(SR):
                rows = slice(j * SEGS, (j + 1) * SEGS)
                dr, di = _cmul(pj_ref[hf, rows, re], pj_ref[hf, rows, im], sr, si)
                x_ref[hf, rows, re] = local[j][0] + dr
                x_ref[hf, rows, im] = local[j][1] + di

    y = jnp.concatenate(
        [jnp.dot(x_ref[hf].astype(BF16), wc_ref[hf], preferred_element_type=F32) for hf in range(2)], axis=-1)
    y = y + d_ref[...] * u
    z = jax.nn.gelu(y).astype(BF16)
    zz = jnp.dot(z, wglu_ref[...], preferred_element_type=F32)
    out = zz[:, :BRANCH_WIDTH] * _sigmoid(zz[:, BRANCH_WIDTH:])
    for lt in lane_tiles:
        for j in range(SR):
            rq_ref[lt, pl.ds(j, SEGS, stride=PITCH), :] = out[SEGS * j:SEGS * (j + 1), lt * 128:(lt + 1) * 128]
    for lt in lane_tiles:
        for s in range(SEGS):
            o_ref[SR * s:SR * (s + 1), lt * 128:(lt + 1) * 128] = rq_ref[lt, pl.ds(PITCH * s, SR), :].astype(o_ref.dtype)


def _s5(proj, wb, wc, pj, pq, d_skip, w_glu, *, tt=S5_TILE):
    _, B, L, _ = proj.shape
    W = BRANCH_WIDTH
    NS2 = 2 * S5_HALF_STATES
    return pl.pallas_call(
        functools.partial(_s5_kernel, tt=tt),
        grid=(B, L // tt),
        in_specs=[
            _slab_spec(SLAB_S5, tt),
            pl.BlockSpec((2, S5_HALF_CH, NS2), lambda b, t: (0, 0, 0)),
            pl.BlockSpec((2, NS2, S5_HALF_CH), lambda b, t: (0, 0, 0)),
            pl.BlockSpec((2, tt, NS2), lambda b, t: (0, 0, 0)),
            pl.BlockSpec((2, 2 * S5_SEGS, NS2), lambda b, t: (0, 0, 0)),
            pl.BlockSpec((1, W), lambda b, t: (0, 0)),
            pl.BlockSpec((W, 2 * W), lambda b, t: (0, 0)),
        ],
        out_specs=pl.BlockSpec((None, tt, W), lambda b, t: (b, t, 0)),
        out_shape=jax.ShapeDtypeStruct((B, L, W), BF16),
        scratch_shapes=[pltpu.VMEM((2, tt, NS2), F32),
                        pltpu.VMEM((W // 128, S5_SEGS * S5_PITCH, 128), F32),
                        pltpu.VMEM((2, S5_SEGS, NS2), F32)],
        compiler_params=_params(("parallel", "arbitrary")),
        name="s5",
    )(proj, wb, wc, pj, pq, d_skip.reshape(1, W), w_glu)


def _s5_tables(a_re, a_im, log_dt, b_re, b_im, c_re, c_im):
    dt = jnp.exp(log_dt)[:, None]
    mag = jnp.exp(a_re * dt)
    ab_re = mag * jnp.cos(a_im * dt)
    ab_im = mag * jnp.sin(a_im * dt)
    den = a_re * a_re + a_im * a_im
    zr = ((ab_re - 1.0) * a_re + ab_im * a_im) / den
    zi = (ab_im * a_re - (ab_re - 1.0) * a_im) / den
    bb_re = zr[..., None] * b_re - zi[..., None] * b_im
    bb_im = zr[..., None] * b_im + zi[..., None] * b_re
    GH = S5_GROUPS // 2
    eye = jnp.eye(GH, dtype=F32)

    def bdiag_in(m):
        m = m.reshape(2, GH, S5_STATE, S5_GROUP)
        return jnp.einsum('hgpc,gk->hgckp', m, eye).reshape(2, GH * S5_GROUP, GH * S5_STATE)

    def bdiag_out(m):
        m = m.reshape(2, GH, S5_GROUP, S5_STATE)
        return jnp.einsum('hgcp,gk->hgpkc', m, eye).reshape(2, GH * S5_STATE, GH * S5_GROUP)

    wb = jnp.concatenate([bdiag_in(bb_re), bdiag_in(bb_im)], axis=-1).astype(BF16)
    wc = jnp.concatenate([bdiag_out(c_re), -bdiag_out(c_im)], axis=1).astype(BF16)
    sr = S5_TILE // S5_SEGS

    def powers(n):
        n = jnp.asarray(n, F32)[:, None, None]
        mag = jnp.exp(n * (a_re * dt)[None])
        ang = n * (a_im * dt)[None]
        split = lambda p: p.reshape(p.shape[0], 2, S5_HALF_STATES).transpose(1, 0, 2)
        return jnp.concatenate([split(mag * jnp.cos(ang)), split(mag * jnp.sin(ang))], axis=-1)

    pj = powers(np.repeat(np.arange(1, sr + 1), S5_SEGS))
    pq = powers(np.concatenate([sr * np.arange(S5_SEGS), sr * np.arange(1, S5_SEGS + 1)]))
    return wb, wc, pj, pq


def _conv_kernel(a_ref, b_ref, w_ref, cb_ref, lg_ref, lbias_ref, o_ref, z_ref, y_ref, *, tt, rb):
    H = CONV_HALO

    @pl.when(pl.program_id(1) == 0)
    def _():
        z_ref[0:H, :] = jnp.zeros((H, BRANCH_WIDTH), F32)

    z_ref[H:H + tt, :] = a_ref[...].astype(F32) * _sigmoid(b_ref[...].astype(F32))
    cb = cb_ref[...]
    lg = lg_ref[...]
    lbias = lbias_ref[...]

    def block(k, carry):
        r0 = pl.multiple_of(k * rb, rb)
        for lc in range(0, BRANCH_WIDTH, 128):
            win = z_ref[pl.ds(r0, rb + H), lc:lc + 128]
            acc = [jnp.zeros((rb, 128), F32), jnp.zeros((rb, 128), F32)]
            for c in range(8):
                shifted = win if c == 0 else pltpu.roll(win, rb + H - c, 0)
                for off in range(c, H + 1, 8):
                    w = off - (H - CONV_WIDTH + 1)
                    if 0 <= w < CONV_WIDTH:
                        acc[w % 2] = acc[w % 2] + shifted[off - c:off - c + rb, :] * w_ref[w:w + 1, lc:lc + 128]
            y_ref[pl.ds(r0, rb), lc:lc + 128] = acc[0] + acc[1]
        return carry

    lax.fori_loop(0, tt // rb, block, 0)
    z_ref[0:H, :] = z_ref[tt:tt + H, :]

    acc = y_ref[...] + cb
    xc = acc - jnp.mean(acc, axis=-1, keepdims=True)
    var = jnp.mean(xc * xc, axis=-1, keepdims=True)
    y = xc * lax.rsqrt(var + NORM_EPS) * lg + lbias
    o_ref[...] = (y * _sigmoid(y)).astype(o_ref.dtype)


def _conv(proj, conv_w, conv_b, ln_g, ln_b, *, tt=512, rb=32):
    _, B, L, _ = proj.shape
    W = BRANCH_WIDTH
    w_pad = jnp.zeros((CONV_HALO, W), F32).at[:CONV_WIDTH].set(conv_w)
    vec = lambda: pl.BlockSpec((1, W), lambda b, t: (0, 0))
    return pl.pallas_call(
        functools.partial(_conv_kernel, tt=tt, rb=rb),
        grid=(B, L // tt),
        in_specs=[_slab_spec(SLAB_CONV_A, tt), _slab_spec(SLAB_CONV_B, tt),
                  pl.BlockSpec((CONV_HALO, W), lambda b, t: (0, 0)), vec(), vec(), vec()],
        out_specs=pl.BlockSpec((None, tt, W), lambda b, t: (b, t, 0)),
        out_shape=jax.ShapeDtypeStruct((B, L, W), BF16),
        scratch_shapes=[pltpu.VMEM((tt + CONV_HALO, W), F32), pltpu.VMEM((tt, W), F32)],
        compiler_params=_params(("parallel", "arbitrary")),
        name="conformer_conv",
    )(proj, proj, w_pad, conv_b.reshape(1, W), ln_g.reshape(1, W), ln_b.reshape(1, W))


def _rope_table_kernel(pos_ref, inv_ref, cos_ref, sin_ref):
    ang = pos_ref[...] * inv_ref[...]
    lane = lax.broadcasted_iota(jnp.int32, ang.shape, 1)
    cos_ref[...] = jnp.cos(ang)
    sin_ref[...] = jnp.where(lane < ROPE_HALF, -jnp.sin(ang), jnp.sin(ang))


def _rope_tables(positions, *, tm=1024):
    T = positions.size
    inv = ROPE_THETA ** (-jnp.arange(ROPE_HALF, dtype=F32) / ROPE_HALF)
    inv = jnp.concatenate([inv, inv, jnp.zeros((ATT_HEAD_DIM - ROPE_DIM,), F32)]).reshape(1, ATT_HEAD_DIM)
    pos = positions.astype(F32).reshape(T, 1)
    return pl.pallas_call(
        _rope_table_kernel,
        grid=(T // tm,),
        in_specs=[pl.BlockSpec((tm, 1), lambda i: (i, 0)), pl.BlockSpec((1, ATT_HEAD_DIM), lambda i: (0, 0))],
        out_specs=[pl.BlockSpec((tm, ATT_HEAD_DIM), lambda i: (i, 0))] * 2,
        out_shape=[jax.ShapeDtypeStruct((T, ATT_HEAD_DIM), F32)] * 2,
        compiler_params=_params(("parallel",)),
        name="rope_tables",
    )(pos, inv)


def _rope(x, cos, sin):
    lane = lax.broadcasted_iota(jnp.int32, x.shape, 1)
    partner = jnp.where(lane < ROPE_HALF, pltpu.roll(x, ATT_HEAD_DIM - ROPE_HALF, 1), pltpu.roll(x, ROPE_HALF, 1))
    return x * cos + partner * sin


DEINT_STRIDE = 4


def _qkv_kernel(h_ref, wq_ref, wk_ref, wv_ref, cos_ref, sin_ref, o_ref, w16_ref, s_ref, t_ref, *, d):
    tm = h_ref.shape[0]

    @pl.when(pl.program_id(0) == 0)
    def _():
        for j, w_ref in enumerate((wq_ref, wk_ref, wv_ref)):
            w16_ref[j] = w_ref[...].astype(BF16)

    h16 = h_ref[...]
    cos, sin = cos_ref[...], sin_ref[...]
    for j in range(3):
        res = jnp.dot(h16, w16_ref[j], preferred_element_type=F32)
        for h in range(ATT_HEADS_PER_GROUP):
            hs = slice(h * ATT_HEAD_DIM, (h + 1) * ATT_HEAD_DIM)
            s_ref[j, h] = _rope(res[:, hs], cos, sin) if j < 2 else res[:, hs]
            if d <= DEINT_STRIDE:
                for r in range(d):
                    o_ref[j, r, :, hs] = s_ref[j, h, pl.ds(r, tm // d, stride=d), :].astype(o_ref.dtype)
            else:
                q = tm // DEINT_STRIDE
                for r0 in range(DEINT_STRIDE):
                    t_ref[j, h, pl.ds(r0 * q, q), :] = s_ref[j, h, pl.ds(r0, q, stride=DEINT_STRIDE), :]
                for r0 in range(DEINT_STRIDE):
                    for r1 in range(d // DEINT_STRIDE):
                        o_ref[j, DEINT_STRIDE * r1 + r0, :, hs] = t_ref[
                            j, h, pl.ds(r0 * q + r1, tm // d, stride=d // DEINT_STRIDE), :].astype(o_ref.dtype)


def _qkv(h, w_in, layer, cos_t, sin_t, gi, d, B, *, tm=1024):
    T, D = h.shape
    L = T // B
    tiles = L // tm
    W = ATT_HEADS_PER_GROUP * ATT_HEAD_DIM
    assert d <= DEINT_STRIDE or d == DEINT_STRIDE * DEINT_STRIDE
    stage = (3, ATT_HEADS_PER_GROUP, tm, ATT_HEAD_DIM)
    w_spec = lambda j: pl.BlockSpec((None, D, SLAB), lambda i: (layer, 0, W_SLAB_Q + gi + 3 * j))
    return pl.pallas_call(
        functools.partial(_qkv_kernel, d=d),
        grid=(T // tm,),
        in_specs=[
            pl.BlockSpec((tm, D), lambda i: (i, 0)),
            w_spec(0), w_spec(1), w_spec(2),
            pl.BlockSpec((tm, ATT_HEAD_DIM), lambda i: (i, 0)),
            pl.BlockSpec((tm, ATT_HEAD_DIM), lambda i: (i, 0)),
        ],
        out_specs=pl.BlockSpec((3, None, d, tm // d, W), lambda i: (0, i // tiles, 0, i % tiles, 0)),
        out_shape=jax.ShapeDtypeStruct((3, B, d, L // d, W), BF16),
        scratch_shapes=[pltpu.VMEM((3, D, SLAB), BF16),
                        pltpu.VMEM(stage, F32),
                        pltpu.VMEM(stage if d > DEINT_STRIDE else (1, 1, 8, ATT_HEAD_DIM), F32)],
        compiler_params=_params(("arbitrary",)),
        name=f"qkv_proj_g{gi}",
    )(h, w_in, w_in, w_in, cos_t, sin_t)


ATT_UNROLL = 8


def _att_kernel(q_ref, kc_ref, kp_ref, vc_ref, vp_ref, o_ref, lse_ref, *, d, nq):
    Q = ATT_QBLOCK
    n = pl.program_id(1)
    qi = lax.broadcasted_iota(jnp.int32, (Q, 2 * Q), 0)
    kj = lax.broadcasted_iota(jnp.int32, (Q, 2 * Q), 1)
    rel = Q + qi - kj
    band = (rel >= 0) & (rel <= Q)
    band_first = band & ((n > 0) | (kj >= Q))
    head_of_lane = lax.broadcasted_iota(jnp.int32, (Q, ATT_HEAD_DIM), 1) // (ATT_HEAD_DIM // ATT_HEADS_PER_GROUP)
    nt = (((1,), (1,)), ((), ()))

    def block(r, i):
        rows = slice(i * Q, (i + 1) * Q)
        outs = []
        lse = jnp.zeros((Q, ATT_HEAD_DIM), F32)
        for h in range(ATT_HEADS_PER_GROUP):
            hs = slice(h * ATT_HEAD_DIM, (h + 1) * ATT_HEAD_DIM)
            if i == 0:
                k = jnp.concatenate([kp_ref[r, :, hs], kc_ref[r, rows, hs]], axis=0)
                v = jnp.concatenate([vp_ref[r, :, hs], vc_ref[r, rows, hs]], axis=0)
            else:
                k = kc_ref[r, (i - 1) * Q:(i + 1) * Q, hs]
                v = vc_ref[r, (i - 1) * Q:(i + 1) * Q, hs]
            s = lax.dot_general(q_ref[r, rows, hs], k, nt, preferred_element_type=F32) * (ATT_HEAD_DIM ** -0.5)
            s = jnp.where(band_first if i == 0 else band, s, MASK_VALUE)
            m = jnp.max(s, axis=-1, keepdims=True)
            p = jnp.exp(s - m)
            l = jnp.sum(p, axis=-1, keepdims=True)
            outs.append(jnp.dot(p.astype(BF16), v, preferred_element_type=F32) / l)
            lse = jnp.where(head_of_lane == h, m + jnp.log(l), lse)
        o_ref[r, rows, :] = jnp.concatenate(outs, axis=-1).astype(o_ref.dtype)
        lse_ref[r, rows, :] = lse

    if d <= ATT_UNROLL:
        for r in range(d):
            for i in range(nq):
                block(r, i)
    else:
        def residues(it, carry):
            for u in range(ATT_UNROLL):
                for i in range(nq):
                    block(it * ATT_UNROLL + u, i)
            return carry

        lax.fori_loop(0, d // ATT_UNROLL, residues, 0)


def _attention_group(qkv, gi):
    _, B, d, n_sub, W = qkv.shape
    Q = ATT_QBLOCK
    nq = max(1, ATT_UNROLL // d)

    def cur(s):
        return pl.BlockSpec((None, None, d, nq * Q, W), lambda b, n: (s, b, 0, n, 0))

    def prev(s):
        return pl.BlockSpec((None, None, d, Q, W), lambda b, n: (s, b, 0, jnp.maximum(nq * n - 1, 0), 0))

    return pl.pallas_call(
        functools.partial(_att_kernel, d=d, nq=nq),
        grid=(B, n_sub // (nq * Q)),
        in_specs=[cur(0), cur(1), prev(1), cur(2), prev(2)],
        out_specs=[pl.BlockSpec((None, d, nq * Q, W), lambda b, n: (b, 0, n, 0)),
                   pl.BlockSpec((None, d, nq * Q, ATT_HEAD_DIM), lambda b, n: (b, 0, n, 0))],
        out_shape=[jax.ShapeDtypeStruct((B, d, n_sub, W), BF16),
                   jax.ShapeDtypeStruct((B, d, n_sub, ATT_HEAD_DIM), F32)],
        compiler_params=_params(("parallel", "arbitrary")),
        name=f"dilated_attention_g{gi}",
    )(qkv, qkv, qkv, qkv, qkv)


def _merge_kernel(x_ref, ya_ref, yb_ref, yc_ref, o1_ref, o2_ref, o3_ref, l1_ref, l2_ref, l3_ref,
                  g0_ref, g1_ref, g2_ref, g3_ref, g4_ref, g5_ref, g6_ref, g7_ref, wb_ref, wo_ref, out_ref,
                  os_ref, ls_ref):
    for g, (o_ref, l_ref) in enumerate(((o1_ref, l1_ref), (o2_ref, l2_ref), (o3_ref, l3_ref))):
        d, rows = o_ref.shape[0], o_ref.shape[1]
        for r in range(d):
            o = o_ref[r].astype(F32)
            for h in range(ATT_HEADS_PER_GROUP):
                os_ref[g, h, pl.ds(r, rows, stride=d), :] = o[:, h * ATT_HEAD_DIM:(h + 1) * ATT_HEAD_DIM]
            ls_ref[g, pl.ds(r, rows, stride=d), :] = l_ref[r]
    lanes = ATT_HEAD_DIM // ATT_HEADS_PER_GROUP
    parts = []
    for h in range(ATT_HEADS_PER_GROUP):
        l1, l2, l3 = (ls_ref[g, :, h * lanes:h * lanes + 1] for g in range(3))
        m = jnp.maximum(jnp.maximum(l1, l2), l3)
        e1, e2, e3 = jnp.exp(l1 - m), jnp.exp(l2 - m), jnp.exp(l3 - m)
        parts.append((e1 * os_ref[0, h] + e2 * os_ref[1, h] + e3 * os_ref[2, h]) / (e1 + e2 + e3))
    yd = jnp.concatenate(parts, axis=-1)
    ys = (ya_ref[...], yb_ref[...], yc_ref[...], yd.astype(BF16))
    gates = ((g0_ref, g1_ref), (g2_ref, g3_ref), (g4_ref, g5_ref), (g6_ref, g7_ref))
    merged = None
    for k in range(N_BRANCHES):
        gate = _sigmoid(jnp.concatenate([gates[k][0][...], gates[k][1][...]], axis=-1).astype(F32))
        term = gate * jnp.dot(ys[k], wb_ref[k], preferred_element_type=F32)
        merged = term if merged is None else merged + term
    out_ref[...] = x_ref[...] + jnp.dot(merged.astype(BF16), wo_ref[...], preferred_element_type=F32)


def _merge(x, ya, yb, yc, att, proj, w_branch, w_out, *, tm=512):
    B, L, D = x.shape
    W = BRANCH_WIDTH
    (o1, l1), (o2, l2), (o3, l3) = att
    row = lambda: pl.BlockSpec((None, tm, W), lambda b, t: (b, t, 0))

    def res(a):
        d, last = a.shape[1], a.shape[3]
        return pl.BlockSpec((None, d, tm // d, last), lambda b, t: (b, 0, t, 0))

    return pl.pallas_call(
        _merge_kernel,
        grid=(B, L // tm),
        in_specs=[pl.BlockSpec((None, tm, D), lambda b, t: (b, t, 0))] + [row() for _ in range(3)]
                 + [res(a) for a in (o1, o2, o3, l1, l2, l3)]
                 + [_slab_spec(SLAB_GATE + s, tm) for s in range(8)]
                 + [pl.BlockSpec((N_BRANCHES, W, D), lambda b, t: (0, 0, 0)), pl.BlockSpec((D, D), lambda b, t: (0, 0))],
        out_specs=pl.BlockSpec((None, tm, D), lambda b, t: (b, t, 0)),
        out_shape=jax.ShapeDtypeStruct((B, L, D), F32),
        scratch_shapes=[pltpu.VMEM((3, ATT_HEADS_PER_GROUP, tm, ATT_HEAD_DIM), F32),
                        pltpu.VMEM((3, tm, ATT_HEAD_DIM), F32)],
        compiler_params=_params(("parallel", "parallel")),
        name="gated_merge",
    )(x, ya, yb, yc, o1, o2, o3, l1, l2, l3, *([proj] * 8), w_branch, w_out)


def kernel(x, positions, ffn1_norm, ffn1_w_gate, ffn1_w_up, ffn1_w_down, mix_norm, w_in, hg_lb_logits, hg_gnorm, s5_a_re, s5_a_im, s5_log_dt, s5_b_re, s5_b_im, s5_c_re, s5_c_im, s5_d, s5_w_glu, conv_w, conv_b, conv_ln_g, conv_ln_b, w_branch, w_out, ffn2_norm, ffn2_w_gate, ffn2_w_up, ffn2_w_down, final_norm):
    B, L, D = x.shape
    T = B * L
    depth = w_in.shape[0]
    lb_soft = jax.nn.softmax(hg_lb_logits.astype(F32), axis=0)
    lb_all = jnp.cumsum(lb_soft, axis=0) - lb_soft[0]
    cos_t, sin_t = _rope_tables(positions)
    bf = lambda w: w.astype(BF16)

    xt = x.reshape(T, D)
    for l in range(depth):
        xt, h_mix = _ffn(xt, ffn1_norm[l], bf(ffn1_w_gate[l]), bf(ffn1_w_up[l]), bf(ffn1_w_down[l]),
                         next_gain=mix_norm[l])
        proj = _proj(h_mix, w_in, l).reshape(N_SLABS, B, L, SLAB)
        ya = _hgrn(proj, lb_all[l], hg_gnorm[l])
        wb, wc, pj, pq = _s5_tables(s5_a_re[l], s5_a_im[l], s5_log_dt[l], s5_b_re[l], s5_b_im[l], s5_c_re[l], s5_c_im[l])
        yb = _s5(proj, wb, wc, pj, pq, s5_d[l], bf(s5_w_glu[l]))
        yc = _conv(proj, conv_w[l], conv_b[l], conv_ln_g[l], conv_ln_b[l])
        att = [_attention_group(_qkv(h_mix, w_in, l, cos_t, sin_t, gi, dil, B), gi)
               for gi, (_, dil) in enumerate(ATT_CONFIGS)]
        xt = _merge(xt.reshape(B, L, D), ya, yb, yc, att, proj, bf(w_branch[l]), bf(w_out[l])).reshape(T, D)
        last = l == depth - 1
        xt = _ffn(xt, ffn2_norm[l], bf(ffn2_w_gate[l]), bf(ffn2_w_up[l]), bf(ffn2_w_down[l]),
                  final_gain=final_norm if last else None)
    return xt.reshape(B, L, D)
```

```python
import functools
import math

import numpy as np
import jax
import jax.numpy as jnp
from jax import lax
from jax.experimental import pallas as pl
from jax.experimental.pallas import tpu as pltpu

F32 = jnp.float32
BF16 = jnp.bfloat16

NORM_EPS = 1e-6
MASK_VALUE = -1e30
D_MODEL = 1024
D_FF = 2816
N_BRANCHES = 4
BRANCH_WIDTH = 512

HG_HEADS = 4
HG_KDIM = 128
HG_VDIM = 128
HG_CHUNK = 64
HG_SUB = 16
HG_ACC_CHAINS = 4

S5_GROUP = 16
S5_GROUPS = 32
S5_STATE = 64
S5_HALF_CH = 256
S5_HALF_STATES = (S5_GROUPS // 2) * S5_STATE
S5_TILE = 256
S5_SEGS = 8
S5_PITCH = S5_TILE // S5_SEGS + 8
S5_SCAN_LANES = 512

CONV_WIDTH = 31
CONV_HALO = 32

ATT_HEAD_DIM = 128
ATT_CONFIGS = ((128, 1), (512, 4), (2048, 16))
ATT_HEADS_PER_GROUP = 4
ATT_HEADS = 12
ATT_QBLOCK = 128
ROPE_THETA = 500000.0
ROPE_DIM = 32
ROPE_HALF = 16

SLAB = 512
SLAB_HQ, SLAB_HF, SLAB_HI, SLAB_HG, SLAB_S5, SLAB_CONV_A, SLAB_CONV_B = 0, 1, 2, 3, 4, 5, 6
W_SLAB_Q = 7
W_SLAB_GATE = 16
SLABS_PER_STEP = 3
SLAB_GATE = 7
N_SLABS = 15

VMEM_LIMIT = 56 * 1024 * 1024


def _params(sem):
    return pltpu.CompilerParams(dimension_semantics=sem, vmem_limit_bytes=VMEM_LIMIT)


def _rms(x):
    return x * lax.rsqrt(jnp.mean(x * x, axis=-1, keepdims=True) + NORM_EPS)


def _sigmoid(x):
    return 0.5 * jnp.tanh(0.5 * x) + 0.5


def _ffn_kernel(x_ref, gain_ref, wg_ref, wu_ref, wd_ref, *rest, final, emit_h):
    rest = list(rest)
    fgain_ref = rest.pop(0) if final else None
    hgain_ref = rest.pop(0) if emit_h else None
    o_ref = rest.pop(0)
    x = x_ref[...]
    h = (_rms(x) * gain_ref[...]).astype(BF16)
    g = jnp.dot(h, wg_ref[...], preferred_element_type=F32)
    u = jnp.dot(h, wu_ref[...], preferred_element_type=F32)
    a = (g * _sigmoid(g) * u).astype(BF16)
    y = x + 0.5 * jnp.dot(a, wd_ref[...], preferred_element_type=F32)
    if final:
        y = _rms(y) * fgain_ref[...]
    o_ref[...] = y
    if emit_h:
        rest[0][...] = (_rms(y) * hgain_ref[...]).astype(BF16)


def _cast_kernel(w_ref, o_ref):
    o_ref[...] = w_ref[...].astype(o_ref.dtype)


def _to_bf16(w, *, rows=256):
    depth, R, C = w.shape
    spec = pl.BlockSpec((None, rows, C), lambda l, i: (l, i, 0))
    return pl.pallas_call(
        _cast_kernel,
        grid=(depth, R // rows),
        in_specs=[spec],
        out_specs=spec,
        out_shape=jax.ShapeDtypeStruct(w.shape, BF16),
        compiler_params=_params(("parallel", "parallel")),
        name="weights_to_bf16",
    )(w)


def _ffn(x, gain, wg, wu, wd, layer, *, final_gain=None, next_gain=None, tm=256):
    T, D = x.shape
    FF = wg.shape[2]
    final, emit_h = final_gain is not None, next_gain is not None
    vec = lambda: pl.BlockSpec((1, D), lambda i: (0, 0))
    row = lambda: pl.BlockSpec((tm, D), lambda i: (i, 0))
    in_specs = [row(), vec(), pl.BlockSpec((None, D, FF), lambda i: (layer, 0, 0)),
                pl.BlockSpec((None, D, FF), lambda i: (layer, 0, 0)), pl.BlockSpec((None, FF, D), lambda i: (layer, 0, 0))]
    args = [x, gain.reshape(1, D), wg, wu, wd]
    for extra in (final_gain, next_gain):
        if extra is not None:
            in_specs.append(vec())
            args.append(extra.reshape(1, D))
    out_specs, out_shape = [row()], [jax.ShapeDtypeStruct((T, D), F32)]
    if emit_h:
        out_specs.append(row())
        out_shape.append(jax.ShapeDtypeStruct((T, D), BF16))
    out = pl.pallas_call(
        functools.partial(_ffn_kernel, final=final, emit_h=emit_h),
        grid=(T // tm,),
        in_specs=in_specs,
        out_specs=out_specs,
        out_shape=out_shape,
        compiler_params=_params(("parallel",)),
        name="ffn_final" if final else "ffn",
    )(*args)
    return out if emit_h else out[0]


def _proj_kernel(h_ref, *refs):
    w_refs, o_ref = refs[:SLABS_PER_STEP], refs[SLABS_PER_STEP]
    h = h_ref[...]
    for s, w_ref in enumerate(w_refs):
        o_ref[s] = jnp.dot(h, w_ref[...].astype(BF16), preferred_element_type=F32).astype(o_ref.dtype)


def _proj(h, w_in, layer, *, tm=2048):
    T, D = h.shape

    def w_spec(k):
        def index(i, j):
            s = j * SLABS_PER_STEP + k
            return layer, 0, jnp.where(s < SLAB_GATE, s, s + (W_SLAB_GATE - SLAB_GATE))
        return pl.BlockSpec((None, D, SLAB), index)

    return pl.pallas_call(
        _proj_kernel,
        grid=(T // tm, N_SLABS // SLABS_PER_STEP),
        in_specs=[pl.BlockSpec((tm, D), lambda i, j: (i, 0))] + [w_spec(k) for k in range(SLABS_PER_STEP)],
        out_specs=pl.BlockSpec((SLABS_PER_STEP, tm, SLAB), lambda i, j: (j, i, 0)),
        out_shape=jax.ShapeDtypeStruct((N_SLABS, T, SLAB), BF16),
        compiler_params=_params(("parallel", "arbitrary")),
        name="in_proj",
    )(h, *([w_in] * SLABS_PER_STEP))


def _slab_spec(slab, rows):
    return pl.BlockSpec((None, None, rows, SLAB), lambda b, t: (slab, b, t, 0))


def _split3_bf16(x):
    hi = x.astype(BF16)
    r = x - hi.astype(F32)
    mid = r.astype(BF16)
    lo = (r - mid.astype(F32)).astype(BF16)
    return hi, mid, lo


def _hgrn_kernel(q_ref, f_ref, i_ref, g_ref, lb_ref, gn_ref, o_ref, st_ref, b_ref, k_ref, v_ref, *, n_chunks):
    C, S = HG_CHUNK, HG_SUB
    tt = n_chunks * C
    heads = [slice(h * HG_KDIM, (h + 1) * HG_KDIM) for h in range(HG_HEADS)]

    @pl.when(pl.program_id(1) == 0)
    def _():
        st_ref[...] = jnp.zeros_like(st_ref)

    lb = lb_ref[...]
    row = lax.broadcasted_iota(jnp.int32, (tt, tt), 0)
    col = lax.broadcasted_iota(jnp.int32, (tt, tt), 1)
    tri = ((col <= row) & (row // C == col // C)).astype(BF16)
    row8 = lax.broadcasted_iota(jnp.int32, (8, 1), 0)
    nt = (((1,), (1,)), ((), ()))
    tn = (((0,), (0,)), ((), ()))

    q = q_ref[...].astype(F32)
    f = f_ref[...].astype(F32)
    v = i_ref[...].astype(F32)
    v16 = v.astype(BF16)
    qf = q * _sigmoid(q) * (HG_KDIM ** -0.5)
    sf = _sigmoid(f)
    kf = (1.0 - lb) * (1.0 - sf)
    logf = jnp.log(lb + (1.0 - lb) * sf)
    b = sum(jnp.dot(tri, p, preferred_element_type=F32) for p in _split3_bf16(logf))
    for h, hs in enumerate(heads):
        b_ref[h] = b[:, hs]
        k_ref[h] = kf[:, hs]
        v_ref[h] = v[:, hs]

    state = [st_ref[h] for h in range(HG_HEADS)]
    mm = []
    for c in range(n_chunks):
        rows = slice(c * C, (c + 1) * C)
        bc, qc, kc, vc = b[rows], qf[rows], kf[rows], v16[rows]
        b_last = bc[C - 1:C, :]
        qd = (qc * jnp.exp(bc)).astype(BF16)
        kd_last = (kc * jnp.exp(b_last - bc)).astype(BF16)
        outs = []
        for h, hs in enumerate(heads):
            outs.append(lax.dot_general(qd[:, hs], state[h].astype(BF16), nt, preferred_element_type=F32))
            state[h] = jnp.exp(b_last[:, hs]) * state[h] + lax.dot_general(
                vc[:, hs], kd_last[:, hs], tn, preferred_element_type=F32)
        mm.append(outs)
    for h in range(HG_HEADS):
        st_ref[h] = state[h]

    zero = lambda n: jnp.zeros((n, HG_KDIM), F32)
    off = [jnp.zeros((tt, HG_VDIM), F32) for _ in range(HG_HEADS)]
    n = S
    while n < C:
        same_pair = (row // (2 * n)) == (col // (2 * n))
        for h, hs in enumerate(heads):
            qs, ks = [], []
            for lo in range(0, tt, 2 * n):
                mid, hi = lo + n, lo + 2 * n
                r = b_ref[h, pl.ds(mid - 1, n, stride=0), :]
                qs += [zero(n), qf[mid:hi, hs] * jnp.exp(b[mid:hi, hs] - r)]
                ks += [kf[lo:mid, hs] * jnp.exp(r - b[lo:mid, hs]), zero(n)]
            att = lax.dot_general(jnp.concatenate(qs, axis=0).astype(BF16), jnp.concatenate(ks, axis=0).astype(BF16),
                                  nt, preferred_element_type=F32)
            att = jnp.where(same_pair, att, 0.0).astype(BF16)
            off[h] = off[h] + jnp.dot(att, v16[:, hs], preferred_element_type=F32)
        n *= 2

    normed = []
    for h, hs in enumerate(heads):
        tiles = []
        for base in range(0, tt, S):
            for lo in range(0, S, 8):
                qq = qf[base + lo:base + lo + 8, hs]
                bb = b[base + lo:base + lo + 8, hs]
                acc = [jnp.zeros((8, HG_VDIM), F32) for _ in range(HG_ACC_CHAINS)]
                for j in range(lo + 8):
                    key = pl.ds(base + j, 8, stride=0)
                    p = qq * k_ref[h, key, :] * jnp.exp(jnp.minimum(bb - b_ref[h, key, :], 0.0))
                    a = jnp.sum(p, axis=-1, keepdims=True)
                    if j >= lo:
                        a = jnp.where(row8 >= j - lo, a, 0.0)
                    acc[j % HG_ACC_CHAINS] = acc[j % HG_ACC_CHAINS] + a * v_ref[h, key, :]
                tiles.append((acc[0] + acc[1]) + (acc[2] + acc[3]))
        o = jnp.concatenate(tiles, axis=0) + (jnp.concatenate([mm[c][h] for c in range(n_chunks)], axis=0) + off[h])
        normed.append(o * lax.rsqrt(jnp.mean(o * o, axis=-1, keepdims=True) + NORM_EPS))
    g = g_ref[...].astype(F32)
    o_ref[...] = (jnp.concatenate(normed, axis=-1) * gn_ref[...] * (g * _sigmoid(g))).astype(o_ref.dtype)


def _hgrn(proj, lb, gnorm, *, tt=256):
    _, B, L, _ = proj.shape
    W = HG_HEADS * HG_KDIM
    return pl.pallas_call(
        functools.partial(_hgrn_kernel, n_chunks=tt // HG_CHUNK),
        grid=(B, L // tt),
        in_specs=[
            _slab_spec(SLAB_HQ, tt), _slab_spec(SLAB_HF, tt), _slab_spec(SLAB_HI, tt), _slab_spec(SLAB_HG, tt),
            pl.BlockSpec((1, W), lambda b, t: (0, 0)),
            pl.BlockSpec((1, W), lambda b, t: (0, 0)),
        ],
        out_specs=pl.BlockSpec((None, tt, W), lambda b, t: (b, t, 0)),
        out_shape=jax.ShapeDtypeStruct((B, L, W), BF16),
        scratch_shapes=[pltpu.VMEM((HG_HEADS, HG_VDIM, HG_KDIM), F32)] + [pltpu.VMEM((HG_HEADS, tt, HG_KDIM), F32)] * 3,
        compiler_params=_params(("parallel", "arbitrary")),
        name="hgrn2",
    )(proj, proj, proj, proj, lb.reshape(1, W), gnorm.reshape(1, W))


def _cmul(ar, ai, br, bi):
    return ar * br - ai * bi, ar * bi + ai * br


def _s5_kernel(u_ref, wb_ref, wc_ref, pj_ref, pq_ref, d_ref, wglu_ref, o_ref, x_ref, rq_ref, carry_ref, *, tt):
    SEGS, SR, PITCH, NS, LC = S5_SEGS, tt // S5_SEGS, S5_PITCH, S5_HALF_STATES, S5_SCAN_LANES
    row8 = lax.broadcasted_iota(jnp.int32, (SEGS, LC), 0)
    lane_tiles = range(BRANCH_WIDTH // 128)

    @pl.when(pl.program_id(1) == 0)
    def _():
        carry_ref[...] = jnp.zeros_like(carry_ref)

    u = u_ref[...].astype(F32)
    for lt in lane_tiles:
        for s in range(SEGS):
            rq_ref[lt, pl.ds(PITCH * s, SR), :] = u[SR * s:SR * (s + 1), lt * 128:(lt + 1) * 128]
    u = jnp.concatenate(
        [jnp.concatenate([rq_ref[lt, pl.ds(j, SEGS, stride=PITCH), :] for lt in lane_tiles], axis=-1)
         for j in range(SR)], axis=0)
    u16 = u.astype(BF16)

    for hf in range(2):
        x_ref[hf] = jnp.dot(u16[:, hf * S5_HALF_CH:(hf + 1) * S5_HALF_CH], wb_ref[hf],
                            preferred_element_type=F32)

    for hf in range(2):
        for lc in range(NS // LC):
            re = slice(lc * LC, (lc + 1) * LC)
            im = slice(NS + lc * LC, NS + (lc + 1) * LC)
            a1r, a1i = pj_ref[hf, 0:SEGS, re], pj_ref[hf, 0:SEGS, im]

            local = [(x_ref[hf, 0:SEGS, re], x_ref[hf, 0:SEGS, im])]
            for j in range(1, SR):
                xr, xi = _cmul(a1r, a1i, *local[-1])
                local.append((xr + x_ref[hf, j * SEGS:(j + 1) * SEGS, re], xi + x_ref[hf, j * SEGS:(j + 1) * SEGS, im]))
            er, ei = local[-1]

            for shift in (1, 2, 4):
                cr = jnp.where(row8 >= shift, pq_ref[hf, shift:shift + 1, re], 0.0)
                ci = jnp.where(row8 >= shift, pq_ref[hf, shift:shift + 1, im], 0.0)
                dr, di = _cmul(cr, ci, pltpu.roll(er, shift, 0), pltpu.roll(ei, shift, 0))
                er, ei = er + dr, ei + di
            c0r = carry_ref[hf, SEGS - 1:SEGS, re]
            c0i = carry_ref[hf, SEGS - 1:SEGS, im]
            sr, si = _cmul(pq_ref[hf, 0:SEGS, re], pq_ref[hf, 0:SEGS, im], c0r, c0i)
            sr = sr + jnp.where(row8 >= 1, pltpu.roll(er, 1, 0), 0.0)
            si = si + jnp.where(row8 >= 1, pltpu.roll(ei, 1, 0), 0.0)
            nr, ni = _cmul(pq_ref[hf, SEGS:2 * SEGS, re], pq_ref[hf, SEGS:2 * SEGS, im], c0r, c0i)
            carry_ref[hf, :, re] = er + nr
            carry_ref[hf, :, im] = ei + ni

            for j in range(SR):
                rows = slice(j * SEGS, (j + 1) * SEGS)
                dr, di = _cmul(pj_ref[hf, rows, re], pj_ref[hf, rows, im], sr, si)
                x_ref[hf, rows, re] = local[j][0] + dr
                x_ref[hf, rows, im] = local[j][1] + di

    y = jnp.concatenate(
        [jnp.dot(x_ref[hf].astype(BF16), wc_ref[hf], preferred_element_type=F32) for hf in range(2)], axis=-1)
    y = y + d_ref[...] * u
    z = jax.nn.gelu(y).astype(BF16)
    zz = jnp.dot(z, wglu_ref[...], preferred_element_type=F32)
    out = zz[:, :BRANCH_WIDTH] * _sigmoid(zz[:, BRANCH_WIDTH:])
    for lt in lane_tiles:
        for j in range(SR):
            rq_ref[lt, pl.ds(j, SEGS, stride=PITCH), :] = out[SEGS * j:SEGS * (j + 1), lt * 128:(lt + 1) * 128]
    for lt in lane_tiles:
        for s in range(SEGS):
            o_ref[SR * s:SR * (s + 1), lt * 128:(lt + 1) * 128] = rq_ref[lt, pl.ds(PITCH * s, SR), :].astype(o_ref.dtype)


def _s5(proj, wb, wc, pj, pq, d_skip, w_glu, *, tt=S5_TILE):
    _, B, L, _ = proj.shape
    W = BRANCH_WIDTH
    NS2 = 2 * S5_HALF_STATES
    return pl.pallas_call(
        functools.partial(_s5_kernel, tt=tt),
        grid=(B, L // tt),
        in_specs=[
            _slab_spec(SLAB_S5, tt),
            pl.BlockSpec((2, S5_HALF_CH, NS2), lambda b, t: (0, 0, 0)),
            pl.BlockSpec((2, NS2, S5_HALF_CH), lambda b, t: (0, 0, 0)),
            pl.BlockSpec((2, tt, NS2), lambda b, t: (0, 0, 0)),
            pl.BlockSpec((2, 2 * S5_SEGS, NS2), lambda b, t: (0, 0, 0)),
            pl.BlockSpec((1, W), lambda b, t: (0, 0)),
            pl.BlockSpec((W, 2 * W), lambda b, t: (0, 0)),
        ],
        out_specs=pl.BlockSpec((None, tt, W), lambda b, t: (b, t, 0)),
        out_shape=jax.ShapeDtypeStruct((B, L, W), BF16),
        scratch_shapes=[pltpu.VMEM((2, tt, NS2), F32),
                        pltpu.VMEM((W // 128, S5_SEGS * S5_PITCH, 128), F32),
                        pltpu.VMEM((2, S5_SEGS, NS2), F32)],
        compiler_params=_params(("parallel", "arbitrary")),
        name="s5",
    )(proj, wb, wc, pj, pq, d_skip.reshape(1, W), w_glu)


def _s5_tables(a_re, a_im, log_dt, b_re, b_im, c_re, c_im):
    dt = jnp.exp(log_dt)[:, None]
    mag = jnp.exp(a_re * dt)
    ab_re = mag * jnp.cos(a_im * dt)
    ab_im = mag * jnp.sin(a_im * dt)
    den = a_re * a_re + a_im * a_im
    zr = ((ab_re - 1.0) * a_re + ab_im * a_im) / den
    zi = (ab_im * a_re - (ab_re - 1.0) * a_im) / den
    bb_re = zr[..., None] * b_re - zi[..., None] * b_im
    bb_im = zr[..., None] * b_im + zi[..., None] * b_re
    GH = S5_GROUPS // 2
    eye = jnp.eye(GH, dtype=F32)

    def bdiag_in(m):
        m = m.reshape(2, GH, S5_STATE, S5_GROUP)
        return jnp.einsum('hgpc,gk->hgckp', m, eye).reshape(2, GH * S5_GROUP, GH * S5_STATE)

    def bdiag_out(m):
        m = m.reshape(2, GH, S5_GROUP, S5_STATE)
        return jnp.einsum('hgcp,gk->hgpkc', m, eye).reshape(2, GH * S5_STATE, GH * S5_GROUP)

    wb = jnp.concatenate([bdiag_in(bb_re), bdiag_in(bb_im)], axis=-1).astype(BF16)
    wc = jnp.concatenate([bdiag_out(c_re), -bdiag_out(c_im)], axis=1).astype(BF16)
    sr = S5_TILE // S5_SEGS

    def powers(n):
        n = jnp.asarray(n, F32)[:, None, None]
        mag = jnp.exp(n * (a_re * dt)[None])
        ang = n * (a_im * dt)[None]
        split = lambda p: p.reshape(p.shape[0], 2, S5_HALF_STATES).transpose(1, 0, 2)
        return jnp.concatenate([split(mag * jnp.cos(ang)), split(mag * jnp.sin(ang))], axis=-1)

    pj = powers(np.repeat(np.arange(1, sr + 1), S5_SEGS))
    pq = powers(np.concatenate([sr * np.arange(S5_SEGS), sr * np.arange(1, S5_SEGS + 1)]))
    return wb, wc, pj, pq


def _conv_kernel(a_ref, b_ref, w_ref, cb_ref, lg_ref, lbias_ref, o_ref, z_ref, y_ref, *, tt, rb):
    H = CONV_HALO

    @pl.when(pl.program_id(1) == 0)
    def _():
        z_ref[0:H, :] = jnp.zeros((H, BRANCH_WIDTH), F32)

    z_ref[H:H + tt, :] = a_ref[...].astype(F32) * _sigmoid(b_ref[...].astype(F32))
    cb = cb_ref[...]
    lg = lg_ref[...]
    lbias = lbias_ref[...]

    def block(k, carry):
        r0 = pl.multiple_of(k * rb, rb)
        for lc in range(0, BRANCH_WIDTH, 128):
            win = z_ref[pl.ds(r0, rb + H), lc:lc + 128]
            acc = [jnp.zeros((rb, 128), F32), jnp.zeros((rb, 128), F32)]
            for c in range(8):
                shifted = win if c == 0 else pltpu.roll(win, rb + H - c, 0)
                for off in range(c, H + 1, 8):
                    w = off - (H - CONV_WIDTH + 1)
                    if 0 <= w < CONV_WIDTH:
                        acc[w % 2] = acc[w % 2] + shifted[off - c:off - c + rb, :] * w_ref[w:w + 1, lc:lc + 128]
            y_ref[pl.ds(r0, rb), lc:lc + 128] = acc[0] + acc[1]
        return carry

    lax.fori_loop(0, tt // rb, block, 0)
    z_ref[0:H, :] = z_ref[tt:tt + H, :]

    acc = y_ref[...] + cb
    xc = acc - jnp.mean(acc, axis=-1, keepdims=True)
    var = jnp.mean(xc * xc, axis=-1, keepdims=True)
    y = xc * lax.rsqrt(var + NORM_EPS) * lg + lbias
    o_ref[...] = (y * _sigmoid(y)).astype(o_ref.dtype)


def _conv(proj, conv_w, conv_b, ln_g, ln_b, *, tt=512, rb=32):
    _, B, L, _ = proj.shape
    W = BRANCH_WIDTH
    w_pad = jnp.zeros((CONV_HALO, W), F32).at[:CONV_WIDTH].set(conv_w)
    vec = lambda: pl.BlockSpec((1, W), lambda b, t: (0, 0))
    return pl.pallas_call(
        functools.partial(_conv_kernel, tt=tt, rb=rb),
        grid=(B, L // tt),
        in_specs=[_slab_spec(SLAB_CONV_A, tt), _slab_spec(SLAB_CONV_B, tt),
                  pl.BlockSpec((CONV_HALO, W), lambda b, t: (0, 0)), vec(), vec(), vec()],
        out_specs=pl.BlockSpec((None, tt, W), lambda b, t: (b, t, 0)),
        out_shape=jax.ShapeDtypeStruct((B, L, W), BF16),
        scratch_shapes=[pltpu.VMEM((tt + CONV_HALO, W), F32), pltpu.VMEM((tt, W), F32)],
        compiler_params=_params(("parallel", "arbitrary")),
        name="conformer_conv",
    )(proj, proj, w_pad, conv_b.reshape(1, W), ln_g.reshape(1, W), ln_b.reshape(1, W))


def _rope_table_kernel(pos_ref, inv_ref, cos_ref, sin_ref):
    ang = pos_ref[...] * inv_ref[...]
    lane = lax.broadcasted_iota(jnp.int32, ang.shape, 1)
    cos_ref[...] = jnp.cos(ang)
    sin_ref[...] = jnp.where(lane < ROPE_HALF, -jnp.sin(ang), jnp.sin(ang))


def _rope_tables(positions, *, tm=1024):
    T = positions.size
    inv = ROPE_THETA ** (-jnp.arange(ROPE_HALF, dtype=F32) / ROPE_HALF)
    inv = jnp.concatenate([inv, inv, jnp.zeros((ATT_HEAD_DIM - ROPE_DIM,), F32)]).reshape(1, ATT_HEAD_DIM)
    pos = positions.astype(F32).reshape(T, 1)
    return pl.pallas_call(
        _rope_table_kernel,
        grid=(T // tm,),
        in_specs=[pl.BlockSpec((tm, 1), lambda i: (i, 0)), pl.BlockSpec((1, ATT_HEAD_DIM), lambda i: (0, 0))],
        out_specs=[pl.BlockSpec((tm, ATT_HEAD_DIM), lambda i: (i, 0))] * 2,
        out_shape=[jax.ShapeDtypeStruct((T, ATT_HEAD_DIM), F32)] * 2,
        compiler_params=_params(("parallel",)),
        name="rope_tables",
    )(pos, inv)


def _rope(x, cos, sin):
    lane = lax.broadcasted_iota(jnp.int32, x.shape, 1)
    partner = jnp.where(lane < ROPE_HALF, pltpu.roll(x, ATT_HEAD_DIM - ROPE_HALF, 1), pltpu.roll(x, ROPE_HALF, 1))
    return x * cos + partner * sin


DEINT_STRIDE = 4


def _qkv_kernel(h_ref, wq_ref, wk_ref, wv_ref, cos_ref, sin_ref, o_ref, w16_ref, s_ref, t_ref, *, d):
    tm = h_ref.shape[0]

    @pl.when(pl.program_id(0) == 0)
    def _():
        for j, w_ref in enumerate((wq_ref, wk_ref, wv_ref)):
            w16_ref[j] = w_ref[...].astype(BF16)

    h16 = h_ref[...]
    cos, sin = cos_ref[...], sin_ref[...]
    for j in range(3):
        res = jnp.dot(h16, w16_ref[j], preferred_element_type=F32)
        for h in range(ATT_HEADS_PER_GROUP):
            hs = slice(h * ATT_HEAD_DIM, (h + 1) * ATT_HEAD_DIM)
            s_ref[j, h] = _rope(res[:, hs], cos, sin) if j < 2 else res[:, hs]
            if d <= DEINT_STRIDE:
                for r in range(d):
                    o_ref[j, r, :, hs] = s_ref[j, h, pl.ds(r, tm // d, stride=d), :].astype(o_ref.dtype)
            else:
                q = tm // DEINT_STRIDE
                for r0 in range(DEINT_STRIDE):
                    t_ref[j, h, pl.ds(r0 * q, q), :] = s_ref[j, h, pl.ds(r0, q, stride=DEINT_STRIDE), :]
                for r0 in range(DEINT_STRIDE):
                    for r1 in range(d // DEINT_STRIDE):
                        o_ref[j, DEINT_STRIDE * r1 + r0, :, hs] = t_ref[
                            j, h, pl.ds(r0 * q + r1, tm // d, stride=d // DEINT_STRIDE), :].astype(o_ref.dtype)


def _qkv(h, w_in, layer, cos_t, sin_t, gi, d, B, *, tm=1024):
    T, D = h.shape
    L = T // B
    tiles = L // tm
    W = ATT_HEADS_PER_GROUP * ATT_HEAD_DIM
    assert d <= DEINT_STRIDE or d == DEINT_STRIDE * DEINT_STRIDE
    stage = (3, ATT_HEADS_PER_GROUP, tm, ATT_HEAD_DIM)
    w_spec = lambda j: pl.BlockSpec((None, D, SLAB), lambda i: (layer, 0, W_SLAB_Q + gi + 3 * j))
    return pl.pallas_call(
        functools.partial(_qkv_kernel, d=d),
        grid=(T // tm,),
        in_specs=[
            pl.BlockSpec((tm, D), lambda i: (i, 0)),
            w_spec(0), w_spec(1), w_spec(2),
            pl.BlockSpec((tm, ATT_HEAD_DIM), lambda i: (i, 0)),
            pl.BlockSpec((tm, ATT_HEAD_DIM), lambda i: (i, 0)),
        ],
        out_specs=pl.BlockSpec((3, None, d, tm // d, W), lambda i: (0, i // tiles, 0, i % tiles, 0)),
        out_shape=jax.ShapeDtypeStruct((3, B, d, L // d, W), BF16),
        scratch_shapes=[pltpu.VMEM((3, D, SLAB), BF16),
                        pltpu.VMEM(stage, F32),
                        pltpu.VMEM(stage if d > DEINT_STRIDE else (1, 1, 8, ATT_HEAD_DIM), F32)],
        compiler_params=_params(("arbitrary",)),
        name=f"qkv_proj_g{gi}",
    )(h, w_in, w_in, w_in, cos_t, sin_t)


ATT_UNROLL = 8


def _att_kernel(q_ref, kc_ref, kp_ref, vc_ref, vp_ref, o_ref, lse_ref, *, d, nq):
    Q = ATT_QBLOCK
    n = pl.program_id(1)
    qi = lax.broadcasted_iota(jnp.int32, (Q, 2 * Q), 0)
    kj = lax.broadcasted_iota(jnp.int32, (Q, 2 * Q), 1)
    rel = Q + qi - kj
    band = (rel >= 0) & (rel <= Q)
    band_first = band & ((n > 0) | (kj >= Q))
    head_of_lane = lax.broadcasted_iota(jnp.int32, (Q, ATT_HEAD_DIM), 1) // (ATT_HEAD_DIM // ATT_HEADS_PER_GROUP)
    nt = (((1,), (1,)), ((), ()))

    def block(r, i):
        rows = slice(i * Q, (i + 1) * Q)
        outs = []
        lse = jnp.zeros((Q, ATT_HEAD_DIM), F32)
        for h in range(ATT_HEADS_PER_GROUP):
            hs = slice(h * ATT_HEAD_DIM, (h + 1) * ATT_HEAD_DIM)
            if i == 0:
                k = jnp.concatenate([kp_ref[r, :, hs], kc_ref[r, rows, hs]], axis=0)
                v = jnp.concatenate([vp_ref[r, :, hs], vc_ref[r, rows, hs]], axis=0)
            else:
                k = kc_ref[r, (i - 1) * Q:(i + 1) * Q, hs]
                v = vc_ref[r, (i - 1) * Q:(i + 1) * Q, hs]
            s = lax.dot_general(q_ref[r, rows, hs], k, nt, preferred_element_type=F32) * (ATT_HEAD_DIM ** -0.5)
            s = jnp.where(band_first if i == 0 else band, s, MASK_VALUE)
            m = jnp.max(s, axis=-1, keepdims=True)
            p = jnp.exp(s - m)
            l = jnp.sum(p, axis=-1, keepdims=True)
            outs.append(jnp.dot(p.astype(BF16), v, preferred_element_type=F32) / l)
            lse = jnp.where(head_of_lane == h, m + jnp.log(l), lse)
        o_ref[r, rows, :] = jnp.concatenate(outs, axis=-1).astype(o_ref.dtype)
        lse_ref[r, rows, :] = lse

    if d <= ATT_UNROLL:
        for r in range(d):
            for i in range(nq):
                block(r, i)
    else:
        def residues(it, carry):
            for u in range(ATT_UNROLL):
                for i in range(nq):
                    block(it * ATT_UNROLL + u, i)
            return carry

        lax.fori_loop(0, d // ATT_UNROLL, residues, 0)


def _attention_group(qkv, gi):
    _, B, d, n_sub, W = qkv.shape
    Q = ATT_QBLOCK
    nq = max(1, ATT_UNROLL // d)

    def cur(s):
        return pl.BlockSpec((None, None, d, nq * Q, W), lambda b, n: (s, b, 0, n, 0))

    def prev(s):
        return pl.BlockSpec((None, None, d, Q, W), lambda b, n: (s, b, 0, jnp.maximum(nq * n - 1, 0), 0))

    return pl.pallas_call(
        functools.partial(_att_kernel, d=d, nq=nq),
        grid=(B, n_sub // (nq * Q)),
        in_specs=[cur(0), cur(1), prev(1), cur(2), prev(2)],
        out_specs=[pl.BlockSpec((None, d, nq * Q, W), lambda b, n: (b, 0, n, 0)),
                   pl.BlockSpec((None, d, nq * Q, ATT_HEAD_DIM), lambda b, n: (b, 0, n, 0))],
        out_shape=[jax.ShapeDtypeStruct((B, d, n_sub, W), BF16),
                   jax.ShapeDtypeStruct((B, d, n_sub, ATT_HEAD_DIM), F32)],
        compiler_params=_params(("parallel", "arbitrary")),
        name=f"dilated_attention_g{gi}",
    )(qkv, qkv, qkv, qkv, qkv)


def _merge_kernel(x_ref, ya_ref, yb_ref, yc_ref, o1_ref, o2_ref, o3_ref, l1_ref, l2_ref, l3_ref,
                  g0_ref, g1_ref, g2_ref, g3_ref, g4_ref, g5_ref, g6_ref, g7_ref, wb_ref, wo_ref, out_ref,
                  os_ref, ls_ref):
    for g, (o_ref, l_ref) in enumerate(((o1_ref, l1_ref), (o2_ref, l2_ref), (o3_ref, l3_ref))):
        d, rows = o_ref.shape[0], o_ref.shape[1]
        for r in range(d):
            o = o_ref[r].astype(F32)
            for h in range(ATT_HEADS_PER_GROUP):
                os_ref[g, h, pl.ds(r, rows, stride=d), :] = o[:, h * ATT_HEAD_DIM:(h + 1) * ATT_HEAD_DIM]
            ls_ref[g, pl.ds(r, rows, stride=d), :] = l_ref[r]
    lanes = ATT_HEAD_DIM // ATT_HEADS_PER_GROUP
    parts = []
    for h in range(ATT_HEADS_PER_GROUP):
        l1, l2, l3 = (ls_ref[g, :, h * lanes:h * lanes + 1] for g in range(3))
        m = jnp.maximum(jnp.maximum(l1, l2), l3)
        e1, e2, e3 = jnp.exp(l1 - m), jnp.exp(l2 - m), jnp.exp(l3 - m)
        parts.append((e1 * os_ref[0, h] + e2 * os_ref[1, h] + e3 * os_ref[2, h]) / (e1 + e2 + e3))
    yd = jnp.concatenate(parts, axis=-1)
    ys = (ya_ref[...], yb_ref[...], yc_ref[...], yd.astype(BF16))
    gates = ((g0_ref, g1_ref), (g2_ref, g3_ref), (g4_ref, g5_ref), (g6_ref, g7_ref))
    merged = None
    for k in range(N_BRANCHES):
        gate = _sigmoid(jnp.concatenate([gates[k][0][...], gates[k][1][...]], axis=-1).astype(F32))
        term = gate * jnp.dot(ys[k], wb_ref[k], preferred_element_type=F32)
        merged = term if merged is None else merged + term
    out_ref[...] = x_ref[...] + jnp.dot(merged.astype(BF16), wo_ref[...], preferred_element_type=F32)


def _merge(x, ya, yb, yc, att, proj, w_branch, w_out, *, tm=512):
    B, L, D = x.shape
    W = BRANCH_WIDTH
    (o1, l1), (o2, l2), (o3, l3) = att
    row = lambda: pl.BlockSpec((None, tm, W), lambda b, t: (b, t, 0))

    def res(a):
        d, last = a.shape[1], a.shape[3]
        return pl.BlockSpec((None, d, tm // d, last), lambda b, t: (b, 0, t, 0))

    return pl.pallas_call(
        _merge_kernel,
        grid=(B, L // tm),
        in_specs=[pl.BlockSpec((None, tm, D), lambda b, t: (b, t, 0))] + [row() for _ in range(3)]
                 + [res(a) for a in (o1, o2, o3, l1, l2, l3)]
                 + [_slab_spec(SLAB_GATE + s, tm) for s in range(8)]
                 + [pl.BlockSpec((N_BRANCHES, W, D), lambda b, t: (0, 0, 0)), pl.BlockSpec((D, D), lambda b, t: (0, 0))],
        out_specs=pl.BlockSpec((None, tm, D), lambda b, t: (b, t, 0)),
        out_shape=jax.ShapeDtypeStruct((B, L, D), F32),
        scratch_shapes=[pltpu.VMEM((3, ATT_HEADS_PER_GROUP, tm, ATT_HEAD_DIM), F32),
                        pltpu.VMEM((3, tm, ATT_HEAD_DIM), F32)],
        compiler_params=_params(("parallel", "parallel")),
        name="gated_merge",
    )(x, ya, yb, yc, o1, o2, o3, l1, l2, l3, *([proj] * 8), w_branch, w_out)


def kernel(x, positions, ffn1_norm, ffn1_w_gate, ffn1_w_up, ffn1_w_down, mix_norm, w_in, hg_lb_logits, hg_gnorm, s5_a_re, s5_a_im, s5_log_dt, s5_b_re, s5_b_im, s5_c_re, s5_c_im, s5_d, s5_w_glu, conv_w, conv_b, conv_ln_g, conv_ln_b, w_branch, w_out, ffn2_norm, ffn2_w_gate, ffn2_w_up, ffn2_w_down, final_norm):
    B, L, D = x.shape
    T = B * L
    depth = w_in.shape[0]
    lb_soft = jax.nn.softmax(hg_lb_logits.astype(F32), axis=0)
    lb_all = jnp.cumsum(lb_soft, axis=0) - lb_soft[0]
    cos_t, sin_t = _rope_tables(positions)
    bf = lambda w: w.astype(BF16)

    ffn1_w = [_to_bf16(w) for w in (ffn1_w_gate, ffn1_w_up, ffn1_w_down)]
    ffn2_w = [_to_bf16(w) for w in (ffn2_w_gate, ffn2_w_up, ffn2_w_down)]

    xt = x.reshape(T, D)
    for l in range(depth):
        xt, h_mix = _ffn(xt, ffn1_norm[l], *ffn1_w, l, next_gain=mix_norm[l])
        proj = _proj(h_mix, w_in, l).reshape(N_SLABS, B, L, SLAB)
        ya = _hgrn(proj, lb_all[l], hg_gnorm[l])
        wb, wc, pj, pq = _s5_tables(s5_a_re[l], s5_a_im[l], s5_log_dt[l], s5_b_re[l], s5_b_im[l], s5_c_re[l], s5_c_im[l])
        yb = _s5(proj, wb, wc, pj, pq, s5_d[l], bf(s5_w_glu[l]))
        yc = _conv(proj, conv_w[l], conv_b[l], conv_ln_g[l], conv_ln_b[l])
        att = [_attention_group(_qkv(h_mix, w_in, l, cos_t, sin_t, gi, dil, B), gi)
               for gi, (_, dil) in enumerate(ATT_CONFIGS)]
        xt = _merge(xt.reshape(B, L, D), ya, yb, yc, att, proj, bf(w_branch[l]), bf(w_out[l])).reshape(T, D)
        last = l == depth - 1
        xt = _ffn(xt, ffn2_norm[l], *ffn2_w, l, final_gain=final_norm if last else None)
    return xt.reshape(B, L, D)
```

```python
import functools
import math

import numpy as np
import jax
import jax.numpy as jnp
from jax import lax
from jax.experimental import pallas as pl
from jax.experimental.pallas import tpu as pltpu

F32 = jnp.float32
BF16 = jnp.bfloat16

NORM_EPS = 1e-6
MASK_VALUE = -1e30
LOG2E = math.log2(math.e)
LN2 = math.log(2.0)
D_MODEL = 1024
D_FF = 2816
N_BRANCHES = 4
BRANCH_WIDTH = 512

HG_HEADS = 4
HG_KDIM = 128
HG_VDIM = 128
HG_CHUNK = 64
HG_SUB = 16
HG_ACC_CHAINS = 4

S5_GROUP = 16
S5_GROUPS = 32
S5_STATE = 64
S5_HALF_CH = 256
S5_HALF_STATES = (S5_GROUPS // 2) * S5_STATE
S5_TILE = 256
S5_SEGS = 8
S5_PITCH = S5_TILE // S5_SEGS + 8
S5_SCAN_LANES = 512

CONV_WIDTH = 31
CONV_HALO = 32

ATT_HEAD_DIM = 128
ATT_CONFIGS = ((128, 1), (512, 4), (2048, 16))
ATT_HEADS_PER_GROUP = 4
ATT_HEADS = 12
ATT_QBLOCK = 128
ROPE_THETA = 500000.0
ROPE_DIM = 32
ROPE_HALF = 16

SLAB = 512
SLAB_HQ, SLAB_HF, SLAB_HI, SLAB_HG, SLAB_S5, SLAB_CONV_A, SLAB_CONV_B = 0, 1, 2, 3, 4, 5, 6
W_SLAB_Q = 7
W_SLAB_GATE = 16
SLABS_PER_STEP = 3
SLAB_GATE = 7
N_SLABS = 15

VMEM_LIMIT = 56 * 1024 * 1024


def _params(sem):
    return pltpu.CompilerParams(dimension_semantics=sem, vmem_limit_bytes=VMEM_LIMIT)


def _rms(x):
    return x * lax.rsqrt(jnp.mean(x * x, axis=-1, keepdims=True) + NORM_EPS)


def _sigmoid(x):
    return 0.5 * jnp.tanh(0.5 * x) + 0.5


def _ffn_kernel(x_ref, gain_ref, wg_ref, wu_ref, wd_ref, *rest, final, emit_h):
    rest = list(rest)
    fgain_ref = rest.pop(0) if final else None
    hgain_ref = rest.pop(0) if emit_h else None
    o_ref = rest.pop(0)
    x = x_ref[...]
    h = (_rms(x) * gain_ref[...]).astype(BF16)
    g = jnp.dot(h, wg_ref[...], preferred_element_type=F32)
    u = jnp.dot(h, wu_ref[...], preferred_element_type=F32)
    a = (g * _sigmoid(g) * u).astype(BF16)
    y = x + 0.5 * jnp.dot(a, wd_ref[...], preferred_element_type=F32)
    if final:
        y = _rms(y) * fgain_ref[...]
    o_ref[...] = y
    if emit_h:
        rest[0][...] = (_rms(y) * hgain_ref[...]).astype(BF16)


def _cast_kernel(w_ref, o_ref):
    o_ref[...] = w_ref[...].astype(o_ref.dtype)


def _to_bf16(w, *, rows=256):
    depth, R, C = w.shape
    spec = pl.BlockSpec((None, rows, C), lambda l, i: (l, i, 0))
    return pl.pallas_call(
        _cast_kernel,
        grid=(depth, R // rows),
        in_specs=[spec],
        out_specs=spec,
        out_shape=jax.ShapeDtypeStruct(w.shape, BF16),
        compiler_params=_params(("parallel", "parallel")),
        name="weights_to_bf16",
    )(w)


def _ffn(x, gain, wg, wu, wd, layer, *, final_gain=None, next_gain=None, tm=256):
    T, D = x.shape
    FF = wg.shape[2]
    final, emit_h = final_gain is not None, next_gain is not None
    vec = lambda: pl.BlockSpec((1, D), lambda i: (0, 0))
    row = lambda: pl.BlockSpec((tm, D), lambda i: (i, 0))
    in_specs = [row(), vec(), pl.BlockSpec((None, D, FF), lambda i: (layer, 0, 0)),
                pl.BlockSpec((None, D, FF), lambda i: (layer, 0, 0)), pl.BlockSpec((None, FF, D), lambda i: (layer, 0, 0))]
    args = [x, gain.reshape(1, D), wg, wu, wd]
    for extra in (final_gain, next_gain):
        if extra is not None:
            in_specs.append(vec())
            args.append(extra.reshape(1, D))
    out_specs, out_shape = [row()], [jax.ShapeDtypeStruct((T, D), F32)]
    if emit_h:
        out_specs.append(row())
        out_shape.append(jax.ShapeDtypeStruct((T, D), BF16))
    out = pl.pallas_call(
        functools.partial(_ffn_kernel, final=final, emit_h=emit_h),
        grid=(T // tm,),
        in_specs=in_specs,
        out_specs=out_specs,
        out_shape=out_shape,
        compiler_params=_params(("parallel",)),
        name="ffn_final" if final else "ffn",
    )(*args)
    return out if emit_h else out[0]


def _proj_kernel(h_ref, *refs):
    w_refs, o_ref = refs[:SLABS_PER_STEP], refs[SLABS_PER_STEP]
    h = h_ref[...]
    for s, w_ref in enumerate(w_refs):
        o_ref[s] = jnp.dot(h, w_ref[...].astype(BF16), preferred_element_type=F32).astype(o_ref.dtype)


def _proj(h, w_in, layer, *, tm=2048):
    T, D = h.shape

    def w_spec(k):
        def index(i, j):
            s = j * SLABS_PER_STEP + k
            return layer, 0, jnp.where(s < SLAB_GATE, s, s + (W_SLAB_GATE - SLAB_GATE))
        return pl.BlockSpec((None, D, SLAB), index)

    return pl.pallas_call(
        _proj_kernel,
        grid=(T // tm, N_SLABS // SLABS_PER_STEP),
        in_specs=[pl.BlockSpec((tm, D), lambda i, j: (i, 0))] + [w_spec(k) for k in range(SLABS_PER_STEP)],
        out_specs=pl.BlockSpec((SLABS_PER_STEP, tm, SLAB), lambda i, j: (j, i, 0)),
        out_shape=jax.ShapeDtypeStruct((N_SLABS, T, SLAB), BF16),
        compiler_params=_params(("parallel", "arbitrary")),
        name="in_proj",
    )(h, *([w_in] * SLABS_PER_STEP))


def _slab_spec(slab, rows):
    return pl.BlockSpec((None, None, rows, SLAB), lambda b, t: (slab, b, t, 0))


def _split3_bf16(x):
    hi = x.astype(BF16)
    r = x - hi.astype(F32)
    mid = r.astype(BF16)
    lo = (r - mid.astype(F32)).astype(BF16)
    return hi, mid, lo


def _hgrn_kernel(q_ref, f_ref, i_ref, g_ref, lb_ref, gn_ref, o_ref, st_ref, b_ref, k_ref, v_ref, *, n_chunks):
    C, S = HG_CHUNK, HG_SUB
    tt = n_chunks * C
    heads = [slice(h * HG_KDIM, (h + 1) * HG_KDIM) for h in range(HG_HEADS)]

    @pl.when(pl.program_id(1) == 0)
    def _():
        st_ref[...] = jnp.zeros_like(st_ref)

    lb = lb_ref[...]
    row = lax.broadcasted_iota(jnp.int32, (tt, tt), 0)
    col = lax.broadcasted_iota(jnp.int32, (tt, tt), 1)
    tri = ((col <= row) & (row // C == col // C)).astype(BF16)
    row8 = lax.broadcasted_iota(jnp.int32, (8, 1), 0)
    nt = (((1,), (1,)), ((), ()))
    tn = (((0,), (0,)), ((), ()))

    q = q_ref[...].astype(F32)
    f = f_ref[...].astype(F32)
    v = i_ref[...].astype(F32)
    v16 = v.astype(BF16)
    qf = q * _sigmoid(q) * (HG_KDIM ** -0.5)
    sf = _sigmoid(f)
    kf = (1.0 - lb) * (1.0 - sf)
    logf = jnp.log(lb + (1.0 - lb) * sf)
    b = sum(jnp.dot(tri, p, preferred_element_type=F32) for p in _split3_bf16(logf)) * LOG2E
    for h, hs in enumerate(heads):
        b_ref[h] = b[:, hs]
        k_ref[h] = kf[:, hs]
        v_ref[h] = v[:, hs]

    state = [st_ref[h] for h in range(HG_HEADS)]
    mm = []
    for c in range(n_chunks):
        rows = slice(c * C, (c + 1) * C)
        bc, qc, kc, vc = b[rows], qf[rows], kf[rows], v16[rows]
        b_last = bc[C - 1:C, :]
        qd = (qc * jnp.exp2(bc)).astype(BF16)
        kd_last = (kc * jnp.exp2(b_last - bc)).astype(BF16)
        outs = []
        for h, hs in enumerate(heads):
            outs.append(lax.dot_general(qd[:, hs], state[h].astype(BF16), nt, preferred_element_type=F32))
            state[h] = jnp.exp2(b_last[:, hs]) * state[h] + lax.dot_general(
                vc[:, hs], kd_last[:, hs], tn, preferred_element_type=F32)
        mm.append(outs)
    for h in range(HG_HEADS):
        st_ref[h] = state[h]

    zero = lambda n: jnp.zeros((n, HG_KDIM), F32)
    off = [jnp.zeros((tt, HG_VDIM), F32) for _ in range(HG_HEADS)]
    n = S
    while n < C:
        same_pair = (row // (2 * n)) == (col // (2 * n))
        for h, hs in enumerate(heads):
            qs, ks = [], []
            for lo in range(0, tt, 2 * n):
                mid, hi = lo + n, lo + 2 * n
                r = b_ref[h, pl.ds(mid - 1, n, stride=0), :]
                qs += [zero(n), qf[mid:hi, hs] * jnp.exp2(b[mid:hi, hs] - r)]
                ks += [kf[lo:mid, hs] * jnp.exp2(r - b[lo:mid, hs]), zero(n)]
            att = lax.dot_general(jnp.concatenate(qs, axis=0).astype(BF16), jnp.concatenate(ks, axis=0).astype(BF16),
                                  nt, preferred_element_type=F32)
            att = jnp.where(same_pair, att, 0.0).astype(BF16)
            off[h] = off[h] + jnp.dot(att, v16[:, hs], preferred_element_type=F32)
        n *= 2

    normed = []
    for h, hs in enumerate(heads):
        tiles = []
        for base in range(0, tt, S):
            for lo in range(0, S, 8):
                qq = qf[base + lo:base + lo + 8, hs]
                bb = b[base + lo:base + lo + 8, hs]
                acc = [jnp.zeros((8, HG_VDIM), F32) for _ in range(HG_ACC_CHAINS)]
                for j in range(lo + 8):
                    key = pl.ds(base + j, 8, stride=0)
                    p = qq * k_ref[h, key, :] * jnp.exp2(jnp.minimum(bb - b_ref[h, key, :], 0.0))
                    a = jnp.sum(p, axis=-1, keepdims=True)
                    if j >= lo:
                        a = jnp.where(row8 >= j - lo, a, 0.0)
                    acc[j % HG_ACC_CHAINS] = acc[j % HG_ACC_CHAINS] + a * v_ref[h, key, :]
                tiles.append((acc[0] + acc[1]) + (acc[2] + acc[3]))
        o = jnp.concatenate(tiles, axis=0) + (jnp.concatenate([mm[c][h] for c in range(n_chunks)], axis=0) + off[h])
        normed.append(o * lax.rsqrt(jnp.mean(o * o, axis=-1, keepdims=True) + NORM_EPS))
    g = g_ref[...].astype(F32)
    o_ref[...] = (jnp.concatenate(normed, axis=-1) * gn_ref[...] * (g * _sigmoid(g))).astype(o_ref.dtype)


def _hgrn(proj, lb, gnorm, *, tt=256):
    _, B, L, _ = proj.shape
    W = HG_HEADS * HG_KDIM
    return pl.pallas_call(
        functools.partial(_hgrn_kernel, n_chunks=tt // HG_CHUNK),
        grid=(B, L // tt),
        in_specs=[
            _slab_spec(SLAB_HQ, tt), _slab_spec(SLAB_HF, tt), _slab_spec(SLAB_HI, tt), _slab_spec(SLAB_HG, tt),
            pl.BlockSpec((1, W), lambda b, t: (0, 0)),
            pl.BlockSpec((1, W), lambda b, t: (0, 0)),
        ],
        out_specs=pl.BlockSpec((None, tt, W), lambda b, t: (b, t, 0)),
        out_shape=jax.ShapeDtypeStruct((B, L, W), BF16),
        scratch_shapes=[pltpu.VMEM((HG_HEADS, HG_VDIM, HG_KDIM), F32)] + [pltpu.VMEM((HG_HEADS, tt, HG_KDIM), F32)] * 3,
        compiler_params=_params(("parallel", "arbitrary")),
        name="hgrn2",
    )(proj, proj, proj, proj, lb.reshape(1, W), gnorm.reshape(1, W))


def _cmul(ar, ai, br, bi):
    return ar * br - ai * bi, ar * bi + ai * br


def _s5_kernel(u_ref, wb_ref, wc_ref, pj_ref, pq_ref, d_ref, wglu_ref, o_ref, x_ref, rq_ref, carry_ref, *, tt):
    SEGS, SR, PITCH, NS, LC = S5_SEGS, tt // S5_SEGS, S5_PITCH, S5_HALF_STATES, S5_SCAN_LANES
    row8 = lax.broadcasted_iota(jnp.int32, (SEGS, LC), 0)
    lane_tiles = range(BRANCH_WIDTH // 128)

    @pl.when(pl.program_id(1) == 0)
    def _():
        carry_ref[...] = jnp.zeros_like(carry_ref)

    u = u_ref[...].astype(F32)
    for lt in lane_tiles:
        for s in range(SEGS):
            rq_ref[lt, pl.ds(PITCH * s, SR), :] = u[SR * s:SR * (s + 1), lt * 128:(lt + 1) * 128]
    u = jnp.concatenate(
        [jnp.concatenate([rq_ref[lt, pl.ds(j, SEGS, stride=PITCH), :] for lt in lane_tiles], axis=-1)
         for j in range(SR)], axis=0)
    u16 = u.astype(BF16)

    for hf in range(2):
        x_ref[hf] = jnp.dot(u16[:, hf * S5_HALF_CH:(hf + 1) * S5_HALF_CH], wb_ref[hf],
                            preferred_element_type=F32)

    for hf in range(2):
        for lc in range(NS // LC):
            re = slice(lc * LC, (lc + 1) * LC)
            im = slice(NS + lc * LC, NS + (lc + 1) * LC)
            a1r, a1i = pj_ref[hf, 0:SEGS, re], pj_ref[hf, 0:SEGS, im]

            local = [(x_ref[hf, 0:SEGS, re], x_ref[hf, 0:SEGS, im])]
            for j in range(1, SR):
                xr, xi = _cmul(a1r, a1i, *local[-1])
                local.append((xr + x_ref[hf, j * SEGS:(j + 1) * SEGS, re], xi + x_ref[hf, j * SEGS:(j + 1) * SEGS, im]))
            er, ei = local[-1]

            for shift in (1, 2, 4):
                cr = jnp.where(row8 >= shift, pq_ref[hf, shift:shift + 1, re], 0.0)
                ci = jnp.where(row8 >= shift, pq_ref[hf, shift:shift + 1, im], 0.0)
                dr, di = _cmul(cr, ci, pltpu.roll(er, shift, 0), pltpu.roll(ei, shift, 0))
                er, ei = er + dr, ei + di
            c0r = carry_ref[hf, SEGS - 1:SEGS, re]
            c0i = carry_ref[hf, SEGS - 1:SEGS, im]
            sr, si = _cmul(pq_ref[hf, 0:SEGS, re], pq_ref[hf, 0:SEGS, im], c0r, c0i)
            sr = sr + jnp.where(row8 >= 1, pltpu.roll(er, 1, 0), 0.0)
            si = si + jnp.where(row8 >= 1, pltpu.roll(ei, 1, 0), 0.0)
            nr, ni = _cmul(pq_ref[hf, SEGS:2 * SEGS, re], pq_ref[hf, SEGS:2 * SEGS, im], c0r, c0i)
            carry_ref[hf, :, re] = er + nr
            carry_ref[hf, :, im] = ei + ni

            for j in range(SR):
                rows = slice(j * SEGS, (j + 1) * SEGS)
                dr, di = _cmul(pj_ref[hf, rows, re], pj_ref[hf, rows, im], sr, si)
                x_ref[hf, rows, re] = local[j][0] + dr
                x_ref[hf, rows, im] = local[j][1] + di

    y = jnp.concatenate(
        [jnp.dot(x_ref[hf].astype(BF16), wc_ref[hf], preferred_element_type=F32) for hf in range(2)], axis=-1)
    y = y + d_ref[...] * u
    z = jax.nn.gelu(y).astype(BF16)
    zz = jnp.dot(z, wglu_ref[...], preferred_element_type=F32)
    out = zz[:, :BRANCH_WIDTH] * _sigmoid(zz[:, BRANCH_WIDTH:])
    for lt in lane_tiles:
        for j in range(SR):
            rq_ref[lt, pl.ds(j, SEGS, stride=PITCH), :] = out[SEGS * j:SEGS * (j + 1), lt * 128:(lt + 1) * 128]
    for lt in lane_tiles:
        for s in range(SEGS):
            o_ref[SR * s:SR * (s + 1), lt * 128:(lt + 1) * 128] = rq_ref[lt, pl.ds(PITCH * s, SR), :].astype(o_ref.dtype)


def _s5(proj, wb, wc, pj, pq, d_skip, w_glu, *, tt=S5_TILE):
    _, B, L, _ = proj.shape
    W = BRANCH_WIDTH
    NS2 = 2 * S5_HALF_STATES
    return pl.pallas_call(
        functools.partial(_s5_kernel, tt=tt),
        grid=(B, L // tt),
        in_specs=[
            _slab_spec(SLAB_S5, tt),
            pl.BlockSpec((2, S5_HALF_CH, NS2), lambda b, t: (0, 0, 0)),
            pl.BlockSpec((2, NS2, S5_HALF_CH), lambda b, t: (0, 0, 0)),
            pl.BlockSpec((2, tt, NS2), lambda b, t: (0, 0, 0)),
            pl.BlockSpec((2, 2 * S5_SEGS, NS2), lambda b, t: (0, 0, 0)),
            pl.BlockSpec((1, W), lambda b, t: (0, 0)),
            pl.BlockSpec((W, 2 * W), lambda b, t: (0, 0)),
        ],
        out_specs=pl.BlockSpec((None, tt, W), lambda b, t: (b, t, 0)),
        out_shape=jax.ShapeDtypeStruct((B, L, W), BF16),
        scratch_shapes=[pltpu.VMEM((2, tt, NS2), F32),
                        pltpu.VMEM((W // 128, S5_SEGS * S5_PITCH, 128), F32),
                        pltpu.VMEM((2, S5_SEGS, NS2), F32)],
        compiler_params=_params(("parallel", "arbitrary")),
        name="s5",
    )(proj, wb, wc, pj, pq, d_skip.reshape(1, W), w_glu)


def _s5_tables(a_re, a_im, log_dt, b_re, b_im, c_re, c_im):
    dt = jnp.exp(log_dt)[:, None]
    mag = jnp.exp(a_re * dt)
    ab_re = mag * jnp.cos(a_im * dt)
    ab_im = mag * jnp.sin(a_im * dt)
    den = a_re * a_re + a_im * a_im
    zr = ((ab_re - 1.0) * a_re + ab_im * a_im) / den
    zi = (ab_im * a_re - (ab_re - 1.0) * a_im) / den
    bb_re = zr[..., None] * b_re - zi[..., None] * b_im
    bb_im = zr[..., None] * b_im + zi[..., None] * b_re
    GH = S5_GROUPS // 2
    eye = jnp.eye(GH, dtype=F32)

    def bdiag_in(m):
        m = m.reshape(2, GH, S5_STATE, S5_GROUP)
        return jnp.einsum('hgpc,gk->hgckp', m, eye).reshape(2, GH * S5_GROUP, GH * S5_STATE)

    def bdiag_out(m):
        m = m.reshape(2, GH, S5_GROUP, S5_STATE)
        return jnp.einsum('hgcp,gk->hgpkc', m, eye).reshape(2, GH * S5_STATE, GH * S5_GROUP)

    wb = jnp.concatenate([bdiag_in(bb_re), bdiag_in(bb_im)], axis=-1).astype(BF16)
    wc = jnp.concatenate([bdiag_out(c_re), -bdiag_out(c_im)], axis=1).astype(BF16)
    sr = S5_TILE // S5_SEGS

    def powers(n):
        n = jnp.asarray(n, F32)[:, None, None]
        mag = jnp.exp(n * (a_re * dt)[None])
        ang = n * (a_im * dt)[None]
        split = lambda p: p.reshape(p.shape[0], 2, S5_HALF_STATES).transpose(1, 0, 2)
        return jnp.concatenate([split(mag * jnp.cos(ang)), split(mag * jnp.sin(ang))], axis=-1)

    pj = powers(np.repeat(np.arange(1, sr + 1), S5_SEGS))
    pq = powers(np.concatenate([sr * np.arange(S5_SEGS), sr * np.arange(1, S5_SEGS + 1)]))
    return wb, wc, pj, pq


def _conv_kernel(a_ref, b_ref, w_ref, cb_ref, lg_ref, lbias_ref, o_ref, z_ref, y_ref, *, tt, rb):
    H = CONV_HALO

    @pl.when(pl.program_id(1) == 0)
    def _():
        z_ref[0:H, :] = jnp.zeros((H, BRANCH_WIDTH), F32)

    z_ref[H:H + tt, :] = a_ref[...].astype(F32) * _sigmoid(b_ref[...].astype(F32))
    cb = cb_ref[...]
    lg = lg_ref[...]
    lbias = lbias_ref[...]

    def block(k, carry):
        r0 = pl.multiple_of(k * rb, rb)
        for lc in range(0, BRANCH_WIDTH, 128):
            win = z_ref[pl.ds(r0, rb + H), lc:lc + 128]
            acc = [jnp.zeros((rb, 128), F32), jnp.zeros((rb, 128), F32)]
            for c in range(8):
                shifted = win if c == 0 else pltpu.roll(win, rb + H - c, 0)
                for off in range(c, H + 1, 8):
                    w = off - (H - CONV_WIDTH + 1)
                    if 0 <= w < CONV_WIDTH:
                        acc[w % 2] = acc[w % 2] + shifted[off - c:off - c + rb, :] * w_ref[w:w + 1, lc:lc + 128]
            y_ref[pl.ds(r0, rb), lc:lc + 128] = acc[0] + acc[1]
        return carry

    lax.fori_loop(0, tt // rb, block, 0)
    z_ref[0:H, :] = z_ref[tt:tt + H, :]

    acc = y_ref[...] + cb
    xc = acc - jnp.mean(acc, axis=-1, keepdims=True)
    var = jnp.mean(xc * xc, axis=-1, keepdims=True)
    y = xc * lax.rsqrt(var + NORM_EPS) * lg + lbias
    o_ref[...] = (y * _sigmoid(y)).astype(o_ref.dtype)


def _conv(proj, conv_w, conv_b, ln_g, ln_b, *, tt=512, rb=32):
    _, B, L, _ = proj.shape
    W = BRANCH_WIDTH
    w_pad = jnp.zeros((CONV_HALO, W), F32).at[:CONV_WIDTH].set(conv_w)
    vec = lambda: pl.BlockSpec((1, W), lambda b, t: (0, 0))
    return pl.pallas_call(
        functools.partial(_conv_kernel, tt=tt, rb=rb),
        grid=(B, L // tt),
        in_specs=[_slab_spec(SLAB_CONV_A, tt), _slab_spec(SLAB_CONV_B, tt),
                  pl.BlockSpec((CONV_HALO, W), lambda b, t: (0, 0)), vec(), vec(), vec()],
        out_specs=pl.BlockSpec((None, tt, W), lambda b, t: (b, t, 0)),
        out_shape=jax.ShapeDtypeStruct((B, L, W), BF16),
        scratch_shapes=[pltpu.VMEM((tt + CONV_HALO, W), F32), pltpu.VMEM((tt, W), F32)],
        compiler_params=_params(("parallel", "arbitrary")),
        name="conformer_conv",
    )(proj, proj, w_pad, conv_b.reshape(1, W), ln_g.reshape(1, W), ln_b.reshape(1, W))


def _rope_table_kernel(pos_ref, inv_ref, cos_ref, sin_ref):
    ang = pos_ref[...] * inv_ref[...]
    lane = lax.broadcasted_iota(jnp.int32, ang.shape, 1)
    cos_ref[...] = jnp.cos(ang)
    sin_ref[...] = jnp.where(lane < ROPE_HALF, -jnp.sin(ang), jnp.sin(ang))


def _rope_tables(positions, *, tm=1024):
    T = positions.size
    inv = ROPE_THETA ** (-jnp.arange(ROPE_HALF, dtype=F32) / ROPE_HALF)
    inv = jnp.concatenate([inv, inv, jnp.zeros((ATT_HEAD_DIM - ROPE_DIM,), F32)]).reshape(1, ATT_HEAD_DIM)
    pos = positions.astype(F32).reshape(T, 1)
    return pl.pallas_call(
        _rope_table_kernel,
        grid=(T // tm,),
        in_specs=[pl.BlockSpec((tm, 1), lambda i: (i, 0)), pl.BlockSpec((1, ATT_HEAD_DIM), lambda i: (0, 0))],
        out_specs=[pl.BlockSpec((tm, ATT_HEAD_DIM), lambda i: (i, 0))] * 2,
        out_shape=[jax.ShapeDtypeStruct((T, ATT_HEAD_DIM), F32)] * 2,
        compiler_params=_params(("parallel",)),
        name="rope_tables",
    )(pos, inv)


def _rope(x, cos, sin):
    lane = lax.broadcasted_iota(jnp.int32, x.shape, 1)
    partner = jnp.where(lane < ROPE_HALF, pltpu.roll(x, ATT_HEAD_DIM - ROPE_HALF, 1), pltpu.roll(x, ROPE_HALF, 1))
    return x * cos + partner * sin


DEINT_STRIDE = 4


def _qkv_kernel(h_ref, wq_ref, wk_ref, wv_ref, cos_ref, sin_ref, o_ref, w16_ref, s_ref, t_ref, *, d):
    tm = h_ref.shape[0]

    @pl.when(pl.program_id(0) == 0)
    def _():
        for j, w_ref in enumerate((wq_ref, wk_ref, wv_ref)):
            w16_ref[j] = w_ref[...].astype(BF16)

    h16 = h_ref[...]
    cos, sin = cos_ref[...], sin_ref[...]
    for j in range(3):
        res = jnp.dot(h16, w16_ref[j], preferred_element_type=F32)
        for h in range(ATT_HEADS_PER_GROUP):
            hs = slice(h * ATT_HEAD_DIM, (h + 1) * ATT_HEAD_DIM)
            s_ref[j, h] = _rope(res[:, hs], cos, sin) if j < 2 else res[:, hs]
            if d <= DEINT_STRIDE:
                for r in range(d):
                    o_ref[j, r, :, hs] = s_ref[j, h, pl.ds(r, tm // d, stride=d), :].astype(o_ref.dtype)
            else:
                q = tm // DEINT_STRIDE
                for r0 in range(DEINT_STRIDE):
                    t_ref[j, h, pl.ds(r0 * q, q), :] = s_ref[j, h, pl.ds(r0, q, stride=DEINT_STRIDE), :]
                for r0 in range(DEINT_STRIDE):
                    for r1 in range(d // DEINT_STRIDE):
                        o_ref[j, DEINT_STRIDE * r1 + r0, :, hs] = t_ref[
                            j, h, pl.ds(r0 * q + r1, tm // d, stride=d // DEINT_STRIDE), :].astype(o_ref.dtype)


def _qkv(h, w_in, layer, cos_t, sin_t, gi, d, B, *, tm=1024):
    T, D = h.shape
    L = T // B
    tiles = L // tm
    W = ATT_HEADS_PER_GROUP * ATT_HEAD_DIM
    assert d <= DEINT_STRIDE or d == DEINT_STRIDE * DEINT_STRIDE
    stage = (3, ATT_HEADS_PER_GROUP, tm, ATT_HEAD_DIM)
    w_spec = lambda j: pl.BlockSpec((None, D, SLAB), lambda i: (layer, 0, W_SLAB_Q + gi + 3 * j))
    return pl.pallas_call(
        functools.partial(_qkv_kernel, d=d),
        grid=(T // tm,),
        in_specs=[
            pl.BlockSpec((tm, D), lambda i: (i, 0)),
            w_spec(0), w_spec(1), w_spec(2),
            pl.BlockSpec((tm, ATT_HEAD_DIM), lambda i: (i, 0)),
            pl.BlockSpec((tm, ATT_HEAD_DIM), lambda i: (i, 0)),
        ],
        out_specs=pl.BlockSpec((3, None, d, tm // d, W), lambda i: (0, i // tiles, 0, i % tiles, 0)),
        out_shape=jax.ShapeDtypeStruct((3, B, d, L // d, W), BF16),
        scratch_shapes=[pltpu.VMEM((3, D, SLAB), BF16),
                        pltpu.VMEM(stage, F32),
                        pltpu.VMEM(stage if d > DEINT_STRIDE else (1, 1, 8, ATT_HEAD_DIM), F32)],
        compiler_params=_params(("arbitrary",)),
        name=f"qkv_proj_g{gi}",
    )(h, w_in, w_in, w_in, cos_t, sin_t)


ATT_UNROLL = 8


def _att_kernel(q_ref, kc_ref, kp_ref, vc_ref, vp_ref, o_ref, lse_ref, *, d, nq):
    Q = ATT_QBLOCK
    n = pl.program_id(1)
    qi = lax.broadcasted_iota(jnp.int32, (Q, 2 * Q), 0)
    kj = lax.broadcasted_iota(jnp.int32, (Q, 2 * Q), 1)
    rel = Q + qi - kj
    band = (rel >= 0) & (rel <= Q)
    band_first = band & ((n > 0) | (kj >= Q))
    head_of_lane = lax.broadcasted_iota(jnp.int32, (Q, ATT_HEAD_DIM), 1) // (ATT_HEAD_DIM // ATT_HEADS_PER_GROUP)
    nt = (((1,), (1,)), ((), ()))

    def block(r, i):
        rows = slice(i * Q, (i + 1) * Q)
        outs = []
        lse = jnp.zeros((Q, ATT_HEAD_DIM), F32)
        for h in range(ATT_HEADS_PER_GROUP):
            hs = slice(h * ATT_HEAD_DIM, (h + 1) * ATT_HEAD_DIM)
            if i == 0:
                k = jnp.concatenate([kp_ref[r, :, hs], kc_ref[r, rows, hs]], axis=0)
                v = jnp.concatenate([vp_ref[r, :, hs], vc_ref[r, rows, hs]], axis=0)
            else:
                k = kc_ref[r, (i - 1) * Q:(i + 1) * Q, hs]
                v = vc_ref[r, (i - 1) * Q:(i + 1) * Q, hs]
            s = lax.dot_general(q_ref[r, rows, hs], k, nt, preferred_element_type=F32) * (ATT_HEAD_DIM ** -0.5 * LOG2E)
            s = jnp.where(band_first if i == 0 else band, s, MASK_VALUE)
            m = jnp.max(s, axis=-1, keepdims=True)
            p = jnp.exp2(s - m)
            l = jnp.sum(p, axis=-1, keepdims=True)
            outs.append(jnp.dot(p.astype(BF16), v, preferred_element_type=F32) / l)
            lse = jnp.where(head_of_lane == h, m * LN2 + jnp.log(l), lse)
        o_ref[r, rows, :] = jnp.concatenate(outs, axis=-1).astype(o_ref.dtype)
        lse_ref[r, rows, :] = lse

    if d <= ATT_UNROLL:
        for r in range(d):
            for i in range(nq):
                block(r, i)
    else:
        def residues(it, carry):
            for u in range(ATT_UNROLL):
                for i in range(nq):
                    block(it * ATT_UNROLL + u, i)
            return carry

        lax.fori_loop(0, d // ATT_UNROLL, residues, 0)


def _attention_group(qkv, gi):
    _, B, d, n_sub, W = qkv.shape
    Q = ATT_QBLOCK
    nq = max(1, ATT_UNROLL // d)

    def cur(s):
        return pl.BlockSpec((None, None, d, nq * Q, W), lambda b, n: (s, b, 0, n, 0))

    def prev(s):
        return pl.BlockSpec((None, None, d, Q, W), lambda b, n: (s, b, 0, jnp.maximum(nq * n - 1, 0), 0))

    return pl.pallas_call(
        functools.partial(_att_kernel, d=d, nq=nq),
        grid=(B, n_sub // (nq * Q)),
        in_specs=[cur(0), cur(1), prev(1), cur(2), prev(2)],
        out_specs=[pl.BlockSpec((None, d, nq * Q, W), lambda b, n: (b, 0, n, 0)),
                   pl.BlockSpec((None, d, nq * Q, ATT_HEAD_DIM), lambda b, n: (b, 0, n, 0))],
        out_shape=[jax.ShapeDtypeStruct((B, d, n_sub, W), BF16),
                   jax.ShapeDtypeStruct((B, d, n_sub, ATT_HEAD_DIM), F32)],
        compiler_params=_params(("parallel", "arbitrary")),
        name=f"dilated_attention_g{gi}",
    )(qkv, qkv, qkv, qkv, qkv)


def _merge_kernel(x_ref, ya_ref, yb_ref, yc_ref, o1_ref, o2_ref, o3_ref, l1_ref, l2_ref, l3_ref,
                  g0_ref, g1_ref, g2_ref, g3_ref, g4_ref, g5_ref, g6_ref, g7_ref, wb_ref, wo_ref, out_ref,
                  os_ref, ls_ref):
    for g, (o_ref, l_ref) in enumerate(((o1_ref, l1_ref), (o2_ref, l2_ref), (o3_ref, l3_ref))):
        d, rows = o_ref.shape[0], o_ref.shape[1]
        for r in range(d):
            o = o_ref[r].astype(F32)
            for h in range(ATT_HEADS_PER_GROUP):
                os_ref[g, h, pl.ds(r, rows, stride=d), :] = o[:, h * ATT_HEAD_DIM:(h + 1) * ATT_HEAD_DIM]
            ls_ref[g, pl.ds(r, rows, stride=d), :] = l_ref[r]
    lanes = ATT_HEAD_DIM // ATT_HEADS_PER_GROUP
    parts = []
    for h in range(ATT_HEADS_PER_GROUP):
        l1, l2, l3 = (ls_ref[g, :, h * lanes:h * lanes + 1] for g in range(3))
        m = jnp.maximum(jnp.maximum(l1, l2), l3)
        e1, e2, e3 = jnp.exp(l1 - m), jnp.exp(l2 - m), jnp.exp(l3 - m)
        parts.append((e1 * os_ref[0, h] + e2 * os_ref[1, h] + e3 * os_ref[2, h]) / (e1 + e2 + e3))
    yd = jnp.concatenate(parts, axis=-1)
    ys = (ya_ref[...], yb_ref[...], yc_ref[...], yd.astype(BF16))
    gates = ((g0_ref, g1_ref), (g2_ref, g3_ref), (g4_ref, g5_ref), (g6_ref, g7_ref))
    merged = None
    for k in range(N_BRANCHES):
        gate = _sigmoid(jnp.concatenate([gates[k][0][...], gates[k][1][...]], axis=-1).astype(F32))
        term = gate * jnp.dot(ys[k], wb_ref[k], preferred_element_type=F32)
        merged = term if merged is None else merged + term
    out_ref[...] = x_ref[...] + jnp.dot(merged.astype(BF16), wo_ref[...], preferred_element_type=F32)


def _merge(x, ya, yb, yc, att, proj, w_branch, w_out, *, tm=512):
    B, L, D = x.shape
    W = BRANCH_WIDTH
    (o1, l1), (o2, l2), (o3, l3) = att
    row = lambda: pl.BlockSpec((None, tm, W), lambda b, t: (b, t, 0))

    def res(a):
        d, last = a.shape[1], a.shape[3]
        return pl.BlockSpec((None, d, tm // d, last), lambda b, t: (b, 0, t, 0))

    return pl.pallas_call(
        _merge_kernel,
        grid=(B, L // tm),
        in_specs=[pl.BlockSpec((None, tm, D), lambda b, t: (b, t, 0))] + [row() for _ in range(3)]
                 + [res(a) for a in (o1, o2, o3, l1, l2, l3)]
                 + [_slab_spec(SLAB_GATE + s, tm) for s in range(8)]
                 + [pl.BlockSpec((N_BRANCHES, W, D), lambda b, t: (0, 0, 0)), pl.BlockSpec((D, D), lambda b, t: (0, 0))],
        out_specs=pl.BlockSpec((None, tm, D), lambda b, t: (b, t, 0)),
        out_shape=jax.ShapeDtypeStruct((B, L, D), F32),
        scratch_shapes=[pltpu.VMEM((3, ATT_HEADS_PER_GROUP, tm, ATT_HEAD_DIM), F32),
                        pltpu.VMEM((3, tm, ATT_HEAD_DIM), F32)],
        compiler_params=_params(("parallel", "parallel")),
        name="gated_merge",
    )(x, ya, yb, yc, o1, o2, o3, l1, l2, l3, *([proj] * 8), w_branch, w_out)


def kernel(x, positions, ffn1_norm, ffn1_w_gate, ffn1_w_up, ffn1_w_down, mix_norm, w_in, hg_lb_logits, hg_gnorm, s5_a_re, s5_a_im, s5_log_dt, s5_b_re, s5_b_im, s5_c_re, s5_c_im, s5_d, s5_w_glu, conv_w, conv_b, conv_ln_g, conv_ln_b, w_branch, w_out, ffn2_norm, ffn2_w_gate, ffn2_w_up, ffn2_w_down, final_norm):
    B, L, D = x.shape
    T = B * L
    depth = w_in.shape[0]
    lb_soft = jax.nn.softmax(hg_lb_logits.astype(F32), axis=0)
    lb_all = jnp.cumsum(lb_soft, axis=0) - lb_soft[0]
    cos_t, sin_t = _rope_tables(positions)
    bf = lambda w: w.astype(BF16)

    ffn1_w = [_to_bf16(w) for w in (ffn1_w_gate, ffn1_w_up, ffn1_w_down)]
    ffn2_w = [_to_bf16(w) for w in (ffn2_w_gate, ffn2_w_up, ffn2_w_down)]

    xt = x.reshape(T, D)
    for l in range(depth):
        xt, h_mix = _ffn(xt, ffn1_norm[l], *ffn1_w, l, next_gain=mix_norm[l])
        proj = _proj(h_mix, w_in, l).reshape(N_SLABS, B, L, SLAB)
        ya = _hgrn(proj, lb_all[l], hg_gnorm[l])
        wb, wc, pj, pq = _s5_tables(s5_a_re[l], s5_a_im[l], s5_log_dt[l], s5_b_re[l], s5_b_im[l], s5_c_re[l], s5_c_im[l])
        yb = _s5(proj, wb, wc, pj, pq, s5_d[l], bf(s5_w_glu[l]))
        yc = _conv(proj, conv_w[l], conv_b[l], conv_ln_g[l], conv_ln_b[l])
        att = [_attention_group(_qkv(h_mix, w_in, l, cos_t, sin_t, gi, dil, B), gi)
               for gi, (_, dil) in enumerate(ATT_CONFIGS)]
        xt = _merge(xt.reshape(B, L, D), ya, yb, yc, att, proj, bf(w_branch[l]), bf(w_out[l])).reshape(T, D)
        last = l == depth - 1
        xt = _ffn(xt, ffn2_norm[l], *ffn2_w, l, final_gain=final_norm if last else None)
    return xt.reshape(B, L, D)
```

```python
import functools
import math

import numpy as np
import jax
import jax.numpy as jnp
from jax import lax
from jax.experimental import pallas as pl
from jax.experimental.pallas import tpu as pltpu

F32 = jnp.float32
BF16 = jnp.bfloat16

NORM_EPS = 1e-6
MASK_VALUE = -1e30
LOG2E = math.log2(math.e)
LN2 = math.log(2.0)
D_MODEL = 1024
D_FF = 2816
N_BRANCHES = 4
BRANCH_WIDTH = 512

HG_HEADS = 4
HG_KDIM = 128
HG_VDIM = 128
HG_CHUNK = 64
HG_SUB = 16
HG_ACC_CHAINS = 4

S5_GROUP = 16
S5_GROUPS = 32
S5_STATE = 64
S5_HALF_CH = 256
S5_HALF_STATES = (S5_GROUPS // 2) * S5_STATE
S5_TILE = 256
S5_SEGS = 8
S5_PITCH = S5_TILE // S5_SEGS + 8
S5_SCAN_LANES = 512

CONV_WIDTH = 31
CONV_HALO = 32

ATT_HEAD_DIM = 128
ATT_CONFIGS = ((128, 1), (512, 4), (2048, 16))
ATT_HEADS_PER_GROUP = 4
ATT_HEADS = 12
ATT_QBLOCK = 128
ROPE_THETA = 500000.0
ROPE_DIM = 32
ROPE_HALF = 16

SLAB = 512
SLAB_HQ, SLAB_HF, SLAB_HI, SLAB_HG, SLAB_S5, SLAB_CONV_A, SLAB_CONV_B = 0, 1, 2, 3, 4, 5, 6
W_SLAB_Q = 7
W_SLAB_GATE = 16
SLABS_PER_STEP = 3
SLAB_GATE = 7
N_SLABS = 15

VMEM_LIMIT = 56 * 1024 * 1024


def _params(sem):
    return pltpu.CompilerParams(dimension_semantics=sem, vmem_limit_bytes=VMEM_LIMIT)


def _rms(x):
    return x * lax.rsqrt(jnp.mean(x * x, axis=-1, keepdims=True) + NORM_EPS)


def _sigmoid(x):
    return 0.5 * jnp.tanh(0.5 * x) + 0.5


def _ffn_kernel(x_ref, gain_ref, wg_ref, wu_ref, wd_ref, *rest, final, emit_h):
    rest = list(rest)
    fgain_ref = rest.pop(0) if final else None
    hgain_ref = rest.pop(0) if emit_h else None
    o_ref = rest.pop(0)
    x = x_ref[...]
    h = (_rms(x) * gain_ref[...]).astype(BF16)
    g = jnp.dot(h, wg_ref[...], preferred_element_type=F32)
    u = jnp.dot(h, wu_ref[...], preferred_element_type=F32)
    a = (g * _sigmoid(g) * u).astype(BF16)
    y = x + 0.5 * jnp.dot(a, wd_ref[...], preferred_element_type=F32)
    if final:
        y = _rms(y) * fgain_ref[...]
    o_ref[...] = y
    if emit_h:
        rest[0][...] = (_rms(y) * hgain_ref[...]).astype(BF16)


def _cast_kernel(w_ref, o_ref):
    o_ref[...] = w_ref[...].astype(o_ref.dtype)


def _to_bf16(w, *, rows=256):
    depth, R, C = w.shape
    spec = pl.BlockSpec((None, rows, C), lambda l, i: (l, i, 0))
    return pl.pallas_call(
        _cast_kernel,
        grid=(depth, R // rows),
        in_specs=[spec],
        out_specs=spec,
        out_shape=jax.ShapeDtypeStruct(w.shape, BF16),
        compiler_params=_params(("parallel", "parallel")),
        name="weights_to_bf16",
    )(w)


def _ffn(x, gain, wg, wu, wd, layer, *, final_gain=None, next_gain=None, tm=512):
    T, D = x.shape
    FF = wg.shape[2]
    final, emit_h = final_gain is not None, next_gain is not None
    vec = lambda: pl.BlockSpec((1, D), lambda i: (0, 0))
    row = lambda: pl.BlockSpec((tm, D), lambda i: (i, 0))
    resident = lambda r, c: pl.BlockSpec((None, r, c), lambda i: (layer, 0, 0), pipeline_mode=pl.Buffered(1))
    in_specs = [row(), vec(), resident(D, FF), resident(D, FF), resident(FF, D)]
    args = [x, gain.reshape(1, D), wg, wu, wd]
    for extra in (final_gain, next_gain):
        if extra is not None:
            in_specs.append(vec())
            args.append(extra.reshape(1, D))
    out_specs, out_shape = [row()], [jax.ShapeDtypeStruct((T, D), F32)]
    if emit_h:
        out_specs.append(row())
        out_shape.append(jax.ShapeDtypeStruct((T, D), BF16))
    out = pl.pallas_call(
        functools.partial(_ffn_kernel, final=final, emit_h=emit_h),
        grid=(T // tm,),
        in_specs=in_specs,
        out_specs=out_specs,
        out_shape=out_shape,
        compiler_params=_params(("parallel",)),
        name="ffn_final" if final else "ffn",
    )(*args)
    return out if emit_h else out[0]


def _proj_kernel(h_ref, *refs):
    w_refs, o_ref = refs[:SLABS_PER_STEP], refs[SLABS_PER_STEP]
    h = h_ref[...]
    for s, w_ref in enumerate(w_refs):
        o_ref[s] = jnp.dot(h, w_ref[...].astype(BF16), preferred_element_type=F32).astype(o_ref.dtype)


def _proj(h, w_in, layer, *, tm=2048):
    T, D = h.shape

    def w_spec(k):
        def index(i, j):
            s = j * SLABS_PER_STEP + k
            return layer, 0, jnp.where(s < SLAB_GATE, s, s + (W_SLAB_GATE - SLAB_GATE))
        return pl.BlockSpec((None, D, SLAB), index)

    return pl.pallas_call(
        _proj_kernel,
        grid=(T // tm, N_SLABS // SLABS_PER_STEP),
        in_specs=[pl.BlockSpec((tm, D), lambda i, j: (i, 0))] + [w_spec(k) for k in range(SLABS_PER_STEP)],
        out_specs=pl.BlockSpec((SLABS_PER_STEP, tm, SLAB), lambda i, j: (j, i, 0)),
        out_shape=jax.ShapeDtypeStruct((N_SLABS, T, SLAB), BF16),
        compiler_params=_params(("parallel", "arbitrary")),
        name="in_proj",
    )(h, *([w_in] * SLABS_PER_STEP))


def _slab_spec(slab, rows):
    return pl.BlockSpec((None, None, rows, SLAB), lambda b, t: (slab, b, t, 0))


def _split3_bf16(x):
    hi = x.astype(BF16)
    r = x - hi.astype(F32)
    mid = r.astype(BF16)
    lo = (r - mid.astype(F32)).astype(BF16)
    return hi, mid, lo


def _hgrn_kernel(q_ref, f_ref, i_ref, g_ref, lb_ref, gn_ref, o_ref, st_ref, b_ref, k_ref, v_ref, *, n_chunks):
    C, S = HG_CHUNK, HG_SUB
    tt = n_chunks * C
    heads = [slice(h * HG_KDIM, (h + 1) * HG_KDIM) for h in range(HG_HEADS)]

    @pl.when(pl.program_id(1) == 0)
    def _():
        st_ref[...] = jnp.zeros_like(st_ref)

    lb = lb_ref[...]
    row = lax.broadcasted_iota(jnp.int32, (tt, tt), 0)
    col = lax.broadcasted_iota(jnp.int32, (tt, tt), 1)
    tri = ((col <= row) & (row // C == col // C)).astype(BF16)
    row8 = lax.broadcasted_iota(jnp.int32, (8, 1), 0)
    nt = (((1,), (1,)), ((), ()))
    tn = (((0,), (0,)), ((), ()))

    q = q_ref[...].astype(F32)
    f = f_ref[...].astype(F32)
    v = i_ref[...].astype(F32)
    v16 = v.astype(BF16)
    qf = q * _sigmoid(q) * (HG_KDIM ** -0.5)
    sf = _sigmoid(f)
    kf = (1.0 - lb) * (1.0 - sf)
    logf = jnp.log(lb + (1.0 - lb) * sf)
    b = sum(jnp.dot(tri, p, preferred_element_type=F32) for p in _split3_bf16(logf)) * LOG2E
    for h, hs in enumerate(heads):
        b_ref[h] = b[:, hs]
        k_ref[h] = kf[:, hs]
        v_ref[h] = v[:, hs]

    state = [st_ref[h] for h in range(HG_HEADS)]
    mm = []
    for c in range(n_chunks):
        rows = slice(c * C, (c + 1) * C)
        bc, qc, kc, vc = b[rows], qf[rows], kf[rows], v16[rows]
        b_last = bc[C - 1:C, :]
        qd = (qc * jnp.exp2(bc)).astype(BF16)
        kd_last = (kc * jnp.exp2(b_last - bc)).astype(BF16)
        outs = []
        for h, hs in enumerate(heads):
            outs.append(lax.dot_general(qd[:, hs], state[h].astype(BF16), nt, preferred_element_type=F32))
            state[h] = jnp.exp2(b_last[:, hs]) * state[h] + lax.dot_general(
                vc[:, hs], kd_last[:, hs], tn, preferred_element_type=F32)
        mm.append(outs)
    for h in range(HG_HEADS):
        st_ref[h] = state[h]

    zero = lambda n: jnp.zeros((n, HG_KDIM), F32)
    off = [jnp.zeros((tt, HG_VDIM), F32) for _ in range(HG_HEADS)]
    n = S
    while n < C:
        same_pair = (row // (2 * n)) == (col // (2 * n))
        for h, hs in enumerate(heads):
            qs, ks = [], []
            for lo in range(0, tt, 2 * n):
                mid, hi = lo + n, lo + 2 * n
                r = b_ref[h, pl.ds(mid - 1, n, stride=0), :]
                qs += [zero(n), qf[mid:hi, hs] * jnp.exp2(b[mid:hi, hs] - r)]
                ks += [kf[lo:mid, hs] * jnp.exp2(r - b[lo:mid, hs]), zero(n)]
            att = lax.dot_general(jnp.concatenate(qs, axis=0).astype(BF16), jnp.concatenate(ks, axis=0).astype(BF16),
                                  nt, preferred_element_type=F32)
            att = jnp.where(same_pair, att, 0.0).astype(BF16)
            off[h] = off[h] + jnp.dot(att, v16[:, hs], preferred_element_type=F32)
        n *= 2

    normed = []
    for h, hs in enumerate(heads):
        tiles = []
        for base in range(0, tt, S):
            for lo in range(0, S, 8):
                qq = qf[base + lo:base + lo + 8, hs]
                bb = b[base + lo:base + lo + 8, hs]
                acc = [jnp.zeros((8, HG_VDIM), F32) for _ in range(HG_ACC_CHAINS)]
                for j in range(lo + 8):
                    key = pl.ds(base + j, 8, stride=0)
                    p = qq * k_ref[h, key, :] * jnp.exp2(jnp.minimum(bb - b_ref[h, key, :], 0.0))
                    a = jnp.sum(p, axis=-1, keepdims=True)
                    if j >= lo:
                        a = jnp.where(row8 >= j - lo, a, 0.0)
                    acc[j % HG_ACC_CHAINS] = acc[j % HG_ACC_CHAINS] + a * v_ref[h, key, :]
                tiles.append((acc[0] + acc[1]) + (acc[2] + acc[3]))
        o = jnp.concatenate(tiles, axis=0) + (jnp.concatenate([mm[c][h] for c in range(n_chunks)], axis=0) + off[h])
        normed.append(o * lax.rsqrt(jnp.mean(o * o, axis=-1, keepdims=True) + NORM_EPS))
    g = g_ref[...].astype(F32)
    o_ref[...] = (jnp.concatenate(normed, axis=-1) * gn_ref[...] * (g * _sigmoid(g))).astype(o_ref.dtype)


def _hgrn(proj, lb, gnorm, *, tt=256):
    _, B, L, _ = proj.shape
    W = HG_HEADS * HG_KDIM
    return pl.pallas_call(
        functools.partial(_hgrn_kernel, n_chunks=tt // HG_CHUNK),
        grid=(B, L // tt),
        in_specs=[
            _slab_spec(SLAB_HQ, tt), _slab_spec(SLAB_HF, tt), _slab_spec(SLAB_HI, tt), _slab_spec(SLAB_HG, tt),
            pl.BlockSpec((1, W), lambda b, t: (0, 0)),
            pl.BlockSpec((1, W), lambda b, t: (0, 0)),
        ],
        out_specs=pl.BlockSpec((None, tt, W), lambda b, t: (b, t, 0)),
        out_shape=jax.ShapeDtypeStruct((B, L, W), BF16),
        scratch_shapes=[pltpu.VMEM((HG_HEADS, HG_VDIM, HG_KDIM), F32)] + [pltpu.VMEM((HG_HEADS, tt, HG_KDIM), F32)] * 3,
        compiler_params=_params(("parallel", "arbitrary")),
        name="hgrn2",
    )(proj, proj, proj, proj, lb.reshape(1, W), gnorm.reshape(1, W))


def _cmul(ar, ai, br, bi):
    return ar * br - ai * bi, ar * bi + ai * br


def _s5_kernel(u_ref, wb_ref, wc_ref, pj_ref, pq_ref, d_ref, wglu_ref, o_ref, x_ref, rq_ref, carry_ref, *, tt):
    SEGS, SR, PITCH, NS, LC = S5_SEGS, tt // S5_SEGS, S5_PITCH, S5_HALF_STATES, S5_SCAN_LANES
    row8 = lax.broadcasted_iota(jnp.int32, (SEGS, LC), 0)
    lane_tiles = range(BRANCH_WIDTH // 128)

    @pl.when(pl.program_id(1) == 0)
    def _():
        carry_ref[...] = jnp.zeros_like(carry_ref)

    u = u_ref[...].astype(F32)
    for lt in lane_tiles:
        for s in range(SEGS):
            rq_ref[lt, pl.ds(PITCH * s, SR), :] = u[SR * s:SR * (s + 1), lt * 128:(lt + 1) * 128]
    u = jnp.concatenate(
        [jnp.concatenate([rq_ref[lt, pl.ds(j, SEGS, stride=PITCH), :] for lt in lane_tiles], axis=-1)
         for j in range(SR)], axis=0)
    u16 = u.astype(BF16)

    for hf in range(2):
        x_ref[hf] = jnp.dot(u16[:, hf * S5_HALF_CH:(hf + 1) * S5_HALF_CH], wb_ref[hf],
                            preferred_element_type=F32)

    for hf in range(2):
        for lc in range(NS // LC):
            re = slice(lc * LC, (lc + 1) * LC)
            im = slice(NS + lc * LC, NS + (lc + 1) * LC)
            a1r, a1i = pj_ref[hf, 0:SEGS, re], pj_ref[hf, 0:SEGS, im]

            local = [(x_ref[hf, 0:SEGS, re], x_ref[hf, 0:SEGS, im])]
            for j in range(1, SR):
                xr, xi = _cmul(a1r, a1i, *local[-1])
                local.append((xr + x_ref[hf, j * SEGS:(j + 1) * SEGS, re], xi + x_ref[hf, j * SEGS:(j + 1) * SEGS, im]))
            er, ei = local[-1]

            for shift in (1, 2, 4):
                cr = jnp.where(row8 >= shift, pq_ref[hf, shift:shift + 1, re], 0.0)
                ci = jnp.where(row8 >= shift, pq_ref[hf, shift:shift + 1, im], 0.0)
                dr, di = _cmul(cr, ci, pltpu.roll(er, shift, 0), pltpu.roll(ei, shift, 0))
                er, ei = er + dr, ei + di
            c0r = carry_ref[hf, SEGS - 1:SEGS, re]
            c0i = carry_ref[hf, SEGS - 1:SEGS, im]
            sr, si = _cmul(pq_ref[hf, 0:SEGS, re], pq_ref[hf, 0:SEGS, im], c0r, c0i)
            sr = sr + jnp.where(row8 >= 1, pltpu.roll(er, 1, 0), 0.0)
            si = si + jnp.where(row8 >= 1, pltpu.roll(ei, 1, 0), 0.0)
            nr, ni = _cmul(pq_ref[hf, SEGS:2 * SEGS, re], pq_ref[hf, SEGS:2 * SEGS, im], c0r, c0i)
            carry_ref[hf, :, re] = er + nr
            carry_ref[hf, :, im] = ei + ni

            for j in range(SR):
                rows = slice(j * SEGS, (j + 1) * SEGS)
                dr, di = _cmul(pj_ref[hf, rows, re], pj_ref[hf, rows, im], sr, si)
                x_ref[hf, rows, re] = local[j][0] + dr
                x_ref[hf, rows, im] = local[j][1] + di

    y = jnp.concatenate(
        [jnp.dot(x_ref[hf].astype(BF16), wc_ref[hf], preferred_element_type=F32) for hf in range(2)], axis=-1)
    y = y + d_ref[...] * u
    z = jax.nn.gelu(y).astype(BF16)
    zz = jnp.dot(z, wglu_ref[...], preferred_element_type=F32)
    out = zz[:, :BRANCH_WIDTH] * _sigmoid(zz[:, BRANCH_WIDTH:])
    for lt in lane_tiles:
        for j in range(SR):
            rq_ref[lt, pl.ds(j, SEGS, stride=PITCH), :] = out[SEGS * j:SEGS * (j + 1), lt * 128:(lt + 1) * 128]
    for lt in lane_tiles:
        for s in range(SEGS):
            o_ref[SR * s:SR * (s + 1), lt * 128:(lt + 1) * 128] = rq_ref[lt, pl.ds(PITCH * s, SR), :].astype(o_ref.dtype)


def _s5(proj, wb, wc, pj, pq, d_skip, w_glu, *, tt=S5_TILE):
    _, B, L, _ = proj.shape
    W = BRANCH_WIDTH
    NS2 = 2 * S5_HALF_STATES
    return pl.pallas_call(
        functools.partial(_s5_kernel, tt=tt),
        grid=(B, L // tt),
        in_specs=[
            _slab_spec(SLAB_S5, tt),
            pl.BlockSpec((2, S5_HALF_CH, NS2), lambda b, t: (0, 0, 0)),
            pl.BlockSpec((2, NS2, S5_HALF_CH), lambda b, t: (0, 0, 0)),
            pl.BlockSpec((2, tt, NS2), lambda b, t: (0, 0, 0)),
            pl.BlockSpec((2, 2 * S5_SEGS, NS2), lambda b, t: (0, 0, 0)),
            pl.BlockSpec((1, W), lambda b, t: (0, 0)),
            pl.BlockSpec((W, 2 * W), lambda b, t: (0, 0)),
        ],
        out_specs=pl.BlockSpec((None, tt, W), lambda b, t: (b, t, 0)),
        out_shape=jax.ShapeDtypeStruct((B, L, W), BF16),
        scratch_shapes=[pltpu.VMEM((2, tt, NS2), F32),
                        pltpu.VMEM((W // 128, S5_SEGS * S5_PITCH, 128), F32),
                        pltpu.VMEM((2, S5_SEGS, NS2), F32)],
        compiler_params=_params(("parallel", "arbitrary")),
        name="s5",
    )(proj, wb, wc, pj, pq, d_skip.reshape(1, W), w_glu)


def _s5_tables(a_re, a_im, log_dt, b_re, b_im, c_re, c_im):
    dt = jnp.exp(log_dt)[:, None]
    mag = jnp.exp(a_re * dt)
    ab_re = mag * jnp.cos(a_im * dt)
    ab_im = mag * jnp.sin(a_im * dt)
    den = a_re * a_re + a_im * a_im
    zr = ((ab_re - 1.0) * a_re + ab_im * a_im) / den
    zi = (ab_im * a_re - (ab_re - 1.0) * a_im) / den
    bb_re = zr[..., None] * b_re - zi[..., None] * b_im
    bb_im = zr[..., None] * b_im + zi[..., None] * b_re
    GH = S5_GROUPS // 2
    eye = jnp.eye(GH, dtype=F32)

    def bdiag_in(m):
        m = m.reshape(2, GH, S5_STATE, S5_GROUP)
        return jnp.einsum('hgpc,gk->hgckp', m, eye).reshape(2, GH * S5_GROUP, GH * S5_STATE)

    def bdiag_out(m):
        m = m.reshape(2, GH, S5_GROUP, S5_STATE)
        return jnp.einsum('hgcp,gk->hgpkc', m, eye).reshape(2, GH * S5_STATE, GH * S5_GROUP)

    wb = jnp.concatenate([bdiag_in(bb_re), bdiag_in(bb_im)], axis=-1).astype(BF16)
    wc = jnp.concatenate([bdiag_out(c_re), -bdiag_out(c_im)], axis=1).astype(BF16)
    sr = S5_TILE // S5_SEGS

    def powers(n):
        n = jnp.asarray(n, F32)[:, None, None]
        mag = jnp.exp(n * (a_re * dt)[None])
        ang = n * (a_im * dt)[None]
        split = lambda p: p.reshape(p.shape[0], 2, S5_HALF_STATES).transpose(1, 0, 2)
        return jnp.concatenate([split(mag * jnp.cos(ang)), split(mag * jnp.sin(ang))], axis=-1)

    pj = powers(np.repeat(np.arange(1, sr + 1), S5_SEGS))
    pq = powers(np.concatenate([sr * np.arange(S5_SEGS), sr * np.arange(1, S5_SEGS + 1)]))
    return wb, wc, pj, pq


def _conv_kernel(a_ref, b_ref, w_ref, cb_ref, lg_ref, lbias_ref, o_ref, z_ref, y_ref, *, tt, rb):
    H = CONV_HALO

    @pl.when(pl.program_id(1) == 0)
    def _():
        z_ref[0:H, :] = jnp.zeros((H, BRANCH_WIDTH), F32)

    z_ref[H:H + tt, :] = a_ref[...].astype(F32) * _sigmoid(b_ref[...].astype(F32))
    cb = cb_ref[...]
    lg = lg_ref[...]
    lbias = lbias_ref[...]

    def block(k, carry):
        r0 = pl.multiple_of(k * rb, rb)
        for lc in range(0, BRANCH_WIDTH, 128):
            win = z_ref[pl.ds(r0, rb + H), lc:lc + 128]
            acc = [jnp.zeros((rb, 128), F32), jnp.zeros((rb, 128), F32)]
            for c in range(8):
                shifted = win if c == 0 else pltpu.roll(win, rb + H - c, 0)
                for off in range(c, H + 1, 8):
                    w = off - (H - CONV_WIDTH + 1)
                    if 0 <= w < CONV_WIDTH:
                        acc[w % 2] = acc[w % 2] + shifted[off - c:off - c + rb, :] * w_ref[w:w + 1, lc:lc + 128]
            y_ref[pl.ds(r0, rb), lc:lc + 128] = acc[0] + acc[1]
        return carry

    lax.fori_loop(0, tt // rb, block, 0)
    z_ref[0:H, :] = z_ref[tt:tt + H, :]

    acc = y_ref[...] + cb
    xc = acc - jnp.mean(acc, axis=-1, keepdims=True)
    var = jnp.mean(xc * xc, axis=-1, keepdims=True)
    y = xc * lax.rsqrt(var + NORM_EPS) * lg + lbias
    o_ref[...] = (y * _sigmoid(y)).astype(o_ref.dtype)


def _conv(proj, conv_w, conv_b, ln_g, ln_b, *, tt=512, rb=32):
    _, B, L, _ = proj.shape
    W = BRANCH_WIDTH
    w_pad = jnp.zeros((CONV_HALO, W), F32).at[:CONV_WIDTH].set(conv_w)
    vec = lambda: pl.BlockSpec((1, W), lambda b, t: (0, 0))
    return pl.pallas_call(
        functools.partial(_conv_kernel, tt=tt, rb=rb),
        grid=(B, L // tt),
        in_specs=[_slab_spec(SLAB_CONV_A, tt), _slab_spec(SLAB_CONV_B, tt),
                  pl.BlockSpec((CONV_HALO, W), lambda b, t: (0, 0)), vec(), vec(), vec()],
        out_specs=pl.BlockSpec((None, tt, W), lambda b, t: (b, t, 0)),
        out_shape=jax.ShapeDtypeStruct((B, L, W), BF16),
        scratch_shapes=[pltpu.VMEM((tt + CONV_HALO, W), F32), pltpu.VMEM((tt, W), F32)],
        compiler_params=_params(("parallel", "arbitrary")),
        name="conformer_conv",
    )(proj, proj, w_pad, conv_b.reshape(1, W), ln_g.reshape(1, W), ln_b.reshape(1, W))


def _rope_table_kernel(pos_ref, inv_ref, cos_ref, sin_ref):
    ang = pos_ref[...] * inv_ref[...]
    lane = lax.broadcasted_iota(jnp.int32, ang.shape, 1)
    cos_ref[...] = jnp.cos(ang)
    sin_ref[...] = jnp.where(lane < ROPE_HALF, -jnp.sin(ang), jnp.sin(ang))


def _rope_tables(positions, *, tm=1024):
    T = positions.size
    inv = ROPE_THETA ** (-jnp.arange(ROPE_HALF, dtype=F32) / ROPE_HALF)
    inv = jnp.concatenate([inv, inv, jnp.zeros((ATT_HEAD_DIM - ROPE_DIM,), F32)]).reshape(1, ATT_HEAD_DIM)
    pos = positions.astype(F32).reshape(T, 1)
    return pl.pallas_call(
        _rope_table_kernel,
        grid=(T // tm,),
        in_specs=[pl.BlockSpec((tm, 1), lambda i: (i, 0)), pl.BlockSpec((1, ATT_HEAD_DIM), lambda i: (0, 0))],
        out_specs=[pl.BlockSpec((tm, ATT_HEAD_DIM), lambda i: (i, 0))] * 2,
        out_shape=[jax.ShapeDtypeStruct((T, ATT_HEAD_DIM), F32)] * 2,
        compiler_params=_params(("parallel",)),
        name="rope_tables",
    )(pos, inv)


def _rope(x, cos, sin):
    lane = lax.broadcasted_iota(jnp.int32, x.shape, 1)
    partner = jnp.where(lane < ROPE_HALF, pltpu.roll(x, ATT_HEAD_DIM - ROPE_HALF, 1), pltpu.roll(x, ROPE_HALF, 1))
    return x * cos + partner * sin


DEINT_STRIDE = 4


def _qkv_kernel(h_ref, wq_ref, wk_ref, wv_ref, cos_ref, sin_ref, o_ref, w16_ref, s_ref, t_ref, *, d):
    tm = h_ref.shape[0]

    @pl.when(pl.program_id(0) == 0)
    def _():
        for j, w_ref in enumerate((wq_ref, wk_ref, wv_ref)):
            w16_ref[j] = w_ref[...].astype(BF16)

    h16 = h_ref[...]
    cos, sin = cos_ref[...], sin_ref[...]
    for j in range(3):
        res = jnp.dot(h16, w16_ref[j], preferred_element_type=F32)
        for h in range(ATT_HEADS_PER_GROUP):
            hs = slice(h * ATT_HEAD_DIM, (h + 1) * ATT_HEAD_DIM)
            s_ref[j, h] = _rope(res[:, hs], cos, sin) if j < 2 else res[:, hs]
            if d <= DEINT_STRIDE:
                for r in range(d):
                    o_ref[j, r, :, hs] = s_ref[j, h, pl.ds(r, tm // d, stride=d), :].astype(o_ref.dtype)
            else:
                q = tm // DEINT_STRIDE
                for r0 in range(DEINT_STRIDE):
                    t_ref[j, h, pl.ds(r0 * q, q), :] = s_ref[j, h, pl.ds(r0, q, stride=DEINT_STRIDE), :]
                for r0 in range(DEINT_STRIDE):
                    for r1 in range(d // DEINT_STRIDE):
                        o_ref[j, DEINT_STRIDE * r1 + r0, :, hs] = t_ref[
                            j, h, pl.ds(r0 * q + r1, tm // d, stride=d // DEINT_STRIDE), :].astype(o_ref.dtype)


def _qkv(h, w_in, layer, cos_t, sin_t, gi, d, B, *, tm=1024):
    T, D = h.shape
    L = T // B
    tiles = L // tm
    W = ATT_HEADS_PER_GROUP * ATT_HEAD_DIM
    assert d <= DEINT_STRIDE or d == DEINT_STRIDE * DEINT_STRIDE
    stage = (3, ATT_HEADS_PER_GROUP, tm, ATT_HEAD_DIM)
    w_spec = lambda j: pl.BlockSpec((None, D, SLAB), lambda i: (layer, 0, W_SLAB_Q + gi + 3 * j))
    return pl.pallas_call(
        functools.partial(_qkv_kernel, d=d),
        grid=(T // tm,),
        in_specs=[
            pl.BlockSpec((tm, D), lambda i: (i, 0)),
            w_spec(0), w_spec(1), w_spec(2),
            pl.BlockSpec((tm, ATT_HEAD_DIM), lambda i: (i, 0)),
            pl.BlockSpec((tm, ATT_HEAD_DIM), lambda i: (i, 0)),
        ],
        out_specs=pl.BlockSpec((3, None, d, tm // d, W), lambda i: (0, i // tiles, 0, i % tiles, 0)),
        out_shape=jax.ShapeDtypeStruct((3, B, d, L // d, W), BF16),
        scratch_shapes=[pltpu.VMEM((3, D, SLAB), BF16),
                        pltpu.VMEM(stage, F32),
                        pltpu.VMEM(stage if d > DEINT_STRIDE else (1, 1, 8, ATT_HEAD_DIM), F32)],
        compiler_params=_params(("arbitrary",)),
        name=f"qkv_proj_g{gi}",
    )(h, w_in, w_in, w_in, cos_t, sin_t)


ATT_UNROLL = 8


def _att_kernel(q_ref, kc_ref, kp_ref, vc_ref, vp_ref, o_ref, lse_ref, *, d, nq):
    Q = ATT_QBLOCK
    n = pl.program_id(1)
    qi = lax.broadcasted_iota(jnp.int32, (Q, 2 * Q), 0)
    kj = lax.broadcasted_iota(jnp.int32, (Q, 2 * Q), 1)
    rel = Q + qi - kj
    band = (rel >= 0) & (rel <= Q)
    band_first = band & ((n > 0) | (kj >= Q))
    head_of_lane = lax.broadcasted_iota(jnp.int32, (Q, ATT_HEAD_DIM), 1) // (ATT_HEAD_DIM // ATT_HEADS_PER_GROUP)
    nt = (((1,), (1,)), ((), ()))

    def block(r, i):
        rows = slice(i * Q, (i + 1) * Q)
        outs = []
        lse = jnp.zeros((Q, ATT_HEAD_DIM), F32)
        for h in range(ATT_HEADS_PER_GROUP):
            hs = slice(h * ATT_HEAD_DIM, (h + 1) * ATT_HEAD_DIM)
            if i == 0:
                k = jnp.concatenate([kp_ref[r, :, hs], kc_ref[r, rows, hs]], axis=0)
                v = jnp.concatenate([vp_ref[r, :, hs], vc_ref[r, rows, hs]], axis=0)
            else:
                k = kc_ref[r, (i - 1) * Q:(i + 1) * Q, hs]
                v = vc_ref[r, (i - 1) * Q:(i + 1) * Q, hs]
            s = lax.dot_general(q_ref[r, rows, hs], k, nt, preferred_element_type=F32) * (ATT_HEAD_DIM ** -0.5 * LOG2E)
            s = jnp.where(band_first if i == 0 else band, s, MASK_VALUE)
            m = jnp.max(s, axis=-1, keepdims=True)
            p = jnp.exp2(s - m)
            l = jnp.sum(p, axis=-1, keepdims=True)
            outs.append(jnp.dot(p.astype(BF16), v, preferred_element_type=F32) / l)
            lse = jnp.where(head_of_lane == h, m * LN2 + jnp.log(l), lse)
        o_ref[r, rows, :] = jnp.concatenate(outs, axis=-1).astype(o_ref.dtype)
        lse_ref[r, rows, :] = lse

    if d <= ATT_UNROLL:
        for r in range(d):
            for i in range(nq):
                block(r, i)
    else:
        def residues(it, carry):
            for u in range(ATT_UNROLL):
                for i in range(nq):
                    block(it * ATT_UNROLL + u, i)
            return carry

        lax.fori_loop(0, d // ATT_UNROLL, residues, 0)


def _attention_group(qkv, gi):
    _, B, d, n_sub, W = qkv.shape
    Q = ATT_QBLOCK
    nq = max(1, ATT_UNROLL // d)

    def cur(s):
        return pl.BlockSpec((None, None, d, nq * Q, W), lambda b, n: (s, b, 0, n, 0))

    def prev(s):
        return pl.BlockSpec((None, None, d, Q, W), lambda b, n: (s, b, 0, jnp.maximum(nq * n - 1, 0), 0))

    return pl.pallas_call(
        functools.partial(_att_kernel, d=d, nq=nq),
        grid=(B, n_sub // (nq * Q)),
        in_specs=[cur(0), cur(1), prev(1), cur(2), prev(2)],
        out_specs=[pl.BlockSpec((None, d, nq * Q, W), lambda b, n: (b, 0, n, 0)),
                   pl.BlockSpec((None, d, nq * Q, ATT_HEAD_DIM), lambda b, n: (b, 0, n, 0))],
        out_shape=[jax.ShapeDtypeStruct((B, d, n_sub, W), BF16),
                   jax.ShapeDtypeStruct((B, d, n_sub, ATT_HEAD_DIM), F32)],
        compiler_params=_params(("parallel", "arbitrary")),
        name=f"dilated_attention_g{gi}",
    )(qkv, qkv, qkv, qkv, qkv)


def _merge_kernel(x_ref, ya_ref, yb_ref, yc_ref, o1_ref, o2_ref, o3_ref, l1_ref, l2_ref, l3_ref,
                  g0_ref, g1_ref, g2_ref, g3_ref, g4_ref, g5_ref, g6_ref, g7_ref, wb_ref, wo_ref, out_ref,
                  os_ref, ls_ref):
    for g, (o_ref, l_ref) in enumerate(((o1_ref, l1_ref), (o2_ref, l2_ref), (o3_ref, l3_ref))):
        d, rows = o_ref.shape[0], o_ref.shape[1]
        for r in range(d):
            o = o_ref[r].astype(F32)
            for h in range(ATT_HEADS_PER_GROUP):
                os_ref[g, h, pl.ds(r, rows, stride=d), :] = o[:, h * ATT_HEAD_DIM:(h + 1) * ATT_HEAD_DIM]
            ls_ref[g, pl.ds(r, rows, stride=d), :] = l_ref[r]
    lanes = ATT_HEAD_DIM // ATT_HEADS_PER_GROUP
    parts = []
    for h in range(ATT_HEADS_PER_GROUP):
        l1, l2, l3 = (ls_ref[g, :, h * lanes:h * lanes + 1] for g in range(3))
        m = jnp.maximum(jnp.maximum(l1, l2), l3)
        e1, e2, e3 = jnp.exp(l1 - m), jnp.exp(l2 - m), jnp.exp(l3 - m)
        parts.append((e1 * os_ref[0, h] + e2 * os_ref[1, h] + e3 * os_ref[2, h]) / (e1 + e2 + e3))
    yd = jnp.concatenate(parts, axis=-1)
    ys = (ya_ref[...], yb_ref[...], yc_ref[...], yd.astype(BF16))
    gates = ((g0_ref, g1_ref), (g2_ref, g3_ref), (g4_ref, g5_ref), (g6_ref, g7_ref))
    merged = None
    for k in range(N_BRANCHES):
        gate = _sigmoid(jnp.concatenate([gates[k][0][...], gates[k][1][...]], axis=-1).astype(F32))
        term = gate * jnp.dot(ys[k], wb_ref[k], preferred_element_type=F32)
        merged = term if merged is None else merged + term
    out_ref[...] = x_ref[...] + jnp.dot(merged.astype(BF16), wo_ref[...], preferred_element_type=F32)


def _merge(x, ya, yb, yc, att, proj, w_branch, w_out, *, tm=512):
    B, L, D = x.shape
    W = BRANCH_WIDTH
    (o1, l1), (o2, l2), (o3, l3) = att
    row = lambda: pl.BlockSpec((None, tm, W), lambda b, t: (b, t, 0))

    def res(a):
        d, last = a.shape[1], a.shape[3]
        return pl.BlockSpec((None, d, tm // d, last), lambda b, t: (b, 0, t, 0))

    return pl.pallas_call(
        _merge_kernel,
        grid=(B, L // tm),
        in_specs=[pl.BlockSpec((None, tm, D), lambda b, t: (b, t, 0))] + [row() for _ in range(3)]
                 + [res(a) for a in (o1, o2, o3, l1, l2, l3)]
                 + [_slab_spec(SLAB_GATE + s, tm) for s in range(8)]
                 + [pl.BlockSpec((N_BRANCHES, W, D), lambda b, t: (0, 0, 0)), pl.BlockSpec((D, D), lambda b, t: (0, 0))],
        out_specs=pl.BlockSpec((None, tm, D), lambda b, t: (b, t, 0)),
        out_shape=jax.ShapeDtypeStruct((B, L, D), F32),
        scratch_shapes=[pltpu.VMEM((3, ATT_HEADS_PER_GROUP, tm, ATT_HEAD_DIM), F32),
                        pltpu.VMEM((3, tm, ATT_HEAD_DIM), F32)],
        compiler_params=_params(("parallel", "parallel")),
        name="gated_merge",
    )(x, ya, yb, yc, o1, o2, o3, l1, l2, l3, *([proj] * 8), w_branch, w_out)


def kernel(x, positions, ffn1_norm, ffn1_w_gate, ffn1_w_up, ffn1_w_down, mix_norm, w_in, hg_lb_logits, hg_gnorm, s5_a_re, s5_a_im, s5_log_dt, s5_b_re, s5_b_im, s5_c_re, s5_c_im, s5_d, s5_w_glu, conv_w, conv_b, conv_ln_g, conv_ln_b, w_branch, w_out, ffn2_norm, ffn2_w_gate, ffn2_w_up, ffn2_w_down, final_norm):
    B, L, D = x.shape
    T = B * L
    depth = w_in.shape[0]
    lb_soft = jax.nn.softmax(hg_lb_logits.astype(F32), axis=0)
    lb_all = jnp.cumsum(lb_soft, axis=0) - lb_soft[0]
    cos_t, sin_t = _rope_tables(positions)
    bf = lambda w: w.astype(BF16)

    ffn1_w = [_to_bf16(w) for w in (ffn1_w_gate, ffn1_w_up, ffn1_w_down)]
    ffn2_w = [_to_bf16(w) for w in (ffn2_w_gate, ffn2_w_up, ffn2_w_down)]

    xt = x.reshape(T, D)
    for l in range(depth):
        xt, h_mix = _ffn(xt, ffn1_norm[l], *ffn1_w, l, next_gain=mix_norm[l])
        proj = _proj(h_mix, w_in, l).reshape(N_SLABS, B, L, SLAB)
        ya = _hgrn(proj, lb_all[l], hg_gnorm[l])
        wb, wc, pj, pq = _s5_tables(s5_a_re[l], s5_a_im[l], s5_log_dt[l], s5_b_re[l], s5_b_im[l], s5_c_re[l], s5_c_im[l])
        yb = _s5(proj, wb, wc, pj, pq, s5_d[l], bf(s5_w_glu[l]))
        yc = _conv(proj, conv_w[l], conv_b[l], conv_ln_g[l], conv_ln_b[l])
        att = [_attention_group(_qkv(h_mix, w_in, l, cos_t, sin_t, gi, dil, B), gi)
               for gi, (_, dil) in enumerate(ATT_CONFIGS)]
        xt = _merge(xt.reshape(B, L, D), ya, yb, yc, att, proj, bf(w_branch[l]), bf(w_out[l])).reshape(T, D)
        last = l == depth - 1
        xt = _ffn(xt, ffn2_norm[l], *ffn2_w, l, final_gain=final_norm if last else None)
    return xt.reshape(B, L, D)
```
